```python
import jax, jax.numpy as jnp
from jax import lax
import numpy as np

D_MODEL = 1024
BATCH = 8
SEQ = 2048
DEPTH = 1
DEC_BATCH = 128
DEC_SEQ = 1
PAST_LEN = 16384
PAGE_SIZE = 128

CHUNK = 128
A_HEADS = 8
A_HEAD_DIM = D_MODEL // A_HEADS
A_WIDTH = A_HEADS * A_HEAD_DIM
B_HEAD_DIM = 64
B_HEADS = D_MODEL // B_HEAD_DIM
B_WIDTH = B_HEADS * B_HEAD_DIM
W_LORA = 64
A_LORA = 64
G_LORA = 128
SHIFT_WIDTH = 3 * B_WIDTH + W_LORA + A_LORA + G_LORA
OFF_U = 0
OFF_V = A_WIDTH
OFF_SHIFT = 2 * A_WIDTH
OFF_GATE_A = OFF_SHIFT + SHIFT_WIDTH
OFF_GATE_B = OFF_GATE_A + D_MODEL
IN_COLS = OFF_GATE_B + D_MODEL
N_GROUPS = 4
EXPERTS_PER_GROUP = 8
N_EXPERTS = N_GROUPS * EXPERTS_PER_GROUP
TOP_K = 2
D_EXPERT = 256
RMS_EPS = 1e-6
LN_EPS = 1e-5
GN_EPS = 64e-5

kernel_name = 'hybrid_sgu_rwkv7_hmoe_step'


def rms_norm(x, g):
    xf = x.astype(jnp.float32)
    y = xf * lax.rsqrt(jnp.mean(xf * xf, axis=-1, keepdims=True) + RMS_EPS)
    return (y * g.astype(jnp.float32)).astype(x.dtype)


def head_norm(x, w, b, eps):
    xf = x.astype(jnp.float32)
    mu = jnp.mean(xf, axis=-1, keepdims=True)
    xc = xf - mu
    var = jnp.mean(xc * xc, axis=-1, keepdims=True)
    y = xc * lax.rsqrt(var + eps) * w.astype(jnp.float32) + b.astype(jnp.float32)
    return y.astype(x.dtype)


def spatial_gating(v, w_s, b_s):
    bsz, t_len = v.shape[0], v.shape[1]
    n_chunks = -(-t_len // CHUNK)
    pad = n_chunks * CHUNK - t_len
    vp = jnp.pad(v, ((0, 0), (0, pad), (0, 0), (0, 0))).reshape(bsz, n_chunks, CHUNK, A_HEADS, A_HEAD_DIM)
    causal = jnp.tril(jnp.ones((CHUNK, CHUNK), dtype=bool))
    w_c = jnp.where(causal[None], w_s, 0.0)
    s = jnp.einsum('hts,bcshn->bcthn', w_c, vp) + jnp.transpose(b_s)[None, None, :, :, None]
    return s.reshape(bsz, n_chunks * CHUNK, A_HEADS, A_HEAD_DIM)[:, :t_len]


def wkv7_scan(s0, r, w, k, v, a, b):
    def step(s, inp):
        r_t, w_t, k_t, v_t, a_t, b_t = inp
        sa = jnp.einsum('bhij,bhj->bhi', s, a_t)
        s = s * w_t[:, :, None, :] + sa[..., :, None] * b_t[..., None, :] + v_t[..., :, None] * k_t[..., None, :]
        y = jnp.einsum('bhij,bhj->bhi', s, r_t)
        return s, y
    xs = (jnp.moveaxis(r, 1, 0), jnp.moveaxis(w, 1, 0), jnp.moveaxis(k, 1, 0),
          jnp.moveaxis(v, 1, 0), jnp.moveaxis(a, 1, 0), jnp.moveaxis(b, 1, 0))
    s_last, ys = lax.scan(step, s0, xs)
    return jnp.moveaxis(ys, 0, 1), s_last


def token_mixers(xn, wkv0, shift0, w_in, sgu_ln_w, sgu_ln_b, sgu_w_s, sgu_b, rwkv_mu,
                 w_lora_up, w_bias, a_lora_up, a_bias, g_lora_up, k_k, k_a, r_k,
                 lnx_w, lnx_b, proj_a, proj_b, w_out):
    bsz, t_len = xn.shape[0], xn.shape[1]
    f32 = jnp.float32
    proj = jnp.einsum('btd,dc->btc', xn, w_in)
    u = jax.nn.gelu(proj[..., OFF_U:OFF_V])
    v = jax.nn.gelu(proj[..., OFF_V:OFF_SHIFT]).reshape(bsz, t_len, A_HEADS, A_HEAD_DIM)
    cols = proj[..., OFF_SHIFT:OFF_GATE_A]
    gate_a = jax.nn.sigmoid(proj[..., OFF_GATE_A:OFF_GATE_B])
    gate_b = jax.nn.sigmoid(proj[..., OFF_GATE_B:IN_COLS])

    v_n = head_norm(v, sgu_ln_w, sgu_ln_b, LN_EPS)
    y_a = u * spatial_gating(v_n, sgu_w_s, sgu_b).reshape(bsz, t_len, A_WIDTH)

    prev = jnp.concatenate([shift0[:, None, :].astype(cols.dtype), cols[:, :-1]], axis=1)
    xs = cols + (prev - cols) * rwkv_mu
    o = 3 * B_WIDTH
    r = xs[..., 0:B_WIDTH]
    k = xs[..., B_WIDTH:2 * B_WIDTH]
    vb = xs[..., 2 * B_WIDTH:o]
    w_d = xs[..., o:o + W_LORA]
    a_d = xs[..., o + W_LORA:o + W_LORA + A_LORA]
    g_d = xs[..., o + W_LORA + A_LORA:SHIFT_WIDTH]
    w_log = -jax.nn.softplus(-(w_bias + jnp.tanh(w_d) @ w_lora_up)) - 0.5
    decay = jnp.exp(-jnp.exp(w_log.astype(f32)))
    a = jax.nn.sigmoid(a_bias + a_d @ a_lora_up)
    g = jax.nn.sigmoid(g_d) @ g_lora_up
    hs = (bsz, t_len, B_HEADS, B_HEAD_DIM)
    kk = (k * k_k).astype(f32).reshape(hs)
    kk = kk / jnp.maximum(jnp.sqrt(jnp.sum(kk * kk, axis=-1, keepdims=True)), 1e-12)
    k = k * (1.0 + (a - 1.0) * k_a)
    r_h = r.astype(f32).reshape(hs)
    k_h = k.astype(f32).reshape(hs)
    v_h = vb.astype(f32).reshape(hs)
    a_h = a.astype(f32).reshape(hs)
    y, wkv_new = wkv7_scan(wkv0.astype(f32), r_h, decay.reshape(hs), k_h, v_h, -kk, kk * a_h)
    y = head_norm(y, lnx_w, lnx_b, GN_EPS)
    bonus = jnp.sum(r_h * k_h * r_k.astype(f32), axis=-1, keepdims=True) * v_h
    y_b = (y + bonus).astype(xn.dtype).reshape(bsz, t_len, B_WIDTH) * g

    merged = gate_a * (y_a @ proj_a) + gate_b * (y_b @ proj_b)
    out = merged @ w_out
    return out, wkv_new.astype(wkv0.dtype), cols[:, -1], v_n


def hier_moe(x2d, router_group, router_group_bias, router_expert, router_expert_bias,
             moe_w_gate, moe_w_up, moe_w_down):
    f32 = jnp.float32
    lg = (x2d @ router_group + router_group_bias).astype(f32)
    grp = jnp.argmax(lg, axis=-1)
    p_grp = jnp.max(jax.nn.softmax(lg, axis=-1), axis=-1, keepdims=True)
    le = (x2d @ router_expert + router_expert_bias).astype(f32).reshape(-1, N_GROUPS, EXPERTS_PER_GROUP)
    le_g = jnp.einsum('tg,tge->te', jax.nn.one_hot(grp, N_GROUPS, dtype=f32), le)
    top_v, top_i = lax.top_k(le_g, TOP_K)
    weights = p_grp * jax.nn.softmax(top_v, axis=-1)
    eid = grp[:, None] * EXPERTS_PER_GROUP + top_i
    combine = jnp.sum(jax.nn.one_hot(eid, N_EXPERTS, dtype=f32) * weights[..., None], axis=1).astype(x2d.dtype)
    y = jnp.zeros_like(x2d)
    for e in range(N_EXPERTS):
        h = jax.nn.silu(x2d @ moe_w_gate[e]) * (x2d @ moe_w_up[e])
        y = y + combine[:, e:e + 1] * (h @ moe_w_down[e])
    return y


def setup_inputs(seed: int = 0) -> dict:
    key = jax.random.key(seed)
    ks = iter(jax.random.split(key, 40))
    nrm = lambda shape, scale: scale * jax.random.normal(next(ks), shape, jnp.float32)
    L = DEPTH
    decay_speed = jnp.tile(jnp.linspace(-6.5, -1.5, B_WIDTH, dtype=jnp.float32)[None], (L, 1))
    return {
        'x_prompt': nrm((BATCH, SEQ, D_MODEL), 1.0),
        'x_sample': nrm((DEC_BATCH, DEC_SEQ, D_MODEL), 1.0),
        'state_wkv': nrm((L, DEC_BATCH, B_HEADS, B_HEAD_DIM, B_HEAD_DIM), 0.5),
        'state_shift': nrm((L, DEC_BATCH, SHIFT_WIDTH), 1.0),
        'norm_mix_w': 1.0 + nrm((L, D_MODEL), 0.05),
        'w_in': nrm((L, D_MODEL, IN_COLS), D_MODEL ** -0.5),
        'sgu_ln_w': 1.0 + nrm((L, A_HEADS, A_HEAD_DIM), 0.05),
        'sgu_ln_b': nrm((L, A_HEADS, A_HEAD_DIM), 0.01),
        'sgu_w_s': nrm((L, A_HEADS, CHUNK, CHUNK), 0.5 * CHUNK ** -0.5),
        'sgu_b': 1.0 + nrm((L, A_HEADS, CHUNK), 0.05),
        'rwkv_mu': jax.random.uniform(next(ks), (L, SHIFT_WIDTH), jnp.float32),
        'w_lora_up': nrm((L, W_LORA, B_WIDTH), 0.5 * W_LORA ** -0.5),
        'w_bias': decay_speed + nrm((L, B_WIDTH), 0.1),
        'a_lora_up': nrm((L, A_LORA, B_WIDTH), 0.5 * A_LORA ** -0.5),
        'a_bias': nrm((L, B_WIDTH), 0.1),
        'g_lora_up': nrm((L, G_LORA, B_WIDTH), G_LORA ** -0.5),
        'k_k': 0.85 + nrm((L, B_WIDTH), 0.05),
        'k_a': 1.0 + nrm((L, B_WIDTH), 0.05),
        'r_k': nrm((L, B_HEADS, B_HEAD_DIM), 0.1),
        'lnx_w': 1.0 + nrm((L, B_HEADS, B_HEAD_DIM), 0.05),
        'lnx_b': nrm((L, B_HEADS, B_HEAD_DIM), 0.01),
        'proj_a': nrm((L, A_WIDTH, D_MODEL), A_WIDTH ** -0.5),
        'proj_b': nrm((L, B_WIDTH, D_MODEL), B_WIDTH ** -0.5),
        'w_out': nrm((L, D_MODEL, D_MODEL), D_MODEL ** -0.5),
        'norm_ffn_w': 1.0 + nrm((L, D_MODEL), 0.05),
        'router_group': nrm((L, D_MODEL, N_GROUPS), D_MODEL ** -0.5),
        'router_group_bias': nrm((L, N_GROUPS), 0.01),
        'router_expert': nrm((L, D_MODEL, N_EXPERTS), D_MODEL ** -0.5),
        'router_expert_bias': nrm((L, N_EXPERTS), 0.01),
        'moe_w_gate': nrm((L, N_EXPERTS, D_MODEL, D_EXPERT), D_MODEL ** -0.5),
        'moe_w_up': nrm((L, N_EXPERTS, D_MODEL, D_EXPERT), D_MODEL ** -0.5),
        'moe_w_down': nrm((L, N_EXPERTS, D_EXPERT, D_MODEL), D_EXPERT ** -0.5),
        'norm_final_w': 1.0 + nrm((D_MODEL,), 0.05),
    }


def reference(x_prompt, x_sample, state_wkv, state_shift, norm_mix_w, w_in, sgu_ln_w, sgu_ln_b,
              sgu_w_s, sgu_b, rwkv_mu, w_lora_up, w_bias, a_lora_up, a_bias, g_lora_up, k_k, k_a,
              r_k, lnx_w, lnx_b, proj_a, proj_b, w_out, norm_ffn_w, router_group, router_group_bias,
              router_expert, router_expert_bias, moe_w_gate, moe_w_up, moe_w_down, norm_final_w):
    xp, xs = x_prompt, x_sample
    bp, tp = xp.shape[0], xp.shape[1]
    n_prompt_tokens = bp * tp
    wkv_p_list, shift_p_list, wkv_s_list, shift_s_list, v_s_list = [], [], [], [], []
    for l in range(DEPTH):
        mix_w = (w_in[l], sgu_ln_w[l], sgu_ln_b[l], sgu_w_s[l], sgu_b[l], rwkv_mu[l], w_lora_up[l],
                 w_bias[l], a_lora_up[l], a_bias[l], g_lora_up[l], k_k[l], k_a[l], r_k[l],
                 lnx_w[l], lnx_b[l], proj_a[l], proj_b[l], w_out[l])
        wkv0_p = jnp.zeros((bp, B_HEADS, B_HEAD_DIM, B_HEAD_DIM), xp.dtype)
        shift0_p = jnp.zeros((bp, SHIFT_WIDTH), xp.dtype)
        out_p, wkv_p, shift_p, _ = token_mixers(rms_norm(xp, norm_mix_w[l]), wkv0_p, shift0_p, *mix_w)
        out_s, wkv_s, shift_s, v_s = token_mixers(rms_norm(xs, norm_mix_w[l]), state_wkv[l], state_shift[l], *mix_w)
        xp = xp + out_p
        xs = xs + out_s
        tokens = jnp.concatenate([rms_norm(xp, norm_ffn_w[l]).reshape(-1, D_MODEL),
                                  rms_norm(xs, norm_ffn_w[l]).reshape(-1, D_MODEL)], axis=0)
        f = hier_moe(tokens, router_group[l], router_group_bias[l], router_expert[l],
                     router_expert_bias[l], moe_w_gate[l], moe_w_up[l], moe_w_down[l])
        xp = xp + f[:n_prompt_tokens].reshape(xp.shape)
        xs = xs + f[n_prompt_tokens:].reshape(xs.shape)
        wkv_p_list.append(wkv_p)
        shift_p_list.append(shift_p)
        wkv_s_list.append(wkv_s)
        shift_s_list.append(shift_s)
        v_s_list.append(v_s)
    y_prompt = rms_norm(xp, norm_final_w)
    y_sample = rms_norm(xs, norm_final_w)
    return (y_prompt, y_sample, jnp.stack(wkv_p_list), jnp.stack(shift_p_list),
            jnp.stack(wkv_s_list), jnp.stack(shift_s_list), jnp.stack(v_s_list))
```

```python
import functools
import math

import jax
import jax.numpy as jnp
from jax import lax
from jax.experimental import pallas as pl
from jax.experimental.pallas import tpu as pltpu

F32 = jnp.float32
BF16 = jnp.bfloat16
HIGHEST = lax.Precision.HIGHEST

D_MODEL = 1024
CHUNK = 128
A_HEADS = 8
A_HEAD_DIM = 128
B_HEADS = 16
B_HEAD_DIM = 64
N_PAIRS = B_HEADS // 2
SHIFT_WIDTH = 3328
OFF_V = 1024
OFF_SHIFT = 2048
OFF_GATE_A = OFF_SHIFT + SHIFT_WIDTH
OFF_GATE_B = OFF_GATE_A + D_MODEL
IN_COLS = OFF_GATE_B + D_MODEL
N_GROUPS = 4
EXPERTS_PER_GROUP = 8
N_EXPERTS = 32
D_EXPERT = 256
ROUTER_LANES = 128
EXPERT_LANE0 = N_GROUPS
RMS_EPS = 1e-6
LN_EPS = 1e-5
GN_EPS = 64e-5

SCAN_CHUNK = 64
VMEM_LIMIT = 56 * 1024 * 1024


def _gelu(x):
    return x * (0.5 * (1.0 + jnp.tanh(math.sqrt(2.0 / math.pi) * (x + 0.044715 * (x * x * x)))))


def _sigmoid(x):
    return 1.0 / (1.0 + jnp.exp(-x))


def _softplus(z):
    return jnp.maximum(z, 0.0) + jnp.log(1.0 + jnp.exp(-jnp.abs(z)))


def _rms_norm(x, g):
    return x * lax.rsqrt(jnp.mean(x * x, axis=-1, keepdims=True) + RMS_EPS) * g


def _dot(a, b):
    return jnp.dot(a, b, preferred_element_type=F32)


def _dot_hi(a, b):
    return jnp.dot(a, b, preferred_element_type=F32, precision=HIGHEST)


def _head_sums(z, ones2):
    outs = []
    for p in range(N_PAIRS):
        zp = z[:, p * 128:(p + 1) * 128]
        hi = zp.astype(BF16)
        lo = (zp - hi.astype(F32)).astype(BF16)
        outs.append(_dot(jnp.concatenate([hi, lo], axis=1), ones2))
    return jnp.concatenate(outs, axis=1)


def _front_kernel(is_sample, tm, *refs):
    if is_sample:
        (x_ref, prev_ref, nw_ref, win_ref, lnw_ref, lnb_ref, ws0_ref, bs0_ref, mu_ref, wl_ref, wb_ref,
         al_ref, ab_ref, gl_ref, kk_ref, ka_ref, rk_ref, pa_ref, ones2_ref,
         outa_ref, gateb_ref, g_ref, bonus_ref, r_ref, lw_ref, k_ref, v_ref, kn_ref, bs_ref,
         cols_ref, vn_ref, ya_ref) = refs
    else:
        (x_ref, nw_ref, win_ref, lnw_ref, lnb_ref, wsc_ref, bsb_ref, mu_ref, wl_ref, wb_ref,
         al_ref, ab_ref, gl_ref, kk_ref, ka_ref, rk_ref, pa_ref, ones2_ref,
         outa_ref, gateb_ref, g_ref, bonus_ref, r_ref, lw_ref, k_ref, v_ref, kn_ref, bs_ref,
         last_ref, ya_ref, carry_ref) = refs

    xb = _rms_norm(x_ref[...], nw_ref[...]).astype(BF16)

    def proj(lo, hi):
        return _dot(xb, win_ref[:, lo:hi])

    for h in range(A_HEADS):
        cs = slice(h * A_HEAD_DIM, (h + 1) * A_HEAD_DIM)
        u = _gelu(proj(h * A_HEAD_DIM, (h + 1) * A_HEAD_DIM))
        v = _gelu(proj(OFF_V + h * A_HEAD_DIM, OFF_V + (h + 1) * A_HEAD_DIM))
        mean = jnp.mean(v, axis=-1, keepdims=True)
        vc = v - mean
        var = jnp.mean(vc * vc, axis=-1, keepdims=True)
        vn = vc * lax.rsqrt(var + LN_EPS) * lnw_ref[:, cs] + lnb_ref[:, cs]
        if is_sample:
            vn_ref[:, cs] = vn
            ya_ref[:, cs] = (u * (vn * ws0_ref[:, cs] + bs0_ref[:, cs])).astype(BF16)
        else:
            vnb = vn.astype(BF16)
            for c in range(tm // CHUNK):
                rs = slice(c * CHUNK, (c + 1) * CHUNK)
                s = _dot(wsc_ref[h], vnb[rs, :]) + bsb_ref[:, cs]
                ya_ref[rs, cs] = (u[rs, :] * s).astype(BF16)
    outa_ref[...] = _sigmoid(proj(OFF_GATE_A, OFF_GATE_B)) * _dot(ya_ref[...], pa_ref[...])
    gateb_ref[...] = _sigmoid(proj(OFF_GATE_B, IN_COLS))

    cols = proj(OFF_SHIFT, OFF_GATE_A)
    if is_sample:
        prev = prev_ref[...]
        cols_ref[...] = cols
    else:
        @pl.when(pl.program_id(1) == 0)
        def _():
            carry_ref[...] = jnp.zeros_like(carry_ref)

        rolled = pltpu.roll(cols, 1, 0)
        row = lax.broadcasted_iota(jnp.int32, cols.shape, 0)
        prev = jnp.where(row == 0, carry_ref[0:1, :], rolled)
        carry_ref[0:1, :] = cols[tm - 1:tm, :]
        last_ref[...] = cols[tm - 1:tm, :]
    xs = cols + (prev - cols) * mu_ref[...]
    r = xs[:, 0:1024]
    k = xs[:, 1024:2048]
    vb = xs[:, 2048:3072]
    wa = xs[:, 3072:3200]
    gd = xs[:, 3200:3328]
    w_log = -_softplus(-(wb_ref[...] + _dot(jnp.tanh(wa).astype(BF16), wl_ref[...]))) - 0.5
    logw = -jnp.exp(w_log)
    a = _sigmoid(ab_ref[...] + _dot(wa.astype(BF16), al_ref[...]))
    g_ref[...] = _dot(_sigmoid(gd).astype(BF16), gl_ref[...])
    ones2 = ones2_ref[...]
    kk = k * kk_ref[...]
    kkn = kk / jnp.maximum(jnp.sqrt(_head_sums(kk * kk, ones2)), 1e-12)
    k2 = k * (1.0 + (a - 1.0) * ka_ref[...])
    bonus_ref[...] = _head_sums(r * k2 * rk_ref[...], ones2) * vb
    bsc = kkn * a
    if is_sample:
        r_ref[...] = r
        lw_ref[...] = logw
        k_ref[...] = k2
        v_ref[...] = vb
        kn_ref[...] = kkn
        bs_ref[...] = bsc
    else:
        for p in range(N_PAIRS):
            ps = slice(p * 128, (p + 1) * 128)
            r_ref[p] = r[:, ps]
            lw_ref[p] = logw[:, ps]
            k_ref[p] = k2[:, ps]
            v_ref[p] = vb[:, ps]
            kn_ref[p] = kkn[:, ps]
            bs_ref[p] = bsc[:, ps]


def _const_spec(shape, grid_rank):
    zeros = (0,) * len(shape)
    if grid_rank == 1:
        return pl.BlockSpec(shape, lambda i: zeros, pipeline_mode=pl.Buffered(1))
    return pl.BlockSpec(shape, lambda b, i: zeros, pipeline_mode=pl.Buffered(1))


def _front_prompt(x, w, tm):
    bsz, t_len, _ = x.shape
    nt = t_len // tm
    consts = [w["nw"], w["win"], w["lnw"], w["lnb"], w["wsc"], w["bsb"], w["mu"], w["wl"], w["wb"],
              w["al"], w["ab"], w["gl"], w["kk"], w["ka"], w["rk"], w["pa"], w["ones2"]]
    tok = lambda: pl.BlockSpec((None, tm, D_MODEL), lambda b, i: (b, i, 0))
    pair = lambda: pl.BlockSpec((None, N_PAIRS, tm, 128), lambda b, i: (b, 0, i, 0))
    tok_shape = jax.ShapeDtypeStruct((bsz, t_len, D_MODEL), F32)
    pair_shape = jax.ShapeDtypeStruct((bsz, N_PAIRS, t_len, 128), F32)
    return pl.pallas_call(
        functools.partial(_front_kernel, False, tm),
        grid=(bsz, nt),
        in_specs=[tok()] + [_const_spec(c.shape, 2) for c in consts],
        out_specs=[tok(), tok(), tok(), tok()] + [pair() for _ in range(6)]
        + [pl.BlockSpec((None, 1, SHIFT_WIDTH), lambda b, i: (b, 0, 0))],
        out_shape=[tok_shape] * 4 + [pair_shape] * 6 + [jax.ShapeDtypeStruct((bsz, 1, SHIFT_WIDTH), F32)],
        scratch_shapes=[pltpu.VMEM((tm, D_MODEL), BF16), pltpu.VMEM((8, SHIFT_WIDTH), F32)],
        compiler_params=pltpu.CompilerParams(dimension_semantics=("parallel", "arbitrary"),
                                             vmem_limit_bytes=VMEM_LIMIT),
        name="front_prompt",
    )(x, *consts)


def _front_sample(x, prev, w):
    n = x.shape[0]
    consts = [w["nw"], w["win"], w["lnw"], w["lnb"], w["ws0"], w["bs0"], w["mu"], w["wl"], w["wb"],
              w["al"], w["ab"], w["gl"], w["kk"], w["ka"], w["rk"], w["pa"], w["ones2"]]
    tok = lambda: pl.BlockSpec((n, D_MODEL), lambda i: (0, 0))
    wide = lambda: pl.BlockSpec((n, SHIFT_WIDTH), lambda i: (0, 0))
    tok_shape = jax.ShapeDtypeStruct((n, D_MODEL), F32)
    return pl.pallas_call(
        functools.partial(_front_kernel, True, n),
        grid=(1,),
        in_specs=[tok(), wide()] + [_const_spec(c.shape, 1) for c in consts],
        out_specs=[tok() for _ in range(10)] + [wide(), tok()],
        out_shape=[tok_shape] * 10 + [jax.ShapeDtypeStruct((n, SHIFT_WIDTH), F32), tok_shape],
        scratch_shapes=[pltpu.VMEM((n, D_MODEL), BF16)],
        compiler_params=pltpu.CompilerParams(dimension_semantics=("arbitrary",),
                                             vmem_limit_bytes=VMEM_LIMIT),
        name="front_sample",
    )(x, prev, *consts)


def _scan_kernel(tt, r_ref, lw_ref, k_ref, v_ref, kn_ref, bs_ref, y_ref, sout_ref, st_ref):
    c_len = SCAN_CHUNK
    i = pl.program_id(2)

    @pl.when(i == 0)
    def _():
        st_ref[...] = jnp.zeros_like(st_ref)

    lane = lax.broadcasted_iota(jnp.int32, (c_len, 128), 1)
    head0 = lane < B_HEAD_DIM
    row2 = lax.broadcasted_iota(jnp.int32, (128, 128), 0)
    col2 = lax.broadcasted_iota(jnp.int32, (128, 128), 1)
    tpos = jnp.bitwise_and(row2, c_len - 1)
    spos = jnp.bitwise_and(col2, c_len - 1)
    strict = tpos > spos
    incl = tpos >= spos
    eye = row2 == col2
    eye_f = jnp.where(eye, 1.0, 0.0).astype(F32)
    rc = lax.broadcasted_iota(jnp.int32, (c_len, c_len), 0)
    cc = lax.broadcasted_iota(jnp.int32, (c_len, c_len), 1)
    tri = jnp.where(rc >= cc, 1.0, 0.0).astype(F32)

    def pair_diag(z):
        return jnp.concatenate([jnp.where(head0, z, 0.0), jnp.where(head0, 0.0, z)], axis=0)

    for c in range(tt // c_len):
        rs = slice(c * c_len, (c + 1) * c_len)
        lw = lw_ref[rs, :]
        kn = kn_ref[rs, :]
        bs = bs_ref[rs, :]
        kx = k_ref[rs, :]
        log_p = _dot_hi(tri, lw)
        log_pc = log_p[c_len - 1:c_len, :]
        e_inv = jnp.exp(-log_p)
        e_dec = jnp.exp(log_pc - log_p)
        a_m = pair_diag(-kn * jnp.exp(log_p - lw))
        r_m = pair_diag(r_ref[rs, :] * jnp.exp(log_p))
        b_m = pair_diag(bs * e_inv)
        k_m = pair_diag(kx * e_inv)
        bt_m = pair_diag(bs * e_dec)
        kt_m = pair_diag(kx * e_dec)
        v_m = pair_diag(v_ref[rs, :])
        pc_diag = jnp.where(eye, jnp.broadcast_to(jnp.exp(log_pc), (128, 128)), 0.0)

        gram = lax.dot_general(jnp.concatenate([a_m, r_m], axis=0), jnp.concatenate([b_m, k_m], axis=0),
                               (((1,), (1,)), ((), ())), precision=HIGHEST, preferred_element_type=F32)
        l_ab = jnp.where(strict, gram[0:128, 0:128], 0.0)
        l_ak = jnp.where(strict, gram[0:128, 128:256], 0.0)
        l_rb = jnp.where(incl, gram[128:256, 0:128], 0.0)
        l_rk = jnp.where(incl, gram[128:256, 128:256], 0.0)

        t_inv = eye_f + l_ab
        pw = l_ab
        for _ in range(int(math.log2(c_len)) - 1):
            pw = _dot_hi(pw, pw)
            t_inv = t_inv + _dot_hi(t_inv, pw)
        wu = _dot_hi(t_inv, jnp.concatenate([a_m, _dot_hi(l_ak, v_m)], axis=1))

        st = st_ref[...]
        xs = _dot_hi(jnp.concatenate([wu[:, 0:128], r_m], axis=0), st)
        u_m = xs[0:128, :] + wu[:, 128:256]
        uv = jnp.concatenate([u_m, v_m], axis=0)
        y_m = xs[128:256, :] + _dot_hi(jnp.concatenate([l_rb, l_rk], axis=1), uv)
        y_ref[rs, :] = y_m[0:c_len, :] + y_m[c_len:128, :]
        lhs_t = jnp.concatenate([bt_m, kt_m, pc_diag], axis=0).T
        st_ref[...] = _dot_hi(lhs_t, jnp.concatenate([uv, st], axis=0))

    @pl.when(i == pl.num_programs(2) - 1)
    def _():
        sout_ref[...] = st_ref[...]


def _scan_prompt(r, lw, k, v, kn, bs, tt):
    bsz, n_pairs, t_len, _ = r.shape
    blk = lambda: pl.BlockSpec((None, None, tt, 128), lambda b, p, i: (b, p, i, 0))
    return pl.pallas_call(
        functools.partial(_scan_kernel, tt),
        grid=(bsz, n_pairs, t_len // tt),
        in_specs=[blk() for _ in range(6)],
        out_specs=[blk(), pl.BlockSpec((None, None, 128, 128), lambda b, p, i: (b, p, 0, 0))],
        out_shape=[jax.ShapeDtypeStruct((bsz, n_pairs, t_len, 128), F32),
                   jax.ShapeDtypeStruct((bsz, n_pairs, 128, 128), F32)],
        scratch_shapes=[pltpu.VMEM((128, 128), F32)],
        compiler_params=pltpu.CompilerParams(dimension_semantics=("parallel", "parallel", "arbitrary"),
                                             vmem_limit_bytes=VMEM_LIMIT),
        name="scan_prompt",
    )(r, lw, k, v, kn, bs)


def _step_kernel(r_ref, lw_ref, k_ref, v_ref, kn_ref, bs_ref, s_ref, y_ref, sout_ref):
    n = B_HEAD_DIM
    eye = lax.broadcasted_iota(jnp.int32, (n, n), 0) == lax.broadcasted_iota(jnp.int32, (n, n), 1)

    def to_col(row):
        return jnp.sum(jnp.where(eye, row, 0.0), axis=1, keepdims=True)

    def to_row(col):
        return jnp.sum(jnp.where(eye, col, 0.0), axis=0, keepdims=True)

    for h in range(B_HEADS):
        hs = slice(h * n, (h + 1) * n)
        s0 = s_ref[h]
        kn = kn_ref[:, hs]
        sa = jnp.sum(s0 * (-kn), axis=1, keepdims=True)
        s1 = s0 * jnp.exp(lw_ref[:, hs]) + sa * bs_ref[:, hs] + to_col(v_ref[:, hs]) * k_ref[:, hs]
        sout_ref[h] = s1
        y_ref[:, hs] = to_row(jnp.sum(s1 * r_ref[:, hs], axis=1, keepdims=True))


def _scan_sample(r, lw, k, v, kn, bs, s0):
    n = r.shape[0]
    vecs = [z.reshape(n, 1, D_MODEL) for z in (r, lw, k, v, kn, bs)]
    vec = lambda: pl.BlockSpec((None, 1, D_MODEL), lambda i: (i, 0, 0))
    st = lambda: pl.BlockSpec((None, B_HEADS, B_HEAD_DIM, B_HEAD_DIM), lambda i: (i, 0, 0, 0))
    y, s1 = pl.pallas_call(
        _step_kernel,
        grid=(n,),
        in_specs=[vec() for _ in range(6)] + [st()],
        out_specs=[vec(), st()],
        out_shape=[jax.ShapeDtypeStruct((n, 1, D_MODEL), F32), jax.ShapeDtypeStruct(s0.shape, F32)],
        compiler_params=pltpu.CompilerParams(dimension_semantics=("parallel",)),
        name="scan_sample",
    )(*vecs, s0)
    return y.reshape(n, D_MODEL), s1


def _back_kernel(paired, y_ref, bonus_ref, g_ref, gateb_ref, outa_ref, x_ref, lxw_ref, lxb_ref, pb_ref,
                 wo_ref, nfw_ref, rw_ref, rb_ref, ones2_ref, x1_ref, tok_ref, comb_ref):
    if paired:
        y = jnp.concatenate([y_ref[p] for p in range(N_PAIRS)], axis=1)
    else:
        y = y_ref[...]
    ones2 = ones2_ref[...]
    inv_n = 1.0 / B_HEAD_DIM
    yc = y - _head_sums(y, ones2) * inv_n
    var = _head_sums(yc * yc, ones2) * inv_n
    yn = yc * lax.rsqrt(var + GN_EPS) * lxw_ref[...] + lxb_ref[...]
    yb = ((yn + bonus_ref[...]) * g_ref[...]).astype(BF16)
    merged = outa_ref[...] + gateb_ref[...] * _dot(yb, pb_ref[...])
    x1 = x_ref[...] + _dot(merged.astype(BF16), wo_ref[...])
    x1_ref[...] = x1
    tok = _rms_norm(x1, nfw_ref[...])
    tok_ref[...] = tok.astype(BF16)

    logits = _dot_hi(tok, rw_ref[...]) + rb_ref[...]
    lane = lax.broadcasted_iota(jnp.int32, logits.shape, 1).astype(F32)
    neg = jnp.float32(-jnp.inf)
    big = jnp.float32(1 << 20)
    is_grp = lane < N_GROUPS
    lg = jnp.where(is_grp, logits, neg)
    gmax = jnp.max(lg, axis=-1, keepdims=True)
    grp = jnp.min(jnp.where(lg == gmax, lane, big), axis=-1, keepdims=True)
    p_grp = 1.0 / jnp.sum(jnp.where(is_grp, jnp.exp(lg - gmax), 0.0), axis=-1, keepdims=True)
    lo = EXPERT_LANE0 + grp * EXPERTS_PER_GROUP
    in_grp = (lane >= lo) & (lane < lo + EXPERTS_PER_GROUP)
    le = jnp.where(in_grp, logits, neg)
    top1 = jnp.max(le, axis=-1, keepdims=True)
    i1 = jnp.min(jnp.where(le == top1, lane, big), axis=-1, keepdims=True)
    le2 = jnp.where(lane == i1, neg, le)
    top2 = jnp.max(le2, axis=-1, keepdims=True)
    i2 = jnp.min(jnp.where(le2 == top2, lane, big), axis=-1, keepdims=True)
    e2 = jnp.exp(top2 - top1)
    w1 = p_grp / (1.0 + e2)
    w2 = p_grp * e2 / (1.0 + e2)
    comb_ref[...] = jnp.where(lane == i1, w1, 0.0) + jnp.where(lane == i2, w2, 0.0)


def _back(paired, y, bonus, g, gateb, outa, x, w, tm):
    n = x.shape[0]
    consts = [w["lxw"], w["lxb"], w["pb"], w["wo"], w["nfw"], w["rw"], w["rb"], w["ones2"]]
    tok = lambda: pl.BlockSpec((tm, D_MODEL), lambda i: (i, 0))
    if paired:
        nt = y.shape[2] // tm
        y_spec = pl.BlockSpec((None, N_PAIRS, tm, 128), lambda i: (i // nt, 0, i % nt, 0))
    else:
        y_spec = tok()
    return pl.pallas_call(
        functools.partial(_back_kernel, paired),
        grid=(n // tm,),
        in_specs=[y_spec] + [tok() for _ in range(5)] + [_const_spec(c.shape, 1) for c in consts],
        out_specs=[tok(), tok(), pl.BlockSpec((tm, ROUTER_LANES), lambda i: (i, 0))],
        out_shape=[jax.ShapeDtypeStruct((n, D_MODEL), F32), jax.ShapeDtypeStruct((n, D_MODEL), BF16),
                   jax.ShapeDtypeStruct((n, ROUTER_LANES), F32)],
        compiler_params=pltpu.CompilerParams(dimension_semantics=("parallel",), vmem_limit_bytes=VMEM_LIMIT),
        name="back_prompt" if paired else "back_sample",
    )(y, bonus, g, gateb, outa, x, *consts)


def _moe_kernel(tok_ref, comb_ref, x1_ref, wg_ref, wu_ref, wd_ref, fw_ref, o_ref, acc_ref):
    e = pl.program_id(1)

    @pl.when(e == 0)
    def _():
        acc_ref[...] = jnp.zeros_like(acc_ref)

    t = tok_ref[...]
    comb = comb_ref[...]
    lane = lax.broadcasted_iota(jnp.int32, comb.shape, 1)
    c = jnp.sum(jnp.where(lane == EXPERT_LANE0 + e, comb, 0.0), axis=-1, keepdims=True)
    gate = _dot(t, wg_ref[...])
    h = gate * _sigmoid(gate) * _dot(t, wu_ref[...])
    acc_ref[...] += c * _dot(h.astype(BF16), wd_ref[...])

    @pl.when(e == N_EXPERTS - 1)
    def _():
        o_ref[...] = _rms_norm(x1_ref[...] + acc_ref[...], fw_ref[...])


def _moe(tok, comb, x1, w, tm):
    n = tok.shape[0]
    return pl.pallas_call(
        _moe_kernel,
        grid=(n // tm, N_EXPERTS),
        in_specs=[pl.BlockSpec((tm, D_MODEL), lambda i, e: (i, 0)),
                  pl.BlockSpec((tm, ROUTER_LANES), lambda i, e: (i, 0)),
                  pl.BlockSpec((tm, D_MODEL), lambda i, e: (i, 0)),
                  pl.BlockSpec((None, D_MODEL, D_EXPERT), lambda i, e: (e, 0, 0)),
                  pl.BlockSpec((None, D_MODEL, D_EXPERT), lambda i, e: (e, 0, 0)),
                  pl.BlockSpec((None, D_EXPERT, D_MODEL), lambda i, e: (e, 0, 0)),
                  pl.BlockSpec((1, D_MODEL), lambda i, e: (0, 0))],
        out_specs=pl.BlockSpec((tm, D_MODEL), lambda i, e: (i, 0)),
        out_shape=jax.ShapeDtypeStruct((n, D_MODEL), F32),
        scratch_shapes=[pltpu.VMEM((tm, D_MODEL), F32)],
        compiler_params=pltpu.CompilerParams(dimension_semantics=("parallel", "arbitrary"),
                                             vmem_limit_bytes=VMEM_LIMIT),
        name="moe",
    )(tok, comb, x1, w["wg"], w["wu"], w["wd"], w["fw"])


def _prep_weights(norm_mix_w, w_in, sgu_ln_w, sgu_ln_b, sgu_w_s, sgu_b, rwkv_mu, w_lora_up, w_bias,
                  a_lora_up, a_bias, g_lora_up, k_k, k_a, r_k, lnx_w, lnx_b, proj_a, proj_b, w_out,
                  norm_ffn_w, router_group, router_group_bias, router_expert, router_expert_bias,
                  moe_w_gate, moe_w_up, moe_w_down, norm_final_w):
    row = lambda z: z.reshape(1, -1).astype(F32)
    causal = jnp.tril(jnp.ones((CHUNK, CHUNK), dtype=bool))
    zeros_lora = jnp.zeros((64, D_MODEL), F32)
    head = jnp.arange(128) // B_HEAD_DIM
    ones_bd = (head[:, None] == head[None, :]).astype(BF16)
    pad = ROUTER_LANES - N_GROUPS - N_EXPERTS
    return {
        "nw": row(norm_mix_w), "win": w_in.astype(BF16), "lnw": row(sgu_ln_w), "lnb": row(sgu_ln_b),
        "wsc": jnp.where(causal[None], sgu_w_s, 0.0).astype(BF16),
        "bsb": jnp.repeat(jnp.transpose(sgu_b), A_HEAD_DIM, axis=1),
        "ws0": jnp.repeat(sgu_w_s[:, 0, 0], A_HEAD_DIM).reshape(1, -1),
        "bs0": jnp.repeat(sgu_b[:, 0], A_HEAD_DIM).reshape(1, -1),
        "mu": row(rwkv_mu),
        "wl": jnp.concatenate([w_lora_up, zeros_lora], axis=0).astype(BF16), "wb": row(w_bias),
        "al": jnp.concatenate([zeros_lora, a_lora_up], axis=0).astype(BF16), "ab": row(a_bias),
        "gl": g_lora_up.astype(BF16), "kk": row(k_k), "ka": row(k_a), "rk": row(r_k),
        "pa": proj_a.astype(BF16), "ones2": jnp.concatenate([ones_bd, ones_bd], axis=0),
        "lxw": row(lnx_w), "lxb": row(lnx_b), "pb": proj_b.astype(BF16), "wo": w_out.astype(BF16),
        "nfw": row(norm_ffn_w),
        "rw": jnp.concatenate([router_group, router_expert, jnp.zeros((D_MODEL, pad), F32)], axis=1),
        "rb": jnp.concatenate([router_group_bias, router_expert_bias, jnp.zeros((pad,), F32)]).reshape(1, -1),
        "wg": moe_w_gate.astype(BF16), "wu": moe_w_up.astype(BF16), "wd": moe_w_down.astype(BF16),
        "fw": row(norm_final_w),
    }


def kernel(x_prompt, x_sample, state_wkv, state_shift, norm_mix_w, w_in, sgu_ln_w, sgu_ln_b, sgu_w_s, sgu_b, rwkv_mu, w_lora_up, w_bias, a_lora_up, a_bias, g_lora_up, k_k, k_a, r_k, lnx_w, lnx_b, proj_a, proj_b, w_out, norm_ffn_w, router_group, router_group_bias, router_expert, router_expert_bias, moe_w_gate, moe_w_up, moe_w_down, norm_final_w):
    layer = [z[0] for z in (norm_mix_w, w_in, sgu_ln_w, sgu_ln_b, sgu_w_s, sgu_b, rwkv_mu, w_lora_up, w_bias,
                            a_lora_up, a_bias, g_lora_up, k_k, k_a, r_k, lnx_w, lnx_b, proj_a, proj_b, w_out,
                            norm_ffn_w, router_group, router_group_bias, router_expert, router_expert_bias,
                            moe_w_gate, moe_w_up, moe_w_down)]
    w = _prep_weights(*layer, norm_final_w)
    bsz, t_len, _ = x_prompt.shape
    n_s = x_sample.shape[0]

    (outa, gateb, g, bonus, r, lw, k, v, kn, bs, last) = _front_prompt(x_prompt, w, 256)
    y, st = _scan_prompt(r, lw, k, v, kn, bs, 256)
    flat = lambda z: z.reshape(bsz * t_len, D_MODEL)
    x1, tok, comb = _back(True, y, flat(bonus), flat(g), flat(gateb), flat(outa), flat(x_prompt), w, 512)
    y_prompt = _moe(tok, comb, x1, w, 1024).reshape(bsz, t_len, D_MODEL)
    st = st.reshape(bsz, N_PAIRS, 2, B_HEAD_DIM, 2, B_HEAD_DIM)
    wkv_p = jnp.stack([st[:, :, 0, :, 0, :], st[:, :, 1, :, 1, :]], axis=2)
    wkv_p = jnp.swapaxes(wkv_p.reshape(bsz, B_HEADS, B_HEAD_DIM, B_HEAD_DIM), -1, -2)

    xs2 = x_sample.reshape(n_s, D_MODEL)
    (outa, gateb, g, bonus, r, lw, k, v, kn, bs, cols_s, vn_s) = _front_sample(xs2, state_shift[0], w)
    y, wkv_s = _scan_sample(r, lw, k, v, kn, bs, state_wkv[0])
    x1, tok, comb = _back(False, y, bonus, g, gateb, outa, xs2, w, n_s)
    y_sample = _moe(tok, comb, x1, w, n_s).reshape(n_s, 1, D_MODEL)

    return (y_prompt, y_sample, wkv_p[None], last.reshape(1, bsz, SHIFT_WIDTH), wkv_s[None], cols_s[None],
            vn_s.reshape(1, n_s, 1, A_HEADS, A_HEAD_DIM))
```

```python
import functools
import math

import jax
import jax.numpy as jnp
from jax import lax
from jax.experimental import pallas as pl
from jax.experimental.pallas import tpu as pltpu

F32 = jnp.float32
BF16 = jnp.bfloat16
HIGHEST = lax.Precision.HIGHEST

D_MODEL = 1024
CHUNK = 128
A_HEADS = 8
A_HEAD_DIM = 128
B_HEADS = 16
B_HEAD_DIM = 64
N_PAIRS = B_HEADS // 2
SHIFT_WIDTH = 3328
OFF_V = 1024
OFF_SHIFT = 2048
OFF_GATE_A = OFF_SHIFT + SHIFT_WIDTH
OFF_GATE_B = OFF_GATE_A + D_MODEL
IN_COLS = OFF_GATE_B + D_MODEL
N_GROUPS = 4
EXPERTS_PER_GROUP = 8
N_EXPERTS = 32
D_EXPERT = 256
ROUTER_LANES = 128
EXPERT_LANE0 = N_GROUPS
RMS_EPS = 1e-6
LN_EPS = 1e-5
GN_EPS = 64e-5

SCAN_CHUNK = 64
VMEM_LIMIT = 56 * 1024 * 1024


def _gelu(x):
    return x * (0.5 * (1.0 + jnp.tanh(math.sqrt(2.0 / math.pi) * (x + 0.044715 * (x * x * x)))))


def _sigmoid(x):
    return 1.0 / (1.0 + jnp.exp(-x))


def _softplus(z):
    return jnp.maximum(z, 0.0) + jnp.log(1.0 + jnp.exp(-jnp.abs(z)))


def _rms_norm(x, g):
    return x * lax.rsqrt(jnp.mean(x * x, axis=-1, keepdims=True) + RMS_EPS) * g


def _dot(a, b):
    return jnp.dot(a, b, preferred_element_type=F32)


def _dot_hi(a, b):
    return jnp.dot(a, b, preferred_element_type=F32, precision=HIGHEST)


def _head_sums(z, ones2):
    outs = []
    for p in range(N_PAIRS):
        zp = z[:, p * 128:(p + 1) * 128]
        hi = zp.astype(BF16)
        lo = (zp - hi.astype(F32)).astype(BF16)
        outs.append(_dot(jnp.concatenate([hi, lo], axis=1), ones2))
    return jnp.concatenate(outs, axis=1)


def _front_kernel(is_sample, tm, *refs):
    if is_sample:
        (x_ref, prev_ref, nw_ref, win_ref, lnw_ref, lnb_ref, ws0_ref, bs0_ref, mu_ref, wl_ref, wb_ref,
         al_ref, ab_ref, gl_ref, kk_ref, ka_ref, rk_ref, pa_ref, ones2_ref,
         outa_ref, gateb_ref, g_ref, bonus_ref, r_ref, lw_ref, k_ref, v_ref, kn_ref, bs_ref,
         cols_ref, vn_ref, ya_ref) = refs
    else:
        (x_ref, nw_ref, win_ref, lnw_ref, lnb_ref, wsc_ref, bsb_ref, mu_ref, wl_ref, wb_ref,
         al_ref, ab_ref, gl_ref, kk_ref, ka_ref, rk_ref, pa_ref, ones2_ref,
         outa_ref, gateb_ref, g_ref, bonus_ref, r_ref, lw_ref, k_ref, v_ref, kn_ref, bs_ref,
         last_ref, ya_ref, carry_ref) = refs

    xb = _rms_norm(x_ref[...], nw_ref[...]).astype(BF16)

    def proj(lo, hi):
        return _dot(xb, win_ref[:, lo:hi])

    for h in range(A_HEADS):
        cs = slice(h * A_HEAD_DIM, (h + 1) * A_HEAD_DIM)
        u = _gelu(proj(h * A_HEAD_DIM, (h + 1) * A_HEAD_DIM))
        v = _gelu(proj(OFF_V + h * A_HEAD_DIM, OFF_V + (h + 1) * A_HEAD_DIM))
        mean = jnp.mean(v, axis=-1, keepdims=True)
        vc = v - mean
        var = jnp.mean(vc * vc, axis=-1, keepdims=True)
        vn = vc * lax.rsqrt(var + LN_EPS) * lnw_ref[:, cs] + lnb_ref[:, cs]
        if is_sample:
            vn_ref[:, cs] = vn
            ya_ref[:, cs] = (u * (vn * ws0_ref[:, cs] + bs0_ref[:, cs])).astype(BF16)
        else:
            vnb = vn.astype(BF16)
            for c in range(tm // CHUNK):
                rs = slice(c * CHUNK, (c + 1) * CHUNK)
                s = _dot(wsc_ref[h], vnb[rs, :]) + bsb_ref[:, cs]
                ya_ref[rs, cs] = (u[rs, :] * s).astype(BF16)
    outa_ref[...] = _sigmoid(proj(OFF_GATE_A, OFF_GATE_B)) * _dot(ya_ref[...], pa_ref[...])
    gateb_ref[...] = _sigmoid(proj(OFF_GATE_B, IN_COLS))

    cols = proj(OFF_SHIFT, OFF_GATE_A)
    if is_sample:
        prev = prev_ref[...]
        cols_ref[...] = cols
    else:
        @pl.when(pl.program_id(1) == 0)
        def _():
            carry_ref[...] = jnp.zeros_like(carry_ref)

        rolled = pltpu.roll(cols, 1, 0)
        row = lax.broadcasted_iota(jnp.int32, cols.shape, 0)
        prev = jnp.where(row == 0, carry_ref[0:1, :], rolled)
        carry_ref[0:1, :] = cols[tm - 1:tm, :]
        last_ref[...] = cols[tm - 1:tm, :]
    xs = cols + (prev - cols) * mu_ref[...]
    r = xs[:, 0:1024]
    k = xs[:, 1024:2048]
    vb = xs[:, 2048:3072]
    wa = xs[:, 3072:3200]
    gd = xs[:, 3200:3328]
    w_log = -_softplus(-(wb_ref[...] + _dot(jnp.tanh(wa).astype(BF16), wl_ref[...]))) - 0.5
    logw = -jnp.exp(w_log)
    a = _sigmoid(ab_ref[...] + _dot(wa.astype(BF16), al_ref[...]))
    g_ref[...] = _dot(_sigmoid(gd).astype(BF16), gl_ref[...])
    ones2 = ones2_ref[...]
    kk = k * kk_ref[...]
    kkn = kk / jnp.maximum(jnp.sqrt(_head_sums(kk * kk, ones2)), 1e-12)
    k2 = k * (1.0 + (a - 1.0) * ka_ref[...])
    bonus_ref[...] = _head_sums(r * k2 * rk_ref[...], ones2) * vb
    bsc = kkn * a
    if is_sample:
        r_ref[...] = r
        lw_ref[...] = logw
        k_ref[...] = k2
        v_ref[...] = vb
        kn_ref[...] = kkn
        bs_ref[...] = bsc
    else:
        for p in range(N_PAIRS):
            ps = slice(p * 128, (p + 1) * 128)
            r_ref[p] = r[:, ps]
            lw_ref[p] = logw[:, ps]
            k_ref[p] = k2[:, ps]
            v_ref[p] = vb[:, ps]
            kn_ref[p] = kkn[:, ps]
            bs_ref[p] = bsc[:, ps]


def _const_spec(shape, grid_rank):
    zeros = (0,) * len(shape)
    if grid_rank == 1:
        return pl.BlockSpec(shape, lambda i: zeros, pipeline_mode=pl.Buffered(1))
    return pl.BlockSpec(shape, lambda b, i: zeros, pipeline_mode=pl.Buffered(1))


def _front_prompt(x, w, tm):
    bsz, t_len, _ = x.shape
    nt = t_len // tm
    consts = [w["nw"], w["win"], w["lnw"], w["lnb"], w["wsc"], w["bsb"], w["mu"], w["wl"], w["wb"],
              w["al"], w["ab"], w["gl"], w["kk"], w["ka"], w["rk"], w["pa"], w["ones2"]]
    tok = lambda: pl.BlockSpec((None, tm, D_MODEL), lambda b, i: (b, i, 0))
    pair = lambda: pl.BlockSpec((None, N_PAIRS, tm, 128), lambda b, i: (b, 0, i, 0))
    tok_shape = jax.ShapeDtypeStruct((bsz, t_len, D_MODEL), F32)
    pair_shape = jax.ShapeDtypeStruct((bsz, N_PAIRS, t_len, 128), F32)
    return pl.pallas_call(
        functools.partial(_front_kernel, False, tm),
        grid=(bsz, nt),
        in_specs=[tok()] + [_const_spec(c.shape, 2) for c in consts],
        out_specs=[tok(), tok(), tok(), tok()] + [pair() for _ in range(6)]
        + [pl.BlockSpec((None, 1, SHIFT_WIDTH), lambda b, i: (b, 0, 0))],
        out_shape=[tok_shape] * 4 + [pair_shape] * 6 + [jax.ShapeDtypeStruct((bsz, 1, SHIFT_WIDTH), F32)],
        scratch_shapes=[pltpu.VMEM((tm, D_MODEL), BF16), pltpu.VMEM((8, SHIFT_WIDTH), F32)],
        compiler_params=pltpu.CompilerParams(dimension_semantics=("parallel", "arbitrary"),
                                             vmem_limit_bytes=VMEM_LIMIT),
        name="front_prompt",
    )(x, *consts)


def _front_sample(x, prev, w):
    n = x.shape[0]
    consts = [w["nw"], w["win"], w["lnw"], w["lnb"], w["ws0"], w["bs0"], w["mu"], w["wl"], w["wb"],
              w["al"], w["ab"], w["gl"], w["kk"], w["ka"], w["rk"], w["pa"], w["ones2"]]
    tok = lambda: pl.BlockSpec((n, D_MODEL), lambda i: (0, 0))
    wide = lambda: pl.BlockSpec((n, SHIFT_WIDTH), lambda i: (0, 0))
    tok_shape = jax.ShapeDtypeStruct((n, D_MODEL), F32)
    return pl.pallas_call(
        functools.partial(_front_kernel, True, n),
        grid=(1,),
        in_specs=[tok(), wide()] + [_const_spec(c.shape, 1) for c in consts],
        out_specs=[tok() for _ in range(10)] + [wide(), tok()],
        out_shape=[tok_shape] * 10 + [jax.ShapeDtypeStruct((n, SHIFT_WIDTH), F32), tok_shape],
        scratch_shapes=[pltpu.VMEM((n, D_MODEL), BF16)],
        compiler_params=pltpu.CompilerParams(dimension_semantics=("arbitrary",),
                                             vmem_limit_bytes=VMEM_LIMIT),
        name="front_sample",
    )(x, prev, *consts)


def _split3(x):
    hi = x.astype(BF16)
    r1 = x - hi.astype(F32)
    mid = r1.astype(BF16)
    lo = (r1 - mid.astype(F32)).astype(BF16)
    return hi, mid, lo


def _scan_kernel(tt, npp, r_ref, lw_ref, k_ref, v_ref, kn_ref, bs_ref, y_ref, sout_ref, st_ref):
    c_len = SCAN_CHUNK
    i = pl.program_id(2)

    @pl.when(i == 0)
    def _():
        st_ref[...] = jnp.zeros_like(st_ref)

    lane = lax.broadcasted_iota(jnp.int32, (c_len, 128), 1)
    head0 = lane < B_HEAD_DIM
    row2 = lax.broadcasted_iota(jnp.int32, (128, 128), 0)
    col2 = lax.broadcasted_iota(jnp.int32, (128, 128), 1)
    tpos = jnp.bitwise_and(row2, c_len - 1)
    spos = jnp.bitwise_and(col2, c_len - 1)
    strict = tpos > spos
    incl = tpos >= spos
    eye = row2 == col2
    rc = lax.broadcasted_iota(jnp.int32, (c_len, 3 * c_len), 0)
    cc = jnp.bitwise_and(lax.broadcasted_iota(jnp.int32, (c_len, 3 * c_len), 1), c_len - 1)
    tri3 = jnp.where(rc >= cc, 1.0, 0.0).astype(BF16)

    def pair_diag(z):
        return jnp.concatenate([jnp.where(head0, z, 0.0), jnp.where(head0, 0.0, z)], axis=0).astype(BF16)

    def bdot(a, b):
        return _dot(a.astype(BF16), b.astype(BF16))

    n_chunks = tt // c_len
    insts = [(c, q) for c in range(n_chunks) for q in range(npp)]
    rows = lambda c: slice(c * c_len, (c + 1) * c_len)

    log_p = {}
    for c, q in insts:
        log_p[c, q] = _dot(tri3, jnp.concatenate(_split3(lw_ref[q, rows(c), :]), axis=0))

    ops, gram = {}, {}
    for c, q in insts:
        lp = log_p[c, q]
        kn = kn_ref[q, rows(c), :]
        bs = bs_ref[q, rows(c), :]
        kx = k_ref[q, rows(c), :]
        log_pc = lp[c_len - 1:c_len, :]
        e_inv = jnp.exp(-lp)
        e_dec = jnp.exp(log_pc - lp)
        a_m = pair_diag(-kn * jnp.exp(lp - lw_ref[q, rows(c), :]))
        r_m = pair_diag(r_ref[q, rows(c), :] * jnp.exp(lp))
        v_m = pair_diag(v_ref[q, rows(c), :])
        btkt = jnp.concatenate([pair_diag(bs * e_dec), pair_diag(kx * e_dec)], axis=0)
        pc_col = jnp.sum(jnp.where(eye, jnp.exp(log_pc), 0.0), axis=1, keepdims=True)
        ops[c, q] = (a_m, r_m, v_m, btkt, pc_col)
        gram[c, q] = lax.dot_general(jnp.concatenate([a_m, r_m], axis=0),
                                     jnp.concatenate([pair_diag(bs * e_inv), pair_diag(kx * e_inv)], axis=0),
                                     (((1,), (1,)), ((), ())), preferred_element_type=F32)

    n_sum, pw, l_r, lakv = {}, {}, {}, {}
    for c, q in insts:
        g = gram[c, q]
        l_ab = jnp.where(strict, g[0:128, 0:128], 0.0)
        n_sum[c, q] = l_ab
        pw[c, q] = l_ab.astype(BF16)
        l_r[c, q] = jnp.concatenate([jnp.where(incl, g[128:256, 0:128], 0.0),
                                     jnp.where(incl, g[128:256, 128:256], 0.0)], axis=1).astype(BF16)
        lakv[c, q] = _dot(jnp.where(strict, g[0:128, 128:256], 0.0).astype(BF16), ops[c, q][2])

    for _ in range(int(math.log2(c_len)) - 1):
        for c, q in insts:
            pw[c, q] = _dot(pw[c, q], pw[c, q])
        for c, q in insts:
            n_sum[c, q] = n_sum[c, q] + pw[c, q] + bdot(n_sum[c, q], pw[c, q])
            pw[c, q] = pw[c, q].astype(BF16)

    wu = {}
    for c, q in insts:
        rhs = jnp.concatenate([ops[c, q][0].astype(F32), lakv[c, q]], axis=1)
        wu[c, q] = rhs + bdot(n_sum[c, q], rhs)

    for c in range(n_chunks):
        xs, st = {}, {}
        for q in range(npp):
            st[q] = st_ref[q]
            xs[q] = bdot(jnp.concatenate([wu[c, q][:, 0:128].astype(BF16), ops[c, q][1]], axis=0), st[q])
        for q in range(npp):
            _, _, v_m, btkt, pc_col = ops[c, q]
            uv = jnp.concatenate([(xs[q][0:128, :] + wu[c, q][:, 128:256]).astype(BF16), v_m], axis=0)
            y_m = xs[q][128:256, :] + _dot(l_r[c, q], uv)
            y_ref[q, rows(c), :] = y_m[0:c_len, :] + y_m[c_len:128, :]
            st_ref[q] = pc_col * st[q] + lax.dot_general(btkt, uv, (((0,), (0,)), ((), ())),
                                                         preferred_element_type=F32)

    @pl.when(i == pl.num_programs(2) - 1)
    def _():
        sout_ref[...] = st_ref[...]


def _scan_prompt(r, lw, k, v, kn, bs, tt, npp):
    bsz, n_pairs, t_len, _ = r.shape
    blk = lambda: pl.BlockSpec((None, npp, tt, 128), lambda b, p, i: (b, p, i, 0))
    return pl.pallas_call(
        functools.partial(_scan_kernel, tt, npp),
        grid=(bsz, n_pairs // npp, t_len // tt),
        in_specs=[blk() for _ in range(6)],
        out_specs=[blk(), pl.BlockSpec((None, npp, 128, 128), lambda b, p, i: (b, p, 0, 0))],
        out_shape=[jax.ShapeDtypeStruct((bsz, n_pairs, t_len, 128), F32),
                   jax.ShapeDtypeStruct((bsz, n_pairs, 128, 128), F32)],
        scratch_shapes=[pltpu.VMEM((npp, 128, 128), F32)],
        compiler_params=pltpu.CompilerParams(dimension_semantics=("parallel", "parallel", "arbitrary"),
                                             vmem_limit_bytes=VMEM_LIMIT),
        name="scan_prompt",
    )(r, lw, k, v, kn, bs)


def _step_kernel(r_ref, lw_ref, k_ref, v_ref, kn_ref, bs_ref, s_ref, y_ref, sout_ref):
    n = B_HEAD_DIM
    eye = lax.broadcasted_iota(jnp.int32, (n, n), 0) == lax.broadcasted_iota(jnp.int32, (n, n), 1)

    def to_col(row):
        return jnp.sum(jnp.where(eye, row, 0.0), axis=1, keepdims=True)

    def to_row(col):
        return jnp.sum(jnp.where(eye, col, 0.0), axis=0, keepdims=True)

    heads = range(B_HEADS)
    hs = lambda h: slice(h * n, (h + 1) * n)
    sa = [jnp.sum(s_ref[h] * (-kn_ref[:, hs(h)]), axis=1, keepdims=True) for h in heads]
    v_col = [to_col(v_ref[:, hs(h)]) for h in heads]
    y_col = []
    for h in heads:
        s1 = s_ref[h] * jnp.exp(lw_ref[:, hs(h)]) + sa[h] * bs_ref[:, hs(h)] + v_col[h] * k_ref[:, hs(h)]
        sout_ref[h] = s1
        y_col.append(jnp.sum(s1 * r_ref[:, hs(h)], axis=1, keepdims=True))
    for h in heads:
        y_ref[:, hs(h)] = to_row(y_col[h])


def _scan_sample(r, lw, k, v, kn, bs, s0):
    n = r.shape[0]
    vecs = [z.reshape(n, 1, D_MODEL) for z in (r, lw, k, v, kn, bs)]
    vec = lambda: pl.BlockSpec((None, 1, D_MODEL), lambda i: (i, 0, 0))
    st = lambda: pl.BlockSpec((None, B_HEADS, B_HEAD_DIM, B_HEAD_DIM), lambda i: (i, 0, 0, 0))
    y, s1 = pl.pallas_call(
        _step_kernel,
        grid=(n,),
        in_specs=[vec() for _ in range(6)] + [st()],
        out_specs=[vec(), st()],
        out_shape=[jax.ShapeDtypeStruct((n, 1, D_MODEL), F32), jax.ShapeDtypeStruct(s0.shape, F32)],
        compiler_params=pltpu.CompilerParams(dimension_semantics=("parallel",)),
        name="scan_sample",
    )(*vecs, s0)
    return y.reshape(n, D_MODEL), s1


def _back_kernel(paired, y_ref, bonus_ref, g_ref, gateb_ref, outa_ref, x_ref, lxw_ref, lxb_ref, pb_ref,
                 wo_ref, nfw_ref, rw_ref, rb_ref, ones2_ref, x1_ref, tok_ref, comb_ref):
    if paired:
        y = jnp.concatenate([y_ref[p] for p in range(N_PAIRS)], axis=1)
    else:
        y = y_ref[...]
    ones2 = ones2_ref[...]
    inv_n = 1.0 / B_HEAD_DIM
    yc = y - _head_sums(y, ones2) * inv_n
    var = _head_sums(yc * yc, ones2) * inv_n
    yn = yc * lax.rsqrt(var + GN_EPS) * lxw_ref[...] + lxb_ref[...]
    yb = ((yn + bonus_ref[...]) * g_ref[...]).astype(BF16)
    merged = outa_ref[...] + gateb_ref[...] * _dot(yb, pb_ref[...])
    x1 = x_ref[...] + _dot(merged.astype(BF16), wo_ref[...])
    x1_ref[...] = x1
    tok = _rms_norm(x1, nfw_ref[...])
    tok_ref[...] = tok.astype(BF16)

    logits = _dot_hi(tok, rw_ref[...]) + rb_ref[...]
    lane = lax.broadcasted_iota(jnp.int32, logits.shape, 1).astype(F32)
    neg = jnp.float32(-jnp.inf)
    big = jnp.float32(1 << 20)
    is_grp = lane < N_GROUPS
    lg = jnp.where(is_grp, logits, neg)
    gmax = jnp.max(lg, axis=-1, keepdims=True)
    grp = jnp.min(jnp.where(lg == gmax, lane, big), axis=-1, keepdims=True)
    p_grp = 1.0 / jnp.sum(jnp.where(is_grp, jnp.exp(lg - gmax), 0.0), axis=-1, keepdims=True)
    lo = EXPERT_LANE0 + grp * EXPERTS_PER_GROUP
    in_grp = (lane >= lo) & (lane < lo + EXPERTS_PER_GROUP)
    le = jnp.where(in_grp, logits, neg)
    top1 = jnp.max(le, axis=-1, keepdims=True)
    i1 = jnp.min(jnp.where(le == top1, lane, big), axis=-1, keepdims=True)
    le2 = jnp.where(lane == i1, neg, le)
    top2 = jnp.max(le2, axis=-1, keepdims=True)
    i2 = jnp.min(jnp.where(le2 == top2, lane, big), axis=-1, keepdims=True)
    e2 = jnp.exp(top2 - top1)
    w1 = p_grp / (1.0 + e2)
    w2 = p_grp * e2 / (1.0 + e2)
    comb_ref[...] = jnp.where(lane == i1, w1, 0.0) + jnp.where(lane == i2, w2, 0.0)


def _back(paired, y, bonus, g, gateb, outa, x, w, tm):
    n = x.shape[0]
    consts = [w["lxw"], w["lxb"], w["pb"], w["wo"], w["nfw"], w["rw"], w["rb"], w["ones2"]]
    tok = lambda: pl.BlockSpec((tm, D_MODEL), lambda i: (i, 0))
    if paired:
        nt = y.shape[2] // tm
        y_spec = pl.BlockSpec((None, N_PAIRS, tm, 128), lambda i: (i // nt, 0, i % nt, 0))
    else:
        y_spec = tok()
    return pl.pallas_call(
        functools.partial(_back_kernel, paired),
        grid=(n // tm,),
        in_specs=[y_spec] + [tok() for _ in range(5)] + [_const_spec(c.shape, 1) for c in consts],
        out_specs=[tok(), tok(), pl.BlockSpec((tm, ROUTER_LANES), lambda i: (i, 0))],
        out_shape=[jax.ShapeDtypeStruct((n, D_MODEL), F32), jax.ShapeDtypeStruct((n, D_MODEL), BF16),
                   jax.ShapeDtypeStruct((n, ROUTER_LANES), F32)],
        compiler_params=pltpu.CompilerParams(dimension_semantics=("parallel",), vmem_limit_bytes=VMEM_LIMIT),
        name="back_prompt" if paired else "back_sample",
    )(y, bonus, g, gateb, outa, x, *consts)


def _moe_kernel(tok_ref, comb_ref, x1_ref, wg_ref, wu_ref, wd_ref, fw_ref, o_ref, acc_ref):
    e = pl.program_id(1)

    @pl.when(e == 0)
    def _():
        acc_ref[...] = jnp.zeros_like(acc_ref)

    t = tok_ref[...]
    comb = comb_ref[...]
    lane = lax.broadcasted_iota(jnp.int32, comb.shape, 1)
    c = jnp.sum(jnp.where(lane == EXPERT_LANE0 + e, comb, 0.0), axis=-1, keepdims=True)
    gate = _dot(t, wg_ref[...])
    h = gate * _sigmoid(gate) * _dot(t, wu_ref[...])
    acc_ref[...] += c * _dot(h.astype(BF16), wd_ref[...])

    @pl.when(e == N_EXPERTS - 1)
    def _():
        o_ref[...] = _rms_norm(x1_ref[...] + acc_ref[...], fw_ref[...])


def _moe(tok, comb, x1, w, tm):
    n = tok.shape[0]
    return pl.pallas_call(
        _moe_kernel,
        grid=(n // tm, N_EXPERTS),
        in_specs=[pl.BlockSpec((tm, D_MODEL), lambda i, e: (i, 0)),
                  pl.BlockSpec((tm, ROUTER_LANES), lambda i, e: (i, 0)),
                  pl.BlockSpec((tm, D_MODEL), lambda i, e: (i, 0)),
                  pl.BlockSpec((None, D_MODEL, D_EXPERT), lambda i, e: (e, 0, 0)),
                  pl.BlockSpec((None, D_MODEL, D_EXPERT), lambda i, e: (e, 0, 0)),
                  pl.BlockSpec((None, D_EXPERT, D_MODEL), lambda i, e: (e, 0, 0)),
                  pl.BlockSpec((1, D_MODEL), lambda i, e: (0, 0))],
        out_specs=pl.BlockSpec((tm, D_MODEL), lambda i, e: (i, 0)),
        out_shape=jax.ShapeDtypeStruct((n, D_MODEL), F32),
        scratch_shapes=[pltpu.VMEM((tm, D_MODEL), F32)],
        compiler_params=pltpu.CompilerParams(dimension_semantics=("parallel", "arbitrary"),
                                             vmem_limit_bytes=VMEM_LIMIT),
        name="moe",
    )(tok, comb, x1, w["wg"], w["wu"], w["wd"], w["fw"])


def _prep_weights(norm_mix_w, w_in, sgu_ln_w, sgu_ln_b, sgu_w_s, sgu_b, rwkv_mu, w_lora_up, w_bias,
                  a_lora_up, a_bias, g_lora_up, k_k, k_a, r_k, lnx_w, lnx_b, proj_a, proj_b, w_out,
                  norm_ffn_w, router_group, router_group_bias, router_expert, router_expert_bias,
                  moe_w_gate, moe_w_up, moe_w_down, norm_final_w):
    row = lambda z: z.reshape(1, -1).astype(F32)
    causal = jnp.tril(jnp.ones((CHUNK, CHUNK), dtype=bool))
    zeros_lora = jnp.zeros((64, D_MODEL), F32)
    head = jnp.arange(128) // B_HEAD_DIM
    ones_bd = (head[:, None] == head[None, :]).astype(BF16)
    pad = ROUTER_LANES - N_GROUPS - N_EXPERTS
    return {
        "nw": row(norm_mix_w), "win": w_in.astype(BF16), "lnw": row(sgu_ln_w), "lnb": row(sgu_ln_b),
        "wsc": jnp.where(causal[None], sgu_w_s, 0.0).astype(BF16),
        "bsb": jnp.repeat(jnp.transpose(sgu_b), A_HEAD_DIM, axis=1),
        "ws0": jnp.repeat(sgu_w_s[:, 0, 0], A_HEAD_DIM).reshape(1, -1),
        "bs0": jnp.repeat(sgu_b[:, 0], A_HEAD_DIM).reshape(1, -1),
        "mu": row(rwkv_mu),
        "wl": jnp.concatenate([w_lora_up, zeros_lora], axis=0).astype(BF16), "wb": row(w_bias),
        "al": jnp.concatenate([zeros_lora, a_lora_up], axis=0).astype(BF16), "ab": row(a_bias),
        "gl": g_lora_up.astype(BF16), "kk": row(k_k), "ka": row(k_a), "rk": row(r_k),
        "pa": proj_a.astype(BF16), "ones2": jnp.concatenate([ones_bd, ones_bd], axis=0),
        "lxw": row(lnx_w), "lxb": row(lnx_b), "pb": proj_b.astype(BF16), "wo": w_out.astype(BF16),
        "nfw": row(norm_ffn_w),
        "rw": jnp.concatenate([router_group, router_expert, jnp.zeros((D_MODEL, pad), F32)], axis=1),
        "rb": jnp.concatenate([router_group_bias, router_expert_bias, jnp.zeros((pad,), F32)]).reshape(1, -1),
        "wg": moe_w_gate.astype(BF16), "wu": moe_w_up.astype(BF16), "wd": moe_w_down.astype(BF16),
        "fw": row(norm_final_w),
    }


def kernel(x_prompt, x_sample, state_wkv, state_shift, norm_mix_w, w_in, sgu_ln_w, sgu_ln_b, sgu_w_s, sgu_b, rwkv_mu, w_lora_up, w_bias, a_lora_up, a_bias, g_lora_up, k_k, k_a, r_k, lnx_w, lnx_b, proj_a, proj_b, w_out, norm_ffn_w, router_group, router_group_bias, router_expert, router_expert_bias, moe_w_gate, moe_w_up, moe_w_down, norm_final_w):
    layer = [z[0] for z in (norm_mix_w, w_in, sgu_ln_w, sgu_ln_b, sgu_w_s, sgu_b, rwkv_mu, w_lora_up, w_bias,
                            a_lora_up, a_bias, g_lora_up, k_k, k_a, r_k, lnx_w, lnx_b, proj_a, proj_b, w_out,
                            norm_ffn_w, router_group, router_group_bias, router_expert, router_expert_bias,
                            moe_w_gate, moe_w_up, moe_w_down)]
    w = _prep_weights(*layer, norm_final_w)
    bsz, t_len, _ = x_prompt.shape
    n_s = x_sample.shape[0]

    (outa, gateb, g, bonus, r, lw, k, v, kn, bs, last) = _front_prompt(x_prompt, w, 256)
    y, st = _scan_prompt(r, lw, k, v, kn, bs, 128, N_PAIRS)
    flat = lambda z: z.reshape(bsz * t_len, D_MODEL)
    x1, tok, comb = _back(True, y, flat(bonus), flat(g), flat(gateb), flat(outa), flat(x_prompt), w, 512)
    y_prompt = _moe(tok, comb, x1, w, 1024).reshape(bsz, t_len, D_MODEL)
    st = st.reshape(bsz, N_PAIRS, 2, B_HEAD_DIM, 2, B_HEAD_DIM)
    wkv_p = jnp.stack([st[:, :, 0, :, 0, :], st[:, :, 1, :, 1, :]], axis=2)
    wkv_p = jnp.swapaxes(wkv_p.reshape(bsz, B_HEADS, B_HEAD_DIM, B_HEAD_DIM), -1, -2)

    xs2 = x_sample.reshape(n_s, D_MODEL)
    (outa, gateb, g, bonus, r, lw, k, v, kn, bs, cols_s, vn_s) = _front_sample(xs2, state_shift[0], w)
    y, wkv_s = _scan_sample(r, lw, k, v, kn, bs, state_wkv[0])
    x1, tok, comb = _back(False, y, bonus, g, gateb, outa, xs2, w, n_s)
    y_sample = _moe(tok, comb, x1, w, n_s).reshape(n_s, 1, D_MODEL)

    return (y_prompt, y_sample, wkv_p[None], last.reshape(1, bsz, SHIFT_WIDTH), wkv_s[None], cols_s[None],
            vn_s.reshape(1, n_s, 1, A_HEADS, A_HEAD_DIM))
```

```python
import functools
import math

import jax
import jax.numpy as jnp
from jax import lax
from jax.experimental import pallas as pl
from jax.experimental.pallas import tpu as pltpu

F32 = jnp.float32
BF16 = jnp.bfloat16
HIGHEST = lax.Precision.HIGHEST

D_MODEL = 1024
CHUNK = 128
A_HEADS = 8
A_HEAD_DIM = 128
B_HEADS = 16
B_HEAD_DIM = 64
N_PAIRS = B_HEADS // 2
SHIFT_WIDTH = 3328
OFF_V = 1024
OFF_SHIFT = 2048
OFF_GATE_A = OFF_SHIFT + SHIFT_WIDTH
OFF_GATE_B = OFF_GATE_A + D_MODEL
IN_COLS = OFF_GATE_B + D_MODEL
N_GROUPS = 4
EXPERTS_PER_GROUP = 8
N_EXPERTS = 32
D_EXPERT = 256
ROUTER_LANES = 128
EXPERT_LANE0 = N_GROUPS
RMS_EPS = 1e-6
LN_EPS = 1e-5
GN_EPS = 64e-5

SCAN_CHUNK = 64
FRONT_BLOCK = 256
VMEM_LIMIT = 56 * 1024 * 1024


def _gelu(x):
    return x * (0.5 * (1.0 + jnp.tanh(math.sqrt(2.0 / math.pi) * (x + 0.044715 * (x * x * x)))))


def _sigmoid(x):
    return 1.0 / (1.0 + jnp.exp(-x))


def _softplus(z):
    return jnp.maximum(z, 0.0) + jnp.log(1.0 + jnp.exp(-jnp.abs(z)))


def _rms_norm(x, g):
    return x * lax.rsqrt(jnp.mean(x * x, axis=-1, keepdims=True) + RMS_EPS) * g


def _dot(a, b):
    return jnp.dot(a, b, preferred_element_type=F32)


def _dot_hi(a, b):
    return jnp.dot(a, b, preferred_element_type=F32, precision=HIGHEST)


def _head_sums(z, ones2):
    outs = []
    for p in range(z.shape[1] // 128):
        zp = z[:, p * 128:(p + 1) * 128]
        hi = zp.astype(BF16)
        lo = (zp - hi.astype(F32)).astype(BF16)
        outs.append(_dot(jnp.concatenate([hi, lo], axis=1), ones2))
    return jnp.concatenate(outs, axis=1)


def _front_kernel(is_sample, tm, *refs):
    if is_sample:
        (x_ref, prev_ref, nw_ref, win_ref, lnw_ref, lnb_ref, ws0_ref, bs0_ref, mu_ref, wl_ref, wb_ref,
         al_ref, ab_ref, gl_ref, kk_ref, ka_ref, rk_ref, pa_ref, ones2_ref,
         outa_ref, gateb_ref, g_ref, bonus_ref, r_ref, lw_ref, k_ref, v_ref, kn_ref, bs_ref,
         cols_ref, vn_ref, ya_ref) = refs
    else:
        (x_ref, nw_ref, win_ref, lnw_ref, lnb_ref, wsc_ref, bsb_ref, mu_ref, wl_ref, wb_ref,
         al_ref, ab_ref, gl_ref, kk_ref, ka_ref, rk_ref, pa_ref, ones2_ref,
         outa_ref, gateb_ref, g_ref, bonus_ref, r_ref, lw_ref, k_ref, v_ref, kn_ref, bs_ref,
         last_ref, ya_ref, carry_ref) = refs

        @pl.when(pl.program_id(1) == 0)
        def _():
            carry_ref[...] = jnp.zeros_like(carry_ref)

    xb = _rms_norm(x_ref[...], nw_ref[...]).astype(BF16)
    ones2 = ones2_ref[...]

    def proj(lo, width):
        return _dot(xb, win_ref[:, lo:lo + width])

    def shifted(lo, width):
        cs = slice(lo, lo + width)
        cols = proj(OFF_SHIFT + lo, width)
        if is_sample:
            prev = prev_ref[:, cs]
            cols_ref[:, cs] = cols
        else:
            row = lax.broadcasted_iota(jnp.int32, cols.shape, 0)
            prev = jnp.where(row == 0, carry_ref[0:1, cs], pltpu.roll(cols, 1, 0))
            carry_ref[0:1, cs] = cols[tm - 1:tm, :]
            last_ref[:, cs] = cols[tm - 1:tm, :]
        return cols + (prev - cols) * mu_ref[:, cs]

    tail = shifted(3 * D_MODEL, 256)
    wa = tail[:, 0:128]
    twa = jnp.tanh(wa).astype(BF16)
    wab = wa.astype(BF16)
    sgd = _sigmoid(tail[:, 128:256]).astype(BF16)

    for j in range(D_MODEL // FRONT_BLOCK):
        lo = j * FRONT_BLOCK
        cs = slice(lo, lo + FRONT_BLOCK)

        r = shifted(lo, FRONT_BLOCK)
        k = shifted(D_MODEL + lo, FRONT_BLOCK)
        vb = shifted(2 * D_MODEL + lo, FRONT_BLOCK)
        w_log = -_softplus(-(wb_ref[:, cs] + _dot(twa, wl_ref[:, cs]))) - 0.5
        logw = -jnp.exp(w_log)
        a = _sigmoid(ab_ref[:, cs] + _dot(wab, al_ref[:, cs]))
        g_ref[:, cs] = _dot(sgd, gl_ref[:, cs])
        kk = k * kk_ref[:, cs]
        kkn = kk / jnp.maximum(jnp.sqrt(_head_sums(kk * kk, ones2)), 1e-12)
        k2 = k * (1.0 + (a - 1.0) * ka_ref[:, cs])
        bonus_ref[:, cs] = _head_sums(r * k2 * rk_ref[:, cs], ones2) * vb
        bsc = kkn * a
        outs = ((r_ref, r), (lw_ref, logw), (k_ref, k2), (v_ref, vb), (kn_ref, kkn), (bs_ref, bsc))
        for o_ref, val in outs:
            if is_sample:
                o_ref[:, cs] = val
            else:
                for q in range(FRONT_BLOCK // 128):
                    o_ref[lo // 128 + q] = val[:, q * 128:(q + 1) * 128]

        u2 = _gelu(proj(lo, FRONT_BLOCK))
        v2 = _gelu(proj(OFF_V + lo, FRONT_BLOCK))
        for q in range(FRONT_BLOCK // A_HEAD_DIM):
            h = lo // A_HEAD_DIM + q
            hs = slice(h * A_HEAD_DIM, (h + 1) * A_HEAD_DIM)
            u = u2[:, q * A_HEAD_DIM:(q + 1) * A_HEAD_DIM]
            v = v2[:, q * A_HEAD_DIM:(q + 1) * A_HEAD_DIM]
            mean = jnp.mean(v, axis=-1, keepdims=True)
            vc = v - mean
            var = jnp.mean(vc * vc, axis=-1, keepdims=True)
            vn = vc * lax.rsqrt(var + LN_EPS) * lnw_ref[:, hs] + lnb_ref[:, hs]
            if is_sample:
                vn_ref[:, hs] = vn
                ya_ref[:, hs] = (u * (vn * ws0_ref[:, hs] + bs0_ref[:, hs])).astype(BF16)
            else:
                vnb = vn.astype(BF16)
                for c in range(tm // CHUNK):
                    rs = slice(c * CHUNK, (c + 1) * CHUNK)
                    s = _dot(wsc_ref[h], vnb[rs, :]) + bsb_ref[:, hs]
                    ya_ref[rs, hs] = (u[rs, :] * s).astype(BF16)

        gateb_ref[:, cs] = _sigmoid(proj(OFF_GATE_B + lo, FRONT_BLOCK))

    outa_ref[...] = _sigmoid(proj(OFF_GATE_A, D_MODEL)) * _dot(ya_ref[...], pa_ref[...])


def _const_spec(shape, grid_rank):
    zeros = (0,) * len(shape)
    if grid_rank == 1:
        return pl.BlockSpec(shape, lambda i: zeros, pipeline_mode=pl.Buffered(1))
    return pl.BlockSpec(shape, lambda b, i: zeros, pipeline_mode=pl.Buffered(1))


def _front_prompt(x, w, tm):
    bsz, t_len, _ = x.shape
    nt = t_len // tm
    consts = [w["nw"], w["win"], w["lnw"], w["lnb"], w["wsc"], w["bsb"], w["mu"], w["wl"], w["wb"],
              w["al"], w["ab"], w["gl"], w["kk"], w["ka"], w["rk"], w["pa"], w["ones2"]]
    tok = lambda: pl.BlockSpec((None, tm, D_MODEL), lambda b, i: (b, i, 0))
    pair = lambda: pl.BlockSpec((None, N_PAIRS, tm, 128), lambda b, i: (b, 0, i, 0))
    tok_shape = jax.ShapeDtypeStruct((bsz, t_len, D_MODEL), F32)
    pair_shape = jax.ShapeDtypeStruct((bsz, N_PAIRS, t_len, 128), F32)
    return pl.pallas_call(
        functools.partial(_front_kernel, False, tm),
        grid=(bsz, nt),
        in_specs=[tok()] + [_const_spec(c.shape, 2) for c in consts],
        out_specs=[tok(), tok(), tok(), tok()] + [pair() for _ in range(6)]
        + [pl.BlockSpec((None, 1, SHIFT_WIDTH), lambda b, i: (b, 0, 0))],
        out_shape=[tok_shape] * 4 + [pair_shape] * 6 + [jax.ShapeDtypeStruct((bsz, 1, SHIFT_WIDTH), F32)],
        scratch_shapes=[pltpu.VMEM((tm, D_MODEL), BF16), pltpu.VMEM((8, SHIFT_WIDTH), F32)],
        compiler_params=pltpu.CompilerParams(dimension_semantics=("parallel", "arbitrary"),
                                             vmem_limit_bytes=VMEM_LIMIT),
        name="front_prompt",
    )(x, *consts)


def _front_sample(x, prev, w):
    n = x.shape[0]
    consts = [w["nw"], w["win"], w["lnw"], w["lnb"], w["ws0"], w["bs0"], w["mu"], w["wl"], w["wb"],
              w["al"], w["ab"], w["gl"], w["kk"], w["ka"], w["rk"], w["pa"], w["ones2"]]
    tok = lambda: pl.BlockSpec((n, D_MODEL), lambda i: (0, 0))
    wide = lambda: pl.BlockSpec((n, SHIFT_WIDTH), lambda i: (0, 0))
    tok_shape = jax.ShapeDtypeStruct((n, D_MODEL), F32)
    return pl.pallas_call(
        functools.partial(_front_kernel, True, n),
        grid=(1,),
        in_specs=[tok(), wide()] + [_const_spec(c.shape, 1) for c in consts],
        out_specs=[tok() for _ in range(10)] + [wide(), tok()],
        out_shape=[tok_shape] * 10 + [jax.ShapeDtypeStruct((n, SHIFT_WIDTH), F32), tok_shape],
        scratch_shapes=[pltpu.VMEM((n, D_MODEL), BF16)],
        compiler_params=pltpu.CompilerParams(dimension_semantics=("arbitrary",),
                                             vmem_limit_bytes=VMEM_LIMIT),
        name="front_sample",
    )(x, prev, *consts)


def _split3(x):
    hi = x.astype(BF16)
    r1 = x - hi.astype(F32)
    mid = r1.astype(BF16)
    lo = (r1 - mid.astype(F32)).astype(BF16)
    return hi, mid, lo


def _scan_kernel(tt, npp, r_ref, lw_ref, k_ref, v_ref, kn_ref, bs_ref, y_ref, sout_ref, st_ref):
    c_len = SCAN_CHUNK
    i = pl.program_id(2)

    @pl.when(i == 0)
    def _():
        st_ref[...] = jnp.zeros_like(st_ref)

    lane = lax.broadcasted_iota(jnp.int32, (c_len, 128), 1)
    head0 = lane < B_HEAD_DIM
    row2 = lax.broadcasted_iota(jnp.int32, (128, 128), 0)
    col2 = lax.broadcasted_iota(jnp.int32, (128, 128), 1)
    tpos = jnp.bitwise_and(row2, c_len - 1)
    spos = jnp.bitwise_and(col2, c_len - 1)
    strict = tpos > spos
    incl = tpos >= spos
    eye = row2 == col2
    rc = lax.broadcasted_iota(jnp.int32, (c_len, 3 * c_len), 0)
    cc = jnp.bitwise_and(lax.broadcasted_iota(jnp.int32, (c_len, 3 * c_len), 1), c_len - 1)
    tri3 = jnp.where(rc >= cc, 1.0, 0.0).astype(BF16)

    def pair_diag(z):
        return jnp.concatenate([jnp.where(head0, z, 0.0), jnp.where(head0, 0.0, z)], axis=0).astype(BF16)

    def bdot(a, b):
        return _dot(a.astype(BF16), b.astype(BF16))

    n_chunks = tt // c_len
    insts = [(c, q) for c in range(n_chunks) for q in range(npp)]
    rows = lambda c: slice(c * c_len, (c + 1) * c_len)

    log_p = {}
    for c, q in insts:
        log_p[c, q] = _dot(tri3, jnp.concatenate(_split3(lw_ref[q, rows(c), :]), axis=0))

    ops, gram = {}, {}
    for c, q in insts:
        lp = log_p[c, q]
        kn = kn_ref[q, rows(c), :]
        bs = bs_ref[q, rows(c), :]
        kx = k_ref[q, rows(c), :]
        log_pc = lp[c_len - 1:c_len, :]
        e_inv = jnp.exp(-lp)
        e_dec = jnp.exp(log_pc - lp)
        a_m = pair_diag(-kn * jnp.exp(lp - lw_ref[q, rows(c), :]))
        r_m = pair_diag(r_ref[q, rows(c), :] * jnp.exp(lp))
        v_m = pair_diag(v_ref[q, rows(c), :])
        btkt = jnp.concatenate([pair_diag(bs * e_dec), pair_diag(kx * e_dec)], axis=0)
        pc_col = jnp.sum(jnp.where(eye, jnp.exp(log_pc), 0.0), axis=1, keepdims=True)
        ops[c, q] = (a_m, r_m, v_m, btkt, pc_col)
        gram[c, q] = lax.dot_general(jnp.concatenate([a_m, r_m], axis=0),
                                     jnp.concatenate([pair_diag(bs * e_inv), pair_diag(kx * e_inv)], axis=0),
                                     (((1,), (1,)), ((), ())), preferred_element_type=F32)

    n_sum, pw, l_r, lakv = {}, {}, {}, {}
    for c, q in insts:
        g = gram[c, q]
        l_ab = jnp.where(strict, g[0:128, 0:128], 0.0)
        n_sum[c, q] = l_ab
        pw[c, q] = l_ab.astype(BF16)
        l_r[c, q] = jnp.concatenate([jnp.where(incl, g[128:256, 0:128], 0.0),
                                     jnp.where(incl, g[128:256, 128:256], 0.0)], axis=1).astype(BF16)
        lakv[c, q] = _dot(jnp.where(strict, g[0:128, 128:256], 0.0).astype(BF16), ops[c, q][2])

    for _ in range(int(math.log2(c_len)) - 1):
        for c, q in insts:
            pw[c, q] = _dot(pw[c, q], pw[c, q])
        for c, q in insts:
            n_sum[c, q] = n_sum[c, q] + pw[c, q] + bdot(n_sum[c, q], pw[c, q])
            pw[c, q] = pw[c, q].astype(BF16)

    wu = {}
    for c, q in insts:
        rhs = jnp.concatenate([ops[c, q][0].astype(F32), lakv[c, q]], axis=1)
        wu[c, q] = rhs + bdot(n_sum[c, q], rhs)

    for c in range(n_chunks):
        xs, st = {}, {}
        for q in range(npp):
            st[q] = st_ref[q]
            xs[q] = bdot(jnp.concatenate([wu[c, q][:, 0:128].astype(BF16), ops[c, q][1]], axis=0), st[q])
        for q in range(npp):
            _, _, v_m, btkt, pc_col = ops[c, q]
            uv = jnp.concatenate([(xs[q][0:128, :] + wu[c, q][:, 128:256]).astype(BF16), v_m], axis=0)
            y_m = xs[q][128:256, :] + _dot(l_r[c, q], uv)
            y_ref[q, rows(c), :] = y_m[0:c_len, :] + y_m[c_len:128, :]
            st_ref[q] = pc_col * st[q] + lax.dot_general(btkt, uv, (((0,), (0,)), ((), ())),
                                                         preferred_element_type=F32)

    @pl.when(i == pl.num_programs(2) - 1)
    def _():
        sout_ref[...] = st_ref[...]


def _scan_prompt(r, lw, k, v, kn, bs, tt, npp):
    bsz, n_pairs, t_len, _ = r.shape
    blk = lambda: pl.BlockSpec((None, npp, tt, 128), lambda b, p, i: (b, p, i, 0))
    return pl.pallas_call(
        functools.partial(_scan_kernel, tt, npp),
        grid=(bsz, n_pairs // npp, t_len // tt),
        in_specs=[blk() for _ in range(6)],
        out_specs=[blk(), pl.BlockSpec((None, npp, 128, 128), lambda b, p, i: (b, p, 0, 0))],
        out_shape=[jax.ShapeDtypeStruct((bsz, n_pairs, t_len, 128), F32),
                   jax.ShapeDtypeStruct((bsz, n_pairs, 128, 128), F32)],
        scratch_shapes=[pltpu.VMEM((npp, 128, 128), F32)],
        compiler_params=pltpu.CompilerParams(dimension_semantics=("parallel", "parallel", "arbitrary"),
                                             vmem_limit_bytes=VMEM_LIMIT),
        name="scan_prompt",
    )(r, lw, k, v, kn, bs)


def _step_kernel(tb, r_ref, lw_ref, k_ref, v_ref, kn_ref, bs_ref, s_ref, y_ref, sout_ref):
    n = B_HEAD_DIM
    row = lax.broadcasted_iota(jnp.int32, (128, 128), 0)
    col = lax.broadcasted_iota(jnp.int32, (128, 128), 1)
    same_head = (row < n) == (col < n)
    eye = jnp.where(row == col, 1.0, 0.0).astype(BF16)
    zero_f = jnp.zeros((n, n), F32)
    zero_b = jnp.zeros((128, 128), BF16)
    tn = (((1,), (1,)), ((), ()))
    insts = [(t, p) for t in range(tb) for p in range(N_PAIRS)]
    ps = lambda p: slice(p * 128, (p + 1) * 128)
    rep = lambda z: jnp.broadcast_to(z, (128, 128))

    s0, sav = {}, {}
    for t, p in insts:
        s0[t, p] = jnp.concatenate([jnp.concatenate([s_ref[t, 2 * p], zero_f], axis=1),
                                    jnp.concatenate([zero_f, s_ref[t, 2 * p + 1]], axis=1)], axis=0)
        v_row = v_ref[t, :, ps(p)]
        v_hi = v_row.astype(BF16)
        v_lo = (v_row - v_hi.astype(F32)).astype(BF16)
        lhs = jnp.concatenate([s0[t, p].astype(BF16), eye, eye], axis=1)
        rhs = jnp.concatenate(
            [jnp.concatenate([rep((-kn_ref[t, :, ps(p)]).astype(BF16)), zero_b, zero_b], axis=1),
             jnp.concatenate([zero_b, rep(v_hi), rep(v_lo)], axis=1)], axis=0)
        sav[t, p] = lax.dot_general(lhs, rhs, tn, preferred_element_type=F32)
    s1 = {}
    for t, p in insts:
        upd = (s0[t, p] * jnp.exp(lw_ref[t, :, ps(p)]) + sav[t, p][:, 0:128] * bs_ref[t, :, ps(p)]
               + sav[t, p][:, 128:256] * k_ref[t, :, ps(p)])
        upd = jnp.where(same_head, upd, 0.0)
        sout_ref[t, 2 * p] = upd[0:n, 0:n]
        sout_ref[t, 2 * p + 1] = upd[n:128, n:128]
        s1[t, p] = upd.astype(BF16)
    for t, p in insts:
        r8 = jnp.broadcast_to(r_ref[t, :, ps(p)].astype(BF16), (8, 128))
        y_ref[t, :, ps(p)] = lax.dot_general(r8, s1[t, p], tn, preferred_element_type=F32)[0:1, :]


def _scan_sample(r, lw, k, v, kn, bs, s0, tb):
    n = r.shape[0]
    vecs = [z.reshape(n, 1, D_MODEL) for z in (r, lw, k, v, kn, bs)]
    vec = lambda: pl.BlockSpec((tb, 1, D_MODEL), lambda i: (i, 0, 0))
    st = lambda: pl.BlockSpec((tb, B_HEADS, B_HEAD_DIM, B_HEAD_DIM), lambda i: (i, 0, 0, 0))
    y, s1 = pl.pallas_call(
        functools.partial(_step_kernel, tb),
        grid=(n // tb,),
        in_specs=[vec() for _ in range(6)] + [st()],
        out_specs=[vec(), st()],
        out_shape=[jax.ShapeDtypeStruct((n, 1, D_MODEL), F32), jax.ShapeDtypeStruct(s0.shape, F32)],
        compiler_params=pltpu.CompilerParams(dimension_semantics=("parallel",)),
        name="scan_sample",
    )(*vecs, s0)
    return y.reshape(n, D_MODEL), s1


def _back_kernel(paired, y_ref, bonus_ref, g_ref, gateb_ref, outa_ref, x_ref, lxw_ref, lxb_ref, pb_ref,
                 wo_ref, nfw_ref, rw_ref, rb_ref, ones2_ref, x1_ref, tok_ref, comb_ref):
    if paired:
        y = jnp.concatenate([y_ref[p] for p in range(N_PAIRS)], axis=1)
    else:
        y = y_ref[...]
    ones2 = ones2_ref[...]
    inv_n = 1.0 / B_HEAD_DIM
    yc = y - _head_sums(y, ones2) * inv_n
    var = _head_sums(yc * yc, ones2) * inv_n
    yn = yc * lax.rsqrt(var + GN_EPS) * lxw_ref[...] + lxb_ref[...]
    yb = ((yn + bonus_ref[...]) * g_ref[...]).astype(BF16)
    merged = outa_ref[...] + gateb_ref[...] * _dot(yb, pb_ref[...])
    x1 = x_ref[...] + _dot(merged.astype(BF16), wo_ref[...])
    x1_ref[...] = x1
    tok = _rms_norm(x1, nfw_ref[...])
    tok_ref[...] = tok.astype(BF16)

    logits = _dot_hi(tok, rw_ref[...]) + rb_ref[...]
    lane = lax.broadcasted_iota(jnp.int32, logits.shape, 1).astype(F32)
    neg = jnp.float32(-jnp.inf)
    big = jnp.float32(1 << 20)
    is_grp = lane < N_GROUPS
    lg = jnp.where(is_grp, logits, neg)
    gmax = jnp.max(lg, axis=-1, keepdims=True)
    grp = jnp.min(jnp.where(lg == gmax, lane, big), axis=-1, keepdims=True)
    p_grp = 1.0 / jnp.sum(jnp.where(is_grp, jnp.exp(lg - gmax), 0.0), axis=-1, keepdims=True)
    lo = EXPERT_LANE0 + grp * EXPERTS_PER_GROUP
    in_grp = (lane >= lo) & (lane < lo + EXPERTS_PER_GROUP)
    le = jnp.where(in_grp, logits, neg)
    top1 = jnp.max(le, axis=-1, keepdims=True)
    i1 = jnp.min(jnp.where(le == top1, lane, big), axis=-1, keepdims=True)
    le2 = jnp.where(lane == i1, neg, le)
    top2 = jnp.max(le2, axis=-1, keepdims=True)
    i2 = jnp.min(jnp.where(le2 == top2, lane, big), axis=-1, keepdims=True)
    e2 = jnp.exp(top2 - top1)
    w1 = p_grp / (1.0 + e2)
    w2 = p_grp * e2 / (1.0 + e2)
    comb_ref[...] = jnp.where(lane == i1, w1, 0.0) + jnp.where(lane == i2, w2, 0.0)


def _back(paired, y, bonus, g, gateb, outa, x, w, tm):
    n = x.shape[0]
    consts = [w["lxw"], w["lxb"], w["pb"], w["wo"], w["nfw"], w["rw"], w["rb"], w["ones2"]]
    tok = lambda: pl.BlockSpec((tm, D_MODEL), lambda i: (i, 0))
    if paired:
        nt = y.shape[2] // tm
        y_spec = pl.BlockSpec((None, N_PAIRS, tm, 128), lambda i: (i // nt, 0, i % nt, 0))
    else:
        y_spec = tok()
    return pl.pallas_call(
        functools.partial(_back_kernel, paired),
        grid=(n // tm,),
        in_specs=[y_spec] + [tok() for _ in range(5)] + [_const_spec(c.shape, 1) for c in consts],
        out_specs=[tok(), tok(), pl.BlockSpec((tm, ROUTER_LANES), lambda i: (i, 0))],
        out_shape=[jax.ShapeDtypeStruct((n, D_MODEL), F32), jax.ShapeDtypeStruct((n, D_MODEL), BF16),
                   jax.ShapeDtypeStruct((n, ROUTER_LANES), F32)],
        compiler_params=pltpu.CompilerParams(dimension_semantics=("parallel",), vmem_limit_bytes=VMEM_LIMIT),
        name="back_prompt" if paired else "back_sample",
    )(y, bonus, g, gateb, outa, x, *consts)


def _moe_kernel(tok_ref, comb_ref, x1_ref, wg_ref, wu_ref, wd_ref, fw_ref, o_ref, acc_ref):
    e = pl.program_id(1)

    @pl.when(e == 0)
    def _():
        acc_ref[...] = jnp.zeros_like(acc_ref)

    t = tok_ref[...]
    comb = comb_ref[...]
    lane = lax.broadcasted_iota(jnp.int32, comb.shape, 1)
    c = jnp.sum(jnp.where(lane == EXPERT_LANE0 + e, comb, 0.0), axis=-1, keepdims=True)
    gate = _dot(t, wg_ref[...])
    h = gate * _sigmoid(gate) * _dot(t, wu_ref[...])
    acc_ref[...] += c * _dot(h.astype(BF16), wd_ref[...])

    @pl.when(e == N_EXPERTS - 1)
    def _():
        o_ref[...] = _rms_norm(x1_ref[...] + acc_ref[...], fw_ref[...])


def _moe(tok, comb, x1, w, tm):
    n = tok.shape[0]
    return pl.pallas_call(
        _moe_kernel,
        grid=(n // tm, N_EXPERTS),
        in_specs=[pl.BlockSpec((tm, D_MODEL), lambda i, e: (i, 0)),
                  pl.BlockSpec((tm, ROUTER_LANES), lambda i, e: (i, 0)),
                  pl.BlockSpec((tm, D_MODEL), lambda i, e: (i, 0)),
                  pl.BlockSpec((None, D_MODEL, D_EXPERT), lambda i, e: (e, 0, 0)),
                  pl.BlockSpec((None, D_MODEL, D_EXPERT), lambda i, e: (e, 0, 0)),
                  pl.BlockSpec((None, D_EXPERT, D_MODEL), lambda i, e: (e, 0, 0)),
                  pl.BlockSpec((1, D_MODEL), lambda i, e: (0, 0))],
        out_specs=pl.BlockSpec((tm, D_MODEL), lambda i, e: (i, 0)),
        out_shape=jax.ShapeDtypeStruct((n, D_MODEL), F32),
        scratch_shapes=[pltpu.VMEM((tm, D_MODEL), F32)],
        compiler_params=pltpu.CompilerParams(dimension_semantics=("parallel", "arbitrary"),
                                             vmem_limit_bytes=VMEM_LIMIT),
        name="moe",
    )(tok, comb, x1, w["wg"], w["wu"], w["wd"], w["fw"])


def _prep_weights(norm_mix_w, w_in, sgu_ln_w, sgu_ln_b, sgu_w_s, sgu_b, rwkv_mu, w_lora_up, w_bias,
                  a_lora_up, a_bias, g_lora_up, k_k, k_a, r_k, lnx_w, lnx_b, proj_a, proj_b, w_out,
                  norm_ffn_w, router_group, router_group_bias, router_expert, router_expert_bias,
                  moe_w_gate, moe_w_up, moe_w_down, norm_final_w):
    row = lambda z: z.reshape(1, -1).astype(F32)
    causal = jnp.tril(jnp.ones((CHUNK, CHUNK), dtype=bool))
    zeros_lora = jnp.zeros((64, D_MODEL), F32)
    head = jnp.arange(128) // B_HEAD_DIM
    ones_bd = (head[:, None] == head[None, :]).astype(BF16)
    pad = ROUTER_LANES - N_GROUPS - N_EXPERTS
    return {
        "nw": row(norm_mix_w), "win": w_in.astype(BF16), "lnw": row(sgu_ln_w), "lnb": row(sgu_ln_b),
        "wsc": jnp.where(causal[None], sgu_w_s, 0.0).astype(BF16),
        "bsb": jnp.repeat(jnp.transpose(sgu_b), A_HEAD_DIM, axis=1),
        "ws0": jnp.repeat(sgu_w_s[:, 0, 0], A_HEAD_DIM).reshape(1, -1),
        "bs0": jnp.repeat(sgu_b[:, 0], A_HEAD_DIM).reshape(1, -1),
        "mu": row(rwkv_mu),
        "wl": jnp.concatenate([w_lora_up, zeros_lora], axis=0).astype(BF16), "wb": row(w_bias),
        "al": jnp.concatenate([zeros_lora, a_lora_up], axis=0).astype(BF16), "ab": row(a_bias),
        "gl": g_lora_up.astype(BF16), "kk": row(k_k), "ka": row(k_a), "rk": row(r_k),
        "pa": proj_a.astype(BF16), "ones2": jnp.concatenate([ones_bd, ones_bd], axis=0),
        "lxw": row(lnx_w), "lxb": row(lnx_b), "pb": proj_b.astype(BF16), "wo": w_out.astype(BF16),
        "nfw": row(norm_ffn_w),
        "rw": jnp.concatenate([router_group, router_expert, jnp.zeros((D_MODEL, pad), F32)], axis=1),
        "rb": jnp.concatenate([router_group_bias, router_expert_bias, jnp.zeros((pad,), F32)]).reshape(1, -1),
        "wg": moe_w_gate.astype(BF16), "wu": moe_w_up.astype(BF16), "wd": moe_w_down.astype(BF16),
        "fw": row(norm_final_w),
    }


def kernel(x_prompt, x_sample, state_wkv, state_shift, norm_mix_w, w_in, sgu_ln_w, sgu_ln_b, sgu_w_s, sgu_b, rwkv_mu, w_lora_up, w_bias, a_lora_up, a_bias, g_lora_up, k_k, k_a, r_k, lnx_w, lnx_b, proj_a, proj_b, w_out, norm_ffn_w, router_group, router_group_bias, router_expert, router_expert_bias, moe_w_gate, moe_w_up, moe_w_down, norm_final_w):
    layer = [z[0] for z in (norm_mix_w, w_in, sgu_ln_w, sgu_ln_b, sgu_w_s, sgu_b, rwkv_mu, w_lora_up, w_bias,
                            a_lora_up, a_bias, g_lora_up, k_k, k_a, r_k, lnx_w, lnx_b, proj_a, proj_b, w_out,
                            norm_ffn_w, router_group, router_group_bias, router_expert, router_expert_bias,
                            moe_w_gate, moe_w_up, moe_w_down)]
    w = _prep_weights(*layer, norm_final_w)
    bsz, t_len, _ = x_prompt.shape
    n_s = x_sample.shape[0]

    (outa, gateb, g, bonus, r, lw, k, v, kn, bs, last) = _front_prompt(x_prompt, w, 256)
    y, st = _scan_prompt(r, lw, k, v, kn, bs, 128, N_PAIRS)
    flat = lambda z: z.reshape(bsz * t_len, D_MODEL)
    x1, tok, comb = _back(True, y, flat(bonus), flat(g), flat(gateb), flat(outa), flat(x_prompt), w, 512)
    y_prompt = _moe(tok, comb, x1, w, 1024).reshape(bsz, t_len, D_MODEL)
    st = st.reshape(bsz, N_PAIRS, 2, B_HEAD_DIM, 2, B_HEAD_DIM)
    wkv_p = jnp.stack([st[:, :, 0, :, 0, :], st[:, :, 1, :, 1, :]], axis=2)
    wkv_p = jnp.swapaxes(wkv_p.reshape(bsz, B_HEADS, B_HEAD_DIM, B_HEAD_DIM), -1, -2)

    xs2 = x_sample.reshape(n_s, D_MODEL)
    (outa, gateb, g, bonus, r, lw, k, v, kn, bs, cols_s, vn_s) = _front_sample(xs2, state_shift[0], w)
    y, wkv_s = _scan_sample(r, lw, k, v, kn, bs, state_wkv[0], 4)
    x1, tok, comb = _back(False, y, bonus, g, gateb, outa, xs2, w, n_s)
    y_sample = _moe(tok, comb, x1, w, n_s).reshape(n_s, 1, D_MODEL)

    return (y_prompt, y_sample, wkv_p[None], last.reshape(1, bsz, SHIFT_WIDTH), wkv_s[None], cols_s[None],
            vn_s.reshape(1, n_s, 1, A_HEADS, A_HEAD_DIM))
```

```python
import functools
import math

import jax
import jax.numpy as jnp
from jax import lax
from jax.experimental import pallas as pl
from jax.experimental.pallas import tpu as pltpu

F32 = jnp.float32
BF16 = jnp.bfloat16
HIGHEST = lax.Precision.HIGHEST

D_MODEL = 1024
CHUNK = 128
A_HEADS = 8
A_HEAD_DIM = 128
B_HEADS = 16
B_HEAD_DIM = 64
N_PAIRS = B_HEADS // 2
SHIFT_WIDTH = 3328
OFF_V = 1024
OFF_SHIFT = 2048
OFF_GATE_A = OFF_SHIFT + SHIFT_WIDTH
OFF_GATE_B = OFF_GATE_A + D_MODEL
IN_COLS = OFF_GATE_B + D_MODEL
N_GROUPS = 4
EXPERTS_PER_GROUP = 8
N_EXPERTS = 32
D_EXPERT = 256
ROUTER_ROWS = 40
EXPERT_ROW0 = 8
SEG_ALIGN = 16
MOE_WINDOW = 64
RMS_EPS = 1e-6
LN_EPS = 1e-5
GN_EPS = 64e-5

SCAN_CHUNK = 64
FRONT_BLOCK = 256
VMEM_LIMIT = 56 * 1024 * 1024


def _gelu(x):
    return x * (0.5 * (1.0 + jnp.tanh(math.sqrt(2.0 / math.pi) * (x + 0.044715 * (x * x * x)))))


def _sigmoid(x):
    return 1.0 / (1.0 + jnp.exp(-x))


def _softplus(z):
    return jnp.maximum(z, 0.0) + jnp.log(1.0 + jnp.exp(-jnp.abs(z)))


def _rms_norm(x, g):
    return x * lax.rsqrt(jnp.mean(x * x, axis=-1, keepdims=True) + RMS_EPS) * g


def _dot(a, b):
    return jnp.dot(a, b, preferred_element_type=F32)


def _dot_hi(a, b):
    return jnp.dot(a, b, preferred_element_type=F32, precision=HIGHEST)


def _head_sums(z, ones2):
    outs = []
    for p in range(z.shape[1] // 128):
        zp = z[:, p * 128:(p + 1) * 128]
        hi = zp.astype(BF16)
        lo = (zp - hi.astype(F32)).astype(BF16)
        outs.append(_dot(jnp.concatenate([hi, lo], axis=1), ones2))
    return jnp.concatenate(outs, axis=1)


def _front_kernel(is_sample, tm, *refs):
    if is_sample:
        (x_ref, prev_ref, nw_ref, win_ref, lnw_ref, lnb_ref, ws0_ref, bs0_ref, mu_ref, wl_ref, wb_ref,
         al_ref, ab_ref, gl_ref, kk_ref, ka_ref, rk_ref, pa_ref, ones2_ref,
         outa_ref, gateb_ref, g_ref, bonus_ref, r_ref, lw_ref, k_ref, v_ref, kn_ref, bs_ref,
         cols_ref, vn_ref, ya_ref) = refs
    else:
        (x_ref, nw_ref, win_ref, lnw_ref, lnb_ref, wsc_ref, bsb_ref, mu_ref, wl_ref, wb_ref,
         al_ref, ab_ref, gl_ref, kk_ref, ka_ref, rk_ref, pa_ref, ones2_ref,
         outa_ref, gateb_ref, g_ref, bonus_ref, r_ref, lw_ref, k_ref, v_ref, kn_ref, bs_ref,
         last_ref, ya_ref, carry_ref) = refs

        @pl.when(pl.program_id(1) == 0)
        def _():
            carry_ref[...] = jnp.zeros_like(carry_ref)

    xb = _rms_norm(x_ref[...], nw_ref[...]).astype(BF16)
    ones2 = ones2_ref[...]

    def proj(lo, width):
        return _dot(xb, win_ref[:, lo:lo + width])

    def shifted(lo, width):
        cs = slice(lo, lo + width)
        cols = proj(OFF_SHIFT + lo, width)
        if is_sample:
            prev = prev_ref[:, cs]
            cols_ref[:, cs] = cols
        else:
            row = lax.broadcasted_iota(jnp.int32, cols.shape, 0)
            prev = jnp.where(row == 0, carry_ref[0:1, cs], pltpu.roll(cols, 1, 0))
            carry_ref[0:1, cs] = cols[tm - 1:tm, :]
            last_ref[:, cs] = cols[tm - 1:tm, :]
        return cols + (prev - cols) * mu_ref[:, cs]

    tail = shifted(3 * D_MODEL, 256)
    wa = tail[:, 0:128]
    twa = jnp.tanh(wa).astype(BF16)
    wab = wa.astype(BF16)
    sgd = _sigmoid(tail[:, 128:256]).astype(BF16)

    for j in range(D_MODEL // FRONT_BLOCK):
        lo = j * FRONT_BLOCK
        cs = slice(lo, lo + FRONT_BLOCK)

        r = shifted(lo, FRONT_BLOCK)
        k = shifted(D_MODEL + lo, FRONT_BLOCK)
        vb = shifted(2 * D_MODEL + lo, FRONT_BLOCK)
        w_log = -_softplus(-(wb_ref[:, cs] + _dot(twa, wl_ref[:, cs]))) - 0.5
        logw = -jnp.exp(w_log)
        a = _sigmoid(ab_ref[:, cs] + _dot(wab, al_ref[:, cs]))
        g_ref[:, cs] = _dot(sgd, gl_ref[:, cs])
        kk = k * kk_ref[:, cs]
        kkn = kk / jnp.maximum(jnp.sqrt(_head_sums(kk * kk, ones2)), 1e-12)
        k2 = k * (1.0 + (a - 1.0) * ka_ref[:, cs])
        bonus_ref[:, cs] = _head_sums(r * k2 * rk_ref[:, cs], ones2) * vb
        bsc = kkn * a
        outs = ((r_ref, r), (lw_ref, logw), (k_ref, k2), (v_ref, vb), (kn_ref, kkn), (bs_ref, bsc))
        for o_ref, val in outs:
            if is_sample:
                o_ref[:, cs] = val
            else:
                for q in range(FRONT_BLOCK // 128):
                    o_ref[lo // 128 + q] = val[:, q * 128:(q + 1) * 128]

        u2 = _gelu(proj(lo, FRONT_BLOCK))
        v2 = _gelu(proj(OFF_V + lo, FRONT_BLOCK))
        for q in range(FRONT_BLOCK // A_HEAD_DIM):
            h = lo // A_HEAD_DIM + q
            hs = slice(h * A_HEAD_DIM, (h + 1) * A_HEAD_DIM)
            u = u2[:, q * A_HEAD_DIM:(q + 1) * A_HEAD_DIM]
            v = v2[:, q * A_HEAD_DIM:(q + 1) * A_HEAD_DIM]
            mean = jnp.mean(v, axis=-1, keepdims=True)
            vc = v - mean
            var = jnp.mean(vc * vc, axis=-1, keepdims=True)
            vn = vc * lax.rsqrt(var + LN_EPS) * lnw_ref[:, hs] + lnb_ref[:, hs]
            if is_sample:
                vn_ref[:, hs] = vn
                ya_ref[:, hs] = (u * (vn * ws0_ref[:, hs] + bs0_ref[:, hs])).astype(BF16)
            else:
                vnb = vn.astype(BF16)
                for c in range(tm // CHUNK):
                    rs = slice(c * CHUNK, (c + 1) * CHUNK)
                    s = _dot(wsc_ref[h], vnb[rs, :]) + bsb_ref[:, hs]
                    ya_ref[rs, hs] = (u[rs, :] * s).astype(BF16)

        gateb_ref[:, cs] = _sigmoid(proj(OFF_GATE_B + lo, FRONT_BLOCK))

    outa_ref[...] = _sigmoid(proj(OFF_GATE_A, D_MODEL)) * _dot(ya_ref[...], pa_ref[...])


def _const_spec(shape, grid_rank):
    zeros = (0,) * len(shape)
    if grid_rank == 1:
        return pl.BlockSpec(shape, lambda i: zeros, pipeline_mode=pl.Buffered(1))
    return pl.BlockSpec(shape, lambda b, i: zeros, pipeline_mode=pl.Buffered(1))


def _front_prompt(x, w, tm):
    bsz, t_len, _ = x.shape
    nt = t_len // tm
    consts = [w["nw"], w["win"], w["lnw"], w["lnb"], w["wsc"], w["bsb"], w["mu"], w["wl"], w["wb"],
              w["al"], w["ab"], w["gl"], w["kk"], w["ka"], w["rk"], w["pa"], w["ones2"]]
    tok = lambda: pl.BlockSpec((None, tm, D_MODEL), lambda b, i: (b, i, 0))
    pair = lambda: pl.BlockSpec((None, N_PAIRS, tm, 128), lambda b, i: (b, 0, i, 0))
    tok_shape = jax.ShapeDtypeStruct((bsz, t_len, D_MODEL), F32)
    pair_shape = jax.ShapeDtypeStruct((bsz, N_PAIRS, t_len, 128), F32)
    return pl.pallas_call(
        functools.partial(_front_kernel, False, tm),
        grid=(bsz, nt),
        in_specs=[tok()] + [_const_spec(c.shape, 2) for c in consts],
        out_specs=[tok(), tok(), tok(), tok()] + [pair() for _ in range(6)]
        + [pl.BlockSpec((None, 1, SHIFT_WIDTH), lambda b, i: (b, 0, 0))],
        out_shape=[tok_shape] * 4 + [pair_shape] * 6 + [jax.ShapeDtypeStruct((bsz, 1, SHIFT_WIDTH), F32)],
        scratch_shapes=[pltpu.VMEM((tm, D_MODEL), BF16), pltpu.VMEM((8, SHIFT_WIDTH), F32)],
        compiler_params=pltpu.CompilerParams(dimension_semantics=("parallel", "arbitrary"),
                                             vmem_limit_bytes=VMEM_LIMIT),
        name="front_prompt",
    )(x, *consts)


def _front_sample(x, prev, w):
    n = x.shape[0]
    consts = [w["nw"], w["win"], w["lnw"], w["lnb"], w["ws0"], w["bs0"], w["mu"], w["wl"], w["wb"],
              w["al"], w["ab"], w["gl"], w["kk"], w["ka"], w["rk"], w["pa"], w["ones2"]]
    tok = lambda: pl.BlockSpec((n, D_MODEL), lambda i: (0, 0))
    wide = lambda: pl.BlockSpec((n, SHIFT_WIDTH), lambda i: (0, 0))
    tok_shape = jax.ShapeDtypeStruct((n, D_MODEL), F32)
    return pl.pallas_call(
        functools.partial(_front_kernel, True, n),
        grid=(1,),
        in_specs=[tok(), wide()] + [_const_spec(c.shape, 1) for c in consts],
        out_specs=[tok() for _ in range(10)] + [wide(), tok()],
        out_shape=[tok_shape] * 10 + [jax.ShapeDtypeStruct((n, SHIFT_WIDTH), F32), tok_shape],
        scratch_shapes=[pltpu.VMEM((n, D_MODEL), BF16)],
        compiler_params=pltpu.CompilerParams(dimension_semantics=("arbitrary",),
                                             vmem_limit_bytes=VMEM_LIMIT),
        name="front_sample",
    )(x, prev, *consts)


def _split3(x):
    hi = x.astype(BF16)
    r1 = x - hi.astype(F32)
    mid = r1.astype(BF16)
    lo = (r1 - mid.astype(F32)).astype(BF16)
    return hi, mid, lo


def _scan_kernel(tt, npp, r_ref, lw_ref, k_ref, v_ref, kn_ref, bs_ref, y_ref, sout_ref, st_ref):
    c_len = SCAN_CHUNK
    i = pl.program_id(2)

    @pl.when(i == 0)
    def _():
        st_ref[...] = jnp.zeros_like(st_ref)

    lane = lax.broadcasted_iota(jnp.int32, (c_len, 128), 1)
    head0 = lane < B_HEAD_DIM
    row2 = lax.broadcasted_iota(jnp.int32, (128, 128), 0)
    col2 = lax.broadcasted_iota(jnp.int32, (128, 128), 1)
    tpos = jnp.bitwise_and(row2, c_len - 1)
    spos = jnp.bitwise_and(col2, c_len - 1)
    strict = tpos > spos
    incl = tpos >= spos
    eye = row2 == col2
    rc = lax.broadcasted_iota(jnp.int32, (c_len, 3 * c_len), 0)
    cc = jnp.bitwise_and(lax.broadcasted_iota(jnp.int32, (c_len, 3 * c_len), 1), c_len - 1)
    tri3 = jnp.where(rc >= cc, 1.0, 0.0).astype(BF16)

    def pair_diag(z):
        return jnp.concatenate([jnp.where(head0, z, 0.0), jnp.where(head0, 0.0, z)], axis=0).astype(BF16)

    def bdot(a, b):
        return _dot(a.astype(BF16), b.astype(BF16))

    n_chunks = tt // c_len
    insts = [(c, q) for c in range(n_chunks) for q in range(npp)]
    rows = lambda c: slice(c * c_len, (c + 1) * c_len)

    log_p = {}
    for c, q in insts:
        log_p[c, q] = _dot(tri3, jnp.concatenate(_split3(lw_ref[q, rows(c), :]), axis=0))

    ops, gram = {}, {}
    for c, q in insts:
        lp = log_p[c, q]
        kn = kn_ref[q, rows(c), :]
        bs = bs_ref[q, rows(c), :]
        kx = k_ref[q, rows(c), :]
        log_pc = lp[c_len - 1:c_len, :]
        e_inv = jnp.exp(-lp)
        e_dec = jnp.exp(log_pc - lp)
        a_m = pair_diag(-kn * jnp.exp(lp - lw_ref[q, rows(c), :]))
        r_m = pair_diag(r_ref[q, rows(c), :] * jnp.exp(lp))
        v_m = pair_diag(v_ref[q, rows(c), :])
        btkt = jnp.concatenate([pair_diag(bs * e_dec), pair_diag(kx * e_dec)], axis=0)
        pc_col = jnp.sum(jnp.where(eye, jnp.exp(log_pc), 0.0), axis=1, keepdims=True)
        ops[c, q] = (a_m, r_m, v_m, btkt, pc_col)
        gram[c, q] = lax.dot_general(jnp.concatenate([a_m, r_m], axis=0),
                                     jnp.concatenate([pair_diag(bs * e_inv), pair_diag(kx * e_inv)], axis=0),
                                     (((1,), (1,)), ((), ())), preferred_element_type=F32)

    n_sum, pw, l_r, lakv = {}, {}, {}, {}
    for c, q in insts:
        g = gram[c, q]
        l_ab = jnp.where(strict, g[0:128, 0:128], 0.0)
        n_sum[c, q] = l_ab
        pw[c, q] = l_ab.astype(BF16)
        l_r[c, q] = jnp.concatenate([jnp.where(incl, g[128:256, 0:128], 0.0),
                                     jnp.where(incl, g[128:256, 128:256], 0.0)], axis=1).astype(BF16)
        lakv[c, q] = _dot(jnp.where(strict, g[0:128, 128:256], 0.0).astype(BF16), ops[c, q][2])

    for _ in range(int(math.log2(c_len)) - 1):
        for c, q in insts:
            pw[c, q] = _dot(pw[c, q], pw[c, q])
        for c, q in insts:
            n_sum[c, q] = n_sum[c, q] + pw[c, q] + bdot(n_sum[c, q], pw[c, q])
            pw[c, q] = pw[c, q].astype(BF16)

    wu = {}
    for c, q in insts:
        rhs = jnp.concatenate([ops[c, q][0].astype(F32), lakv[c, q]], axis=1)
        wu[c, q] = rhs + bdot(n_sum[c, q], rhs)

    for c in range(n_chunks):
        xs, st = {}, {}
        for q in range(npp):
            st[q] = st_ref[q]
            xs[q] = bdot(jnp.concatenate([wu[c, q][:, 0:128].astype(BF16), ops[c, q][1]], axis=0), st[q])
        for q in range(npp):
            _, _, v_m, btkt, pc_col = ops[c, q]
            uv = jnp.concatenate([(xs[q][0:128, :] + wu[c, q][:, 128:256]).astype(BF16), v_m], axis=0)
            y_m = xs[q][128:256, :] + _dot(l_r[c, q], uv)
            y_ref[q, rows(c), :] = y_m[0:c_len, :] + y_m[c_len:128, :]
            st_ref[q] = pc_col * st[q] + lax.dot_general(btkt, uv, (((0,), (0,)), ((), ())),
                                                         preferred_element_type=F32)

    @pl.when(i == pl.num_programs(2) - 1)
    def _():
        sout_ref[...] = st_ref[...]


def _scan_prompt(r, lw, k, v, kn, bs, tt, npp):
    bsz, n_pairs, t_len, _ = r.shape
    blk = lambda: pl.BlockSpec((None, npp, tt, 128), lambda b, p, i: (b, p, i, 0))
    return pl.pallas_call(
        functools.partial(_scan_kernel, tt, npp),
        grid=(bsz, n_pairs // npp, t_len // tt),
        in_specs=[blk() for _ in range(6)],
        out_specs=[blk(), pl.BlockSpec((None, npp, 128, 128), lambda b, p, i: (b, p, 0, 0))],
        out_shape=[jax.ShapeDtypeStruct((bsz, n_pairs, t_len, 128), F32),
                   jax.ShapeDtypeStruct((bsz, n_pairs, 128, 128), F32)],
        scratch_shapes=[pltpu.VMEM((npp, 128, 128), F32)],
        compiler_params=pltpu.CompilerParams(dimension_semantics=("parallel", "parallel", "arbitrary"),
                                             vmem_limit_bytes=VMEM_LIMIT),
        name="scan_prompt",
    )(r, lw, k, v, kn, bs)


def _step_kernel(tb, r_ref, lw_ref, k_ref, v_ref, kn_ref, bs_ref, s_ref, y_ref, sout_ref):
    n = B_HEAD_DIM
    row = lax.broadcasted_iota(jnp.int32, (128, 128), 0)
    col = lax.broadcasted_iota(jnp.int32, (128, 128), 1)
    same_head = (row < n) == (col < n)
    eye = jnp.where(row == col, 1.0, 0.0).astype(BF16)
    zero_f = jnp.zeros((n, n), F32)
    zero_b = jnp.zeros((128, 128), BF16)
    tn = (((1,), (1,)), ((), ()))
    insts = [(t, p) for t in range(tb) for p in range(N_PAIRS)]
    ps = lambda p: slice(p * 128, (p + 1) * 128)
    rep = lambda z: jnp.broadcast_to(z, (128, 128))

    s0, sav = {}, {}
    for t, p in insts:
        s0[t, p] = jnp.concatenate([jnp.concatenate([s_ref[t, 2 * p], zero_f], axis=1),
                                    jnp.concatenate([zero_f, s_ref[t, 2 * p + 1]], axis=1)], axis=0)
        v_row = v_ref[t, :, ps(p)]
        v_hi = v_row.astype(BF16)
        v_lo = (v_row - v_hi.astype(F32)).astype(BF16)
        lhs = jnp.concatenate([s0[t, p].astype(BF16), eye, eye], axis=1)
        rhs = jnp.concatenate(
            [jnp.concatenate([rep((-kn_ref[t, :, ps(p)]).astype(BF16)), zero_b, zero_b], axis=1),
             jnp.concatenate([zero_b, rep(v_hi), rep(v_lo)], axis=1)], axis=0)
        sav[t, p] = lax.dot_general(lhs, rhs, tn, preferred_element_type=F32)
    s1 = {}
    for t, p in insts:
        upd = (s0[t, p] * jnp.exp(lw_ref[t, :, ps(p)]) + sav[t, p][:, 0:128] * bs_ref[t, :, ps(p)]
               + sav[t, p][:, 128:256] * k_ref[t, :, ps(p)])
        upd = jnp.where(same_head, upd, 0.0)
        sout_ref[t, 2 * p] = upd[0:n, 0:n]
        sout_ref[t, 2 * p + 1] = upd[n:128, n:128]
        s1[t, p] = upd.astype(BF16)
    for t, p in insts:
        r8 = jnp.broadcast_to(r_ref[t, :, ps(p)].astype(BF16), (8, 128))
        y_ref[t, :, ps(p)] = lax.dot_general(r8, s1[t, p], tn, preferred_element_type=F32)[0:1, :]


def _scan_sample(r, lw, k, v, kn, bs, s0, tb):
    n = r.shape[0]
    vecs = [z.reshape(n, 1, D_MODEL) for z in (r, lw, k, v, kn, bs)]
    vec = lambda: pl.BlockSpec((tb, 1, D_MODEL), lambda i: (i, 0, 0))
    st = lambda: pl.BlockSpec((tb, B_HEADS, B_HEAD_DIM, B_HEAD_DIM), lambda i: (i, 0, 0, 0))
    y, s1 = pl.pallas_call(
        functools.partial(_step_kernel, tb),
        grid=(n // tb,),
        in_specs=[vec() for _ in range(6)] + [st()],
        out_specs=[vec(), st()],
        out_shape=[jax.ShapeDtypeStruct((n, 1, D_MODEL), F32), jax.ShapeDtypeStruct(s0.shape, F32)],
        compiler_params=pltpu.CompilerParams(dimension_semantics=("parallel",)),
        name="scan_sample",
    )(*vecs, s0)
    return y.reshape(n, D_MODEL), s1


def _back_kernel(paired, y_ref, bonus_ref, g_ref, gateb_ref, outa_ref, x_ref, lxw_ref, lxb_ref, pb_ref,
                 wo_ref, nfw_ref, rw_ref, rb_ref, ones2_ref, x1_ref, tok_ref, logt_ref):
    if paired:
        y = jnp.concatenate([y_ref[p] for p in range(N_PAIRS)], axis=1)
    else:
        y = y_ref[...]
    ones2 = ones2_ref[...]
    inv_n = 1.0 / B_HEAD_DIM
    yc = y - _head_sums(y, ones2) * inv_n
    var = _head_sums(yc * yc, ones2) * inv_n
    yn = yc * lax.rsqrt(var + GN_EPS) * lxw_ref[...] + lxb_ref[...]
    yb = ((yn + bonus_ref[...]) * g_ref[...]).astype(BF16)
    merged = outa_ref[...] + gateb_ref[...] * _dot(yb, pb_ref[...])
    x1 = x_ref[...] + _dot(merged.astype(BF16), wo_ref[...])
    x1_ref[...] = x1
    tok = _rms_norm(x1, nfw_ref[...])
    tok_ref[...] = tok.astype(BF16)

    logt_ref[...] = lax.dot_general(rw_ref[...], tok, (((1,), (1,)), ((), ())), precision=HIGHEST,
                                    preferred_element_type=F32) + rb_ref[...]


def _back(paired, y, bonus, g, gateb, outa, x, w, tm):
    n = x.shape[0]
    consts = [w["lxw"], w["lxb"], w["pb"], w["wo"], w["nfw"], w["rwt"], w["rbt"], w["ones2"]]
    tok = lambda: pl.BlockSpec((tm, D_MODEL), lambda i: (i, 0))
    if paired:
        nt = y.shape[2] // tm
        y_spec = pl.BlockSpec((None, N_PAIRS, tm, 128), lambda i: (i // nt, 0, i % nt, 0))
    else:
        y_spec = tok()
    return pl.pallas_call(
        functools.partial(_back_kernel, paired),
        grid=(n // tm,),
        in_specs=[y_spec] + [tok() for _ in range(5)] + [_const_spec(c.shape, 1) for c in consts],
        out_specs=[tok(), tok(), pl.BlockSpec((ROUTER_ROWS, tm), lambda i: (0, i))],
        out_shape=[jax.ShapeDtypeStruct((n, D_MODEL), F32), jax.ShapeDtypeStruct((n, D_MODEL), BF16),
                   jax.ShapeDtypeStruct((ROUTER_ROWS, n), F32)],
        compiler_params=pltpu.CompilerParams(dimension_semantics=("parallel",), vmem_limit_bytes=VMEM_LIMIT),
        name="back_prompt" if paired else "back_sample",
    )(y, bonus, g, gateb, outa, x, *consts)


def _moe_slots(ts):
    n = 2 * ts + N_EXPERTS * (SEG_ALIGN - 1)
    return -(-n // MOE_WINDOW) * MOE_WINDOW


def _split3_f32(x):
    hi = x.astype(BF16).astype(F32)
    mid = (x - hi).astype(BF16).astype(F32)
    lo = (x - hi - mid).astype(BF16).astype(F32)
    return hi, mid, lo


def _moe_kernel(ts, ns, tok_ref, logt_ref, x1_ref, utri_ref, wg_ref, wu_ref, wd_ref, fw_ref, o_ref,
                xy_ref, pg_ref, ws_ref, meta_ref):
    n_slots = xy_ref.shape[1]
    step = pl.program_id(1)
    tn = (((1,), (1,)), ((), ()))

    @pl.when(step < ns)
    def _route_and_sort():
        lt = logt_ref[...]
        neg = jnp.float32(-jnp.inf)
        big = jnp.float32(99.0)
        row8 = lax.broadcasted_iota(jnp.int32, (8, ts), 0).astype(F32)
        is_grp = row8 < N_GROUPS
        lg = jnp.where(is_grp, lt[0:8, :], neg)
        gmax = jnp.max(lg, axis=0, keepdims=True)
        grp = jnp.min(jnp.where(lg == gmax, row8, big), axis=0, keepdims=True)
        p_grp = 1.0 / jnp.sum(jnp.where(is_grp, jnp.exp(lg - gmax), 0.0), axis=0, keepdims=True)
        le = lt[EXPERT_ROW0:EXPERT_ROW0 + EXPERTS_PER_GROUP, :]
        for g in range(1, N_GROUPS):
            lo = EXPERT_ROW0 + g * EXPERTS_PER_GROUP
            le = jnp.where(grp == g, lt[lo:lo + EXPERTS_PER_GROUP, :], le)
        top1 = jnp.max(le, axis=0, keepdims=True)
        i1 = jnp.min(jnp.where(le == top1, row8, big), axis=0, keepdims=True)
        le2 = jnp.where(row8 == i1, neg, le)
        top2 = jnp.max(le2, axis=0, keepdims=True)
        i2 = jnp.min(jnp.where(le2 == top2, row8, big), axis=0, keepdims=True)
        e2 = jnp.exp(top2 - top1)
        w1 = p_grp / (1.0 + e2)
        w2 = p_grp * e2 / (1.0 + e2)

        row_e = lax.broadcasted_iota(jnp.int32, (N_EXPERTS, ts), 0).astype(F32)
        a1 = row_e == grp * EXPERTS_PER_GROUP + i1
        a2 = row_e == grp * EXPERTS_PER_GROUP + i2
        at = jnp.where(a1, 1.0, 0.0) + jnp.where(a2, 1.0, 0.0)
        rank = _dot(at.astype(BF16), utri_ref[...])
        cnt = rank[:, ts - 1:ts] + at[:, ts - 1:ts]
        cntp = jnp.floor((cnt + (SEG_ALIGN - 1)) * (1.0 / SEG_ALIGN)) * SEG_ALIGN
        cntp_b = jnp.broadcast_to(cntp, (N_EXPERTS, 128))
        r_e = lax.broadcasted_iota(jnp.int32, (N_EXPERTS, N_EXPERTS), 0)
        c_e = lax.broadcasted_iota(jnp.int32, (N_EXPERTS, N_EXPERTS), 1)
        off = _dot(jnp.where(r_e > c_e, 1.0, 0.0).astype(BF16), cntp_b.astype(BF16))
        slot = off[:, 0:1] + rank
        s1 = jnp.sum(jnp.where(a1, slot, 0.0), axis=0, keepdims=True)
        s2 = jnp.sum(jnp.where(a2, slot, 0.0), axis=0, keepdims=True)
        srow = lax.broadcasted_iota(jnp.int32, (n_slots, ts), 0).astype(F32)
        pg1 = jnp.where(srow == s1, 1.0, 0.0)
        pg2 = jnp.where(srow == s2, 1.0, 0.0)
        pg = (pg1 + pg2).astype(BF16)
        pg_ref[step] = pg
        xy_ref[step] = _dot(pg, tok_ref[...]).astype(BF16)

        wrows = jnp.concatenate(
            [jnp.concatenate([p1, p2], axis=1) for p1, p2 in zip(_split3_f32(w1), _split3_f32(w2))]
            + [jnp.zeros((5, 2 * ts), F32)], axis=0).astype(BF16)
        wsl = lax.dot_general(jnp.concatenate([pg1.astype(BF16), pg2.astype(BF16)], axis=1), wrows, tn,
                              preferred_element_type=F32)
        ws_ref[step] = jnp.broadcast_to(wsl[:, 0:1] + wsl[:, 1:2] + wsl[:, 2:3], (n_slots, 128))
        meta_ref[step, 0:N_EXPERTS, :] = off.astype(jnp.int32)
        meta_ref[step, N_EXPERTS:2 * N_EXPERTS, :] = cntp_b.astype(jnp.int32)

    @pl.when((step >= ns) & (step < ns + N_EXPERTS))
    def _expert():
        e = step - ns
        offs = [meta_ref[j, pl.ds(e, 1), :][0, 0] for j in range(ns)]
        cnts = [meta_ref[j, pl.ds(N_EXPERTS + e, 1), :][0, 0] for j in range(ns)]
        n_win = functools.reduce(jnp.maximum, [(c + MOE_WINDOW - 1) // MOE_WINDOW for c in cnts])
        riota = lax.broadcasted_iota(jnp.int32, (MOE_WINDOW, 1), 0)

        def window(w, carry):
            starts, pieces = [], []
            for j in range(ns):
                st = jnp.minimum(offs[j] + w * MOE_WINDOW, n_slots - MOE_WINDOW)
                starts.append(pl.multiple_of(st, SEG_ALIGN))
                pieces.append(xy_ref[j, pl.ds(starts[j], MOE_WINDOW), :])
            lhs = jnp.concatenate(pieces, axis=0)
            gate = _dot(lhs, wg_ref[...])
            h = gate * _sigmoid(gate) * _dot(lhs, wu_ref[...])
            y = _dot(h.astype(BF16), wd_ref[...])
            for j in range(ns):
                rows = starts[j] + riota
                valid = (rows >= offs[j] + w * MOE_WINDOW) & (rows < offs[j] + cnts[j])
                yw = y[j * MOE_WINDOW:(j + 1) * MOE_WINDOW, :] * ws_ref[j, pl.ds(starts[j], MOE_WINDOW), 0:1]
                xy_ref[j, pl.ds(starts[j], MOE_WINDOW), :] = jnp.where(
                    valid, yw, pieces[j].astype(F32)).astype(BF16)
            return carry

        lax.fori_loop(0, n_win, window, 0)

    @pl.when(step >= ns + N_EXPERTS)
    def _combine():
        j = step - ns - N_EXPERTS
        c = lax.dot_general(pg_ref[j], xy_ref[j], (((0,), (0,)), ((), ())), preferred_element_type=F32)
        o_ref[...] = _rms_norm(x1_ref[...] + c, fw_ref[...])


def _moe(tok, logt, x1, w, ts, ns):
    n = tok.shape[0]
    n_slots = _moe_slots(ts)
    sub_in = lambda s, k: s * ns + jnp.clip(k, 0, ns - 1)
    sub_out = lambda s, k: s * ns + jnp.clip(k - ns - N_EXPERTS, 0, ns - 1)
    expert = lambda k: jnp.clip(k - ns, 0, N_EXPERTS - 1)
    utri = (jnp.arange(ts)[:, None] < jnp.arange(ts)[None, :]).astype(BF16)
    return pl.pallas_call(
        functools.partial(_moe_kernel, ts, ns),
        grid=(n // (ts * ns), 2 * ns + N_EXPERTS),
        in_specs=[pl.BlockSpec((ts, D_MODEL), lambda s, k: (sub_in(s, k), 0)),
                  pl.BlockSpec((ROUTER_ROWS, ts), lambda s, k: (0, sub_in(s, k))),
                  pl.BlockSpec((ts, D_MODEL), lambda s, k: (sub_out(s, k), 0)),
                  pl.BlockSpec((ts, ts), lambda s, k: (0, 0)),
                  pl.BlockSpec((None, D_MODEL, D_EXPERT), lambda s, k: (expert(k), 0, 0)),
                  pl.BlockSpec((None, D_MODEL, D_EXPERT), lambda s, k: (expert(k), 0, 0)),
                  pl.BlockSpec((None, D_EXPERT, D_MODEL), lambda s, k: (expert(k), 0, 0)),
                  pl.BlockSpec((1, D_MODEL), lambda s, k: (0, 0))],
        out_specs=pl.BlockSpec((ts, D_MODEL), lambda s, k: (sub_out(s, k), 0)),
        out_shape=jax.ShapeDtypeStruct((n, D_MODEL), F32),
        scratch_shapes=[pltpu.VMEM((ns, n_slots, D_MODEL), BF16), pltpu.VMEM((ns, n_slots, ts), BF16),
                        pltpu.VMEM((ns, n_slots, 128), F32), pltpu.VMEM((ns, 2 * N_EXPERTS, 128), jnp.int32)],
        compiler_params=pltpu.CompilerParams(dimension_semantics=("parallel", "arbitrary"),
                                             vmem_limit_bytes=VMEM_LIMIT),
        name="moe",
    )(tok, logt, x1, utri, w["wg"], w["wu"], w["wd"], w["fw"])


def _prep_weights(norm_mix_w, w_in, sgu_ln_w, sgu_ln_b, sgu_w_s, sgu_b, rwkv_mu, w_lora_up, w_bias,
                  a_lora_up, a_bias, g_lora_up, k_k, k_a, r_k, lnx_w, lnx_b, proj_a, proj_b, w_out,
                  norm_ffn_w, router_group, router_group_bias, router_expert, router_expert_bias,
                  moe_w_gate, moe_w_up, moe_w_down, norm_final_w):
    row = lambda z: z.reshape(1, -1).astype(F32)
    causal = jnp.tril(jnp.ones((CHUNK, CHUNK), dtype=bool))
    zeros_lora = jnp.zeros((64, D_MODEL), F32)
    head = jnp.arange(128) // B_HEAD_DIM
    ones_bd = (head[:, None] == head[None, :]).astype(BF16)
    pad = EXPERT_ROW0 - N_GROUPS
    return {
        "nw": row(norm_mix_w), "win": w_in.astype(BF16), "lnw": row(sgu_ln_w), "lnb": row(sgu_ln_b),
        "wsc": jnp.where(causal[None], sgu_w_s, 0.0).astype(BF16),
        "bsb": jnp.repeat(jnp.transpose(sgu_b), A_HEAD_DIM, axis=1),
        "ws0": jnp.repeat(sgu_w_s[:, 0, 0], A_HEAD_DIM).reshape(1, -1),
        "bs0": jnp.repeat(sgu_b[:, 0], A_HEAD_DIM).reshape(1, -1),
        "mu": row(rwkv_mu),
        "wl": jnp.concatenate([w_lora_up, zeros_lora], axis=0).astype(BF16), "wb": row(w_bias),
        "al": jnp.concatenate([zeros_lora, a_lora_up], axis=0).astype(BF16), "ab": row(a_bias),
        "gl": g_lora_up.astype(BF16), "kk": row(k_k), "ka": row(k_a), "rk": row(r_k),
        "pa": proj_a.astype(BF16), "ones2": jnp.concatenate([ones_bd, ones_bd], axis=0),
        "lxw": row(lnx_w), "lxb": row(lnx_b), "pb": proj_b.astype(BF16), "wo": w_out.astype(BF16),
        "nfw": row(norm_ffn_w),
        "rwt": jnp.transpose(jnp.concatenate([router_group, jnp.zeros((D_MODEL, pad), F32), router_expert], axis=1)),
        "rbt": jnp.concatenate([router_group_bias, jnp.zeros((pad,), F32), router_expert_bias]).reshape(-1, 1),
        "wg": moe_w_gate.astype(BF16), "wu": moe_w_up.astype(BF16), "wd": moe_w_down.astype(BF16),
        "fw": row(norm_final_w),
    }


def kernel(x_prompt, x_sample, state_wkv, state_shift, norm_mix_w, w_in, sgu_ln_w, sgu_ln_b, sgu_w_s, sgu_b, rwkv_mu, w_lora_up, w_bias, a_lora_up, a_bias, g_lora_up, k_k, k_a, r_k, lnx_w, lnx_b, proj_a, proj_b, w_out, norm_ffn_w, router_group, router_group_bias, router_expert, router_expert_bias, moe_w_gate, moe_w_up, moe_w_down, norm_final_w):
    layer = [z[0] for z in (norm_mix_w, w_in, sgu_ln_w, sgu_ln_b, sgu_w_s, sgu_b, rwkv_mu, w_lora_up, w_bias,
                            a_lora_up, a_bias, g_lora_up, k_k, k_a, r_k, lnx_w, lnx_b, proj_a, proj_b, w_out,
                            norm_ffn_w, router_group, router_group_bias, router_expert, router_expert_bias,
                            moe_w_gate, moe_w_up, moe_w_down)]
    w = _prep_weights(*layer, norm_final_w)
    bsz, t_len, _ = x_prompt.shape
    n_s = x_sample.shape[0]

    (outa, gateb, g, bonus, r, lw, k, v, kn, bs, last) = _front_prompt(x_prompt, w, 256)
    y, st = _scan_prompt(r, lw, k, v, kn, bs, 128, N_PAIRS)
    flat = lambda z: z.reshape(bsz * t_len, D_MODEL)
    x1, tok, logt = _back(True, y, flat(bonus), flat(g), flat(gateb), flat(outa), flat(x_prompt), w, 512)
    y_prompt = _moe(tok, logt, x1, w, 512, 4).reshape(bsz, t_len, D_MODEL)
    st = st.reshape(bsz, N_PAIRS, 2, B_HEAD_DIM, 2, B_HEAD_DIM)
    wkv_p = jnp.stack([st[:, :, 0, :, 0, :], st[:, :, 1, :, 1, :]], axis=2)
    wkv_p = jnp.swapaxes(wkv_p.reshape(bsz, B_HEADS, B_HEAD_DIM, B_HEAD_DIM), -1, -2)

    xs2 = x_sample.reshape(n_s, D_MODEL)
    (outa, gateb, g, bonus, r, lw, k, v, kn, bs, cols_s, vn_s) = _front_sample(xs2, state_shift[0], w)
    y, wkv_s = _scan_sample(r, lw, k, v, kn, bs, state_wkv[0], 4)
    x1, tok, logt = _back(False, y, bonus, g, gateb, outa, xs2, w, n_s)
    y_sample = _moe(tok, logt, x1, w, n_s, 1).reshape(n_s, 1, D_MODEL)

    return (y_prompt, y_sample, wkv_p[None], last.reshape(1, bsz, SHIFT_WIDTH), wkv_s[None], cols_s[None],
            vn_s.reshape(1, n_s, 1, A_HEADS, A_HEAD_DIM))
```

```python
import functools
import math

import jax
import jax.numpy as jnp
from jax import lax
from jax.experimental import pallas as pl
from jax.experimental.pallas import tpu as pltpu

F32 = jnp.float32
BF16 = jnp.bfloat16
HIGHEST = lax.Precision.HIGHEST

D_MODEL = 1024
CHUNK = 128
A_HEADS = 8
A_HEAD_DIM = 128
B_HEADS = 16
B_HEAD_DIM = 64
N_PAIRS = B_HEADS // 2
SHIFT_WIDTH = 3328
OFF_V = 1024
OFF_SHIFT = 2048
OFF_GATE_A = OFF_SHIFT + SHIFT_WIDTH
OFF_GATE_B = OFF_GATE_A + D_MODEL
IN_COLS = OFF_GATE_B + D_MODEL
N_GROUPS = 4
EXPERTS_PER_GROUP = 8
N_EXPERTS = 32
D_EXPERT = 256
ROUTER_ROWS = 40
EXPERT_ROW0 = 8
SEG_ALIGN = 16
MOE_WINDOW = 64
MOE_EXPERTS_PER_STEP = 4
RMS_EPS = 1e-6
LN_EPS = 1e-5
GN_EPS = 64e-5

SCAN_CHUNK = 64
FRONT_BLOCK = 256
VMEM_LIMIT = 56 * 1024 * 1024


def _gelu(x):
    return x * (0.5 * (1.0 + jnp.tanh(math.sqrt(2.0 / math.pi) * (x + 0.044715 * (x * x * x)))))


def _sigmoid(x):
    return 1.0 / (1.0 + jnp.exp(-x))


def _softplus(z):
    return jnp.maximum(z, 0.0) + jnp.log(1.0 + jnp.exp(-jnp.abs(z)))


def _rms_norm(x, g):
    return x * lax.rsqrt(jnp.mean(x * x, axis=-1, keepdims=True) + RMS_EPS) * g


def _dot(a, b):
    return jnp.dot(a, b, preferred_element_type=F32)


def _dot_hi(a, b):
    return jnp.dot(a, b, preferred_element_type=F32, precision=HIGHEST)


def _head_sums(z, ones2):
    outs = []
    for p in range(z.shape[1] // 128):
        zp = z[:, p * 128:(p + 1) * 128]
        hi = zp.astype(BF16)
        lo = (zp - hi.astype(F32)).astype(BF16)
        outs.append(_dot(jnp.concatenate([hi, lo], axis=1), ones2))
    return jnp.concatenate(outs, axis=1)


def _front_kernel(is_sample, tm, *refs):
    if is_sample:
        (x_ref, prev_ref, nw_ref, win_ref, lnw_ref, lnb_ref, ws0_ref, bs0_ref, mu_ref, wl_ref, wb_ref,
         al_ref, ab_ref, gl_ref, kk_ref, ka_ref, rk_ref, pa_ref, ones2_ref,
         outa_ref, gateb_ref, g_ref, bonus_ref, r_ref, lw_ref, k_ref, v_ref, kn_ref, bs_ref,
         cols_ref, vn_ref, ya_ref) = refs
    else:
        (x_ref, nw_ref, win_ref, lnw_ref, lnb_ref, wsc_ref, bsb_ref, mu_ref, wl_ref, wb_ref,
         al_ref, ab_ref, gl_ref, kk_ref, ka_ref, rk_ref, pa_ref, ones2_ref,
         outa_ref, gateb_ref, g_ref, bonus_ref, r_ref, lw_ref, k_ref, v_ref, kn_ref, bs_ref,
         last_ref, ya_ref, carry_ref) = refs

        @pl.when(pl.program_id(1) == 0)
        def _():
            carry_ref[...] = jnp.zeros_like(carry_ref)

    xb = _rms_norm(x_ref[...], nw_ref[...]).astype(BF16)
    ones2 = ones2_ref[...]

    def proj(lo, width):
        return _dot(xb, win_ref[:, lo:lo + width])

    def shifted(lo, width):
        cs = slice(lo, lo + width)
        cols = proj(OFF_SHIFT + lo, width)
        if is_sample:
            prev = prev_ref[:, cs]
            cols_ref[:, cs] = cols
        else:
            row = lax.broadcasted_iota(jnp.int32, cols.shape, 0)
            prev = jnp.where(row == 0, carry_ref[0:1, cs], pltpu.roll(cols, 1, 0))
            carry_ref[0:1, cs] = cols[tm - 1:tm, :]
            last_ref[:, cs] = cols[tm - 1:tm, :]
        return cols + (prev - cols) * mu_ref[:, cs]

    tail = shifted(3 * D_MODEL, 256)
    wa = tail[:, 0:128]
    twa = jnp.tanh(wa).astype(BF16)
    wab = wa.astype(BF16)
    sgd = _sigmoid(tail[:, 128:256]).astype(BF16)

    for j in range(D_MODEL // FRONT_BLOCK):
        lo = j * FRONT_BLOCK
        cs = slice(lo, lo + FRONT_BLOCK)

        r = shifted(lo, FRONT_BLOCK)
        k = shifted(D_MODEL + lo, FRONT_BLOCK)
        vb = shifted(2 * D_MODEL + lo, FRONT_BLOCK)
        w_log = -_softplus(-(wb_ref[:, cs] + _dot(twa, wl_ref[:, cs]))) - 0.5
        logw = -jnp.exp(w_log)
        a = _sigmoid(ab_ref[:, cs] + _dot(wab, al_ref[:, cs]))
        g_ref[:, cs] = _dot(sgd, gl_ref[:, cs])
        kk = k * kk_ref[:, cs]
        kkn = kk / jnp.maximum(jnp.sqrt(_head_sums(kk * kk, ones2)), 1e-12)
        k2 = k * (1.0 + (a - 1.0) * ka_ref[:, cs])
        bonus_ref[:, cs] = _head_sums(r * k2 * rk_ref[:, cs], ones2) * vb
        bsc = kkn * a
        outs = ((r_ref, r), (lw_ref, logw), (k_ref, k2), (v_ref, vb), (kn_ref, kkn), (bs_ref, bsc))
        for o_ref, val in outs:
            if is_sample:
                o_ref[:, cs] = val
            else:
                for q in range(FRONT_BLOCK // 128):
                    o_ref[lo // 128 + q] = val[:, q * 128:(q + 1) * 128]

        u2 = _gelu(proj(lo, FRONT_BLOCK))
        v2 = _gelu(proj(OFF_V + lo, FRONT_BLOCK))
        for q in range(FRONT_BLOCK // A_HEAD_DIM):
            h = lo // A_HEAD_DIM + q
            hs = slice(h * A_HEAD_DIM, (h + 1) * A_HEAD_DIM)
            u = u2[:, q * A_HEAD_DIM:(q + 1) * A_HEAD_DIM]
            v = v2[:, q * A_HEAD_DIM:(q + 1) * A_HEAD_DIM]
            mean = jnp.mean(v, axis=-1, keepdims=True)
            vc = v - mean
            var = jnp.mean(vc * vc, axis=-1, keepdims=True)
            vn = vc * lax.rsqrt(var + LN_EPS) * lnw_ref[:, hs] + lnb_ref[:, hs]
            if is_sample:
                vn_ref[:, hs] = vn
                ya_ref[:, hs] = (u * (vn * ws0_ref[:, hs] + bs0_ref[:, hs])).astype(BF16)
            else:
                vnb = vn.astype(BF16)
                for c in range(tm // CHUNK):
                    rs = slice(c * CHUNK, (c + 1) * CHUNK)
                    s = _dot(wsc_ref[h], vnb[rs, :]) + bsb_ref[:, hs]
                    ya_ref[rs, hs] = (u[rs, :] * s).astype(BF16)

        gateb_ref[:, cs] = _sigmoid(proj(OFF_GATE_B + lo, FRONT_BLOCK))

    outa_ref[...] = _sigmoid(proj(OFF_GATE_A, D_MODEL)) * _dot(ya_ref[...], pa_ref[...])


def _const_spec(shape, grid_rank):
    zeros = (0,) * len(shape)
    if grid_rank == 1:
        return pl.BlockSpec(shape, lambda i: zeros, pipeline_mode=pl.Buffered(1))
    return pl.BlockSpec(shape, lambda b, i: zeros, pipeline_mode=pl.Buffered(1))


def _front_prompt(x, w, tm):
    bsz, t_len, _ = x.shape
    nt = t_len // tm
    consts = [w["nw"], w["win"], w["lnw"], w["lnb"], w["wsc"], w["bsb"], w["mu"], w["wl"], w["wb"],
              w["al"], w["ab"], w["gl"], w["kk"], w["ka"], w["rk"], w["pa"], w["ones2"]]
    tok = lambda: pl.BlockSpec((None, tm, D_MODEL), lambda b, i: (b, i, 0))
    pair = lambda: pl.BlockSpec((None, N_PAIRS, tm, 128), lambda b, i: (b, 0, i, 0))
    tok_shape = jax.ShapeDtypeStruct((bsz, t_len, D_MODEL), F32)
    pair_shape = jax.ShapeDtypeStruct((bsz, N_PAIRS, t_len, 128), F32)
    return pl.pallas_call(
        functools.partial(_front_kernel, False, tm),
        grid=(bsz, nt),
        in_specs=[tok()] + [_const_spec(c.shape, 2) for c in consts],
        out_specs=[tok(), tok(), tok(), tok()] + [pair() for _ in range(6)]
        + [pl.BlockSpec((None, 1, SHIFT_WIDTH), lambda b, i: (b, 0, 0))],
        out_shape=[tok_shape] * 4 + [pair_shape] * 6 + [jax.ShapeDtypeStruct((bsz, 1, SHIFT_WIDTH), F32)],
        scratch_shapes=[pltpu.VMEM((tm, D_MODEL), BF16), pltpu.VMEM((8, SHIFT_WIDTH), F32)],
        compiler_params=pltpu.CompilerParams(dimension_semantics=("parallel", "arbitrary"),
                                             vmem_limit_bytes=VMEM_LIMIT),
        name="front_prompt",
    )(x, *consts)


def _front_sample(x, prev, w):
    n = x.shape[0]
    consts = [w["nw"], w["win"], w["lnw"], w["lnb"], w["ws0"], w["bs0"], w["mu"], w["wl"], w["wb"],
              w["al"], w["ab"], w["gl"], w["kk"], w["ka"], w["rk"], w["pa"], w["ones2"]]
    tok = lambda: pl.BlockSpec((n, D_MODEL), lambda i: (0, 0))
    wide = lambda: pl.BlockSpec((n, SHIFT_WIDTH), lambda i: (0, 0))
    tok_shape = jax.ShapeDtypeStruct((n, D_MODEL), F32)
    return pl.pallas_call(
        functools.partial(_front_kernel, True, n),
        grid=(1,),
        in_specs=[tok(), wide()] + [_const_spec(c.shape, 1) for c in consts],
        out_specs=[tok() for _ in range(10)] + [wide(), tok()],
        out_shape=[tok_shape] * 10 + [jax.ShapeDtypeStruct((n, SHIFT_WIDTH), F32), tok_shape],
        scratch_shapes=[pltpu.VMEM((n, D_MODEL), BF16)],
        compiler_params=pltpu.CompilerParams(dimension_semantics=("arbitrary",),
                                             vmem_limit_bytes=VMEM_LIMIT),
        name="front_sample",
    )(x, prev, *consts)


def _split3(x):
    hi = x.astype(BF16)
    r1 = x - hi.astype(F32)
    mid = r1.astype(BF16)
    lo = (r1 - mid.astype(F32)).astype(BF16)
    return hi, mid, lo


def _scan_kernel(tt, npp, r_ref, lw_ref, k_ref, v_ref, kn_ref, bs_ref, y_ref, sout_ref, st_ref):
    c_len = SCAN_CHUNK
    i = pl.program_id(2)

    @pl.when(i == 0)
    def _():
        st_ref[...] = jnp.zeros_like(st_ref)

    lane = lax.broadcasted_iota(jnp.int32, (c_len, 128), 1)
    head0 = lane < B_HEAD_DIM
    row2 = lax.broadcasted_iota(jnp.int32, (128, 128), 0)
    col2 = lax.broadcasted_iota(jnp.int32, (128, 128), 1)
    tpos = jnp.bitwise_and(row2, c_len - 1)
    spos = jnp.bitwise_and(col2, c_len - 1)
    strict = tpos > spos
    incl = tpos >= spos
    eye = row2 == col2
    rc = lax.broadcasted_iota(jnp.int32, (c_len, 3 * c_len), 0)
    cc = jnp.bitwise_and(lax.broadcasted_iota(jnp.int32, (c_len, 3 * c_len), 1), c_len - 1)
    tri3 = jnp.where(rc >= cc, 1.0, 0.0).astype(BF16)

    def pair_diag(z):
        return jnp.concatenate([jnp.where(head0, z, 0.0), jnp.where(head0, 0.0, z)], axis=0).astype(BF16)

    def bdot(a, b):
        return _dot(a.astype(BF16), b.astype(BF16))

    n_chunks = tt // c_len
    insts = [(c, q) for c in range(n_chunks) for q in range(npp)]
    rows = lambda c: slice(c * c_len, (c + 1) * c_len)

    log_p = {}
    for c, q in insts:
        log_p[c, q] = _dot(tri3, jnp.concatenate(_split3(lw_ref[q, rows(c), :]), axis=0))

    ops, gram = {}, {}
    for c, q in insts:
        lp = log_p[c, q]
        kn = kn_ref[q, rows(c), :]
        bs = bs_ref[q, rows(c), :]
        kx = k_ref[q, rows(c), :]
        log_pc = lp[c_len - 1:c_len, :]
        e_inv = jnp.exp(-lp)
        e_dec = jnp.exp(log_pc - lp)
        a_m = pair_diag(-kn * jnp.exp(lp - lw_ref[q, rows(c), :]))
        r_m = pair_diag(r_ref[q, rows(c), :] * jnp.exp(lp))
        v_m = pair_diag(v_ref[q, rows(c), :])
        btkt = jnp.concatenate([pair_diag(bs * e_dec), pair_diag(kx * e_dec)], axis=0)
        pc_col = jnp.sum(jnp.where(eye, jnp.exp(log_pc), 0.0), axis=1, keepdims=True)
        ops[c, q] = (a_m, r_m, v_m, btkt, pc_col)
        gram[c, q] = lax.dot_general(jnp.concatenate([a_m, r_m], axis=0),
                                     jnp.concatenate([pair_diag(bs * e_inv), pair_diag(kx * e_inv)], axis=0),
                                     (((1,), (1,)), ((), ())), preferred_element_type=F32)

    n_sum, pw, l_r, lakv = {}, {}, {}, {}
    for c, q in insts:
        g = gram[c, q]
        l_ab = jnp.where(strict, g[0:128, 0:128], 0.0)
        n_sum[c, q] = l_ab
        pw[c, q] = l_ab.astype(BF16)
        l_r[c, q] = jnp.concatenate([jnp.where(incl, g[128:256, 0:128], 0.0),
                                     jnp.where(incl, g[128:256, 128:256], 0.0)], axis=1).astype(BF16)
        lakv[c, q] = _dot(jnp.where(strict, g[0:128, 128:256], 0.0).astype(BF16), ops[c, q][2])

    pw_f = {}
    for c, q in insts:
        pw_f[c, q] = _dot(pw[c, q], pw[c, q])
        pw[c, q] = pw_f[c, q].astype(BF16)
    for _ in range(int(math.log2(c_len)) - 2):
        prod = {}
        for c, q in insts:
            prod[c, q] = _dot(pw[c, q], jnp.concatenate([pw[c, q], n_sum[c, q].astype(BF16)], axis=1))
        for c, q in insts:
            n_sum[c, q] = n_sum[c, q] + pw_f[c, q] + prod[c, q][:, 128:256]
            pw_f[c, q] = prod[c, q][:, 0:128]
            pw[c, q] = pw_f[c, q].astype(BF16)
    for c, q in insts:
        n_sum[c, q] = n_sum[c, q] + pw_f[c, q] + bdot(pw[c, q], n_sum[c, q])

    wu = {}
    for c, q in insts:
        rhs = jnp.concatenate([ops[c, q][0].astype(F32), lakv[c, q]], axis=1)
        wu[c, q] = rhs + bdot(n_sum[c, q], rhs)

    for c in range(n_chunks):
        xs, st = {}, {}
        for q in range(npp):
            st[q] = st_ref[q]
            xs[q] = bdot(jnp.concatenate([wu[c, q][:, 0:128].astype(BF16), ops[c, q][1]], axis=0), st[q])
        for q in range(npp):
            _, _, v_m, btkt, pc_col = ops[c, q]
            uv = jnp.concatenate([(xs[q][0:128, :] + wu[c, q][:, 128:256]).astype(BF16), v_m], axis=0)
            y_m = xs[q][128:256, :] + _dot(l_r[c, q], uv)
            y_ref[q, rows(c), :] = y_m[0:c_len, :] + y_m[c_len:128, :]
            st_ref[q] = pc_col * st[q] + lax.dot_general(btkt, uv, (((0,), (0,)), ((), ())),
                                                         preferred_element_type=F32)

    @pl.when(i == pl.num_programs(2) - 1)
    def _():
        sout_ref[...] = st_ref[...]


def _scan_prompt(r, lw, k, v, kn, bs, tt, npp):
    bsz, n_pairs, t_len, _ = r.shape
    blk = lambda: pl.BlockSpec((None, npp, tt, 128), lambda b, p, i: (b, p, i, 0))
    return pl.pallas_call(
        functools.partial(_scan_kernel, tt, npp),
        grid=(bsz, n_pairs // npp, t_len // tt),
        in_specs=[blk() for _ in range(6)],
        out_specs=[blk(), pl.BlockSpec((None, npp, 128, 128), lambda b, p, i: (b, p, 0, 0))],
        out_shape=[jax.ShapeDtypeStruct((bsz, n_pairs, t_len, 128), F32),
                   jax.ShapeDtypeStruct((bsz, n_pairs, 128, 128), F32)],
        scratch_shapes=[pltpu.VMEM((npp, 128, 128), F32)],
        compiler_params=pltpu.CompilerParams(dimension_semantics=("parallel", "parallel", "arbitrary"),
                                             vmem_limit_bytes=VMEM_LIMIT),
        name="scan_prompt",
    )(r, lw, k, v, kn, bs)


def _step_kernel(tb, r_ref, lw_ref, k_ref, v_ref, kn_ref, bs_ref, s_ref, y_ref, sout_ref):
    n = B_HEAD_DIM
    row = lax.broadcasted_iota(jnp.int32, (128, 128), 0)
    col = lax.broadcasted_iota(jnp.int32, (128, 128), 1)
    same_head = (row < n) == (col < n)
    eye = jnp.where(row == col, 1.0, 0.0).astype(BF16)
    zero_f = jnp.zeros((n, n), F32)
    zero_b = jnp.zeros((128, 128), BF16)
    tn = (((1,), (1,)), ((), ()))
    insts = [(t, p) for t in range(tb) for p in range(N_PAIRS)]
    ps = lambda p: slice(p * 128, (p + 1) * 128)
    rep = lambda z: jnp.broadcast_to(z, (128, 128))

    s0, sav = {}, {}
    for t, p in insts:
        s0[t, p] = jnp.concatenate([jnp.concatenate([s_ref[t, 2 * p], zero_f], axis=1),
                                    jnp.concatenate([zero_f, s_ref[t, 2 * p + 1]], axis=1)], axis=0)
        v_row = v_ref[t, :, ps(p)]
        v_hi = v_row.astype(BF16)
        v_lo = (v_row - v_hi.astype(F32)).astype(BF16)
        lhs = jnp.concatenate([s0[t, p].astype(BF16), eye, eye], axis=1)
        rhs = jnp.concatenate(
            [jnp.concatenate([rep((-kn_ref[t, :, ps(p)]).astype(BF16)), zero_b, zero_b], axis=1),
             jnp.concatenate([zero_b, rep(v_hi), rep(v_lo)], axis=1)], axis=0)
        sav[t, p] = lax.dot_general(lhs, rhs, tn, preferred_element_type=F32)
    s1 = {}
    for t, p in insts:
        upd = (s0[t, p] * jnp.exp(lw_ref[t, :, ps(p)]) + sav[t, p][:, 0:128] * bs_ref[t, :, ps(p)]
               + sav[t, p][:, 128:256] * k_ref[t, :, ps(p)])
        upd = jnp.where(same_head, upd, 0.0)
        sout_ref[t, 2 * p] = upd[0:n, 0:n]
        sout_ref[t, 2 * p + 1] = upd[n:128, n:128]
        s1[t, p] = upd.astype(BF16)
    for t, p in insts:
        r8 = jnp.broadcast_to(r_ref[t, :, ps(p)].astype(BF16), (8, 128))
        y_ref[t, :, ps(p)] = lax.dot_general(r8, s1[t, p], tn, preferred_element_type=F32)[0:1, :]


def _scan_sample(r, lw, k, v, kn, bs, s0, tb):
    n = r.shape[0]
    vecs = [z.reshape(n, 1, D_MODEL) for z in (r, lw, k, v, kn, bs)]
    vec = lambda: pl.BlockSpec((tb, 1, D_MODEL), lambda i: (i, 0, 0))
    st = lambda: pl.BlockSpec((None, tb, B_HEADS, B_HEAD_DIM, B_HEAD_DIM), lambda i: (0, i, 0, 0, 0))
    y, s1 = pl.pallas_call(
        functools.partial(_step_kernel, tb),
        grid=(n // tb,),
        in_specs=[vec() for _ in range(6)] + [st()],
        out_specs=[vec(), st()],
        out_shape=[jax.ShapeDtypeStruct((n, 1, D_MODEL), F32), jax.ShapeDtypeStruct(s0.shape, F32)],
        compiler_params=pltpu.CompilerParams(dimension_semantics=("parallel",)),
        name="scan_sample",
    )(*vecs, s0)
    return y.reshape(n, D_MODEL), s1


def _back_kernel(paired, y_ref, bonus_ref, g_ref, gateb_ref, outa_ref, x_ref, lxw_ref, lxb_ref, pb_ref,
                 wo_ref, nfw_ref, rw_ref, rb_ref, ones2_ref, x1_ref, tok_ref, logt_ref):
    if paired:
        y = jnp.concatenate([y_ref[p] for p in range(N_PAIRS)], axis=1)
    else:
        y = y_ref[...]
    ones2 = ones2_ref[...]
    inv_n = 1.0 / B_HEAD_DIM
    yc = y - _head_sums(y, ones2) * inv_n
    var = _head_sums(yc * yc, ones2) * inv_n
    yn = yc * lax.rsqrt(var + GN_EPS) * lxw_ref[...] + lxb_ref[...]
    yb = ((yn + bonus_ref[...]) * g_ref[...]).astype(BF16)
    merged = outa_ref[...] + gateb_ref[...] * _dot(yb, pb_ref[...])
    x1 = x_ref[...] + _dot(merged.astype(BF16), wo_ref[...])
    x1_ref[...] = x1
    tok = _rms_norm(x1, nfw_ref[...])
    tok_ref[...] = tok.astype(BF16)

    logt_ref[...] = lax.dot_general(rw_ref[...], tok, (((1,), (1,)), ((), ())), precision=HIGHEST,
                                    preferred_element_type=F32) + rb_ref[...]


def _back(paired, y, bonus, g, gateb, outa, x, w, tm):
    n = x.shape[0]
    consts = [w["lxw"], w["lxb"], w["pb"], w["wo"], w["nfw"], w["rwt"], w["rbt"], w["ones2"]]
    tok = lambda: pl.BlockSpec((tm, D_MODEL), lambda i: (i, 0))
    if paired:
        nt = y.shape[2] // tm
        y_spec = pl.BlockSpec((None, N_PAIRS, tm, 128), lambda i: (i // nt, 0, i % nt, 0))
    else:
        y_spec = tok()
    return pl.pallas_call(
        functools.partial(_back_kernel, paired),
        grid=(n // tm,),
        in_specs=[y_spec] + [tok() for _ in range(5)] + [_const_spec(c.shape, 1) for c in consts],
        out_specs=[tok(), tok(), pl.BlockSpec((ROUTER_ROWS, tm), lambda i: (0, i))],
        out_shape=[jax.ShapeDtypeStruct((n, D_MODEL), F32), jax.ShapeDtypeStruct((n, D_MODEL), BF16),
                   jax.ShapeDtypeStruct((ROUTER_ROWS, n), F32)],
        compiler_params=pltpu.CompilerParams(dimension_semantics=("parallel",), vmem_limit_bytes=VMEM_LIMIT),
        name="back_prompt" if paired else "back_sample",
    )(y, bonus, g, gateb, outa, x, *consts)


def _moe_slots(ts):
    n = 2 * ts + N_EXPERTS * (SEG_ALIGN - 1)
    return -(-n // MOE_WINDOW) * MOE_WINDOW


def _split3_f32(x):
    hi = x.astype(BF16).astype(F32)
    mid = (x - hi).astype(BF16).astype(F32)
    lo = (x - hi - mid).astype(BF16).astype(F32)
    return hi, mid, lo


def _moe_kernel(ts, ns, tok_ref, logt_ref, x1_ref, utri_ref, wg_ref, wu_ref, wd_ref, fw_ref, o_ref,
                xy_ref, pg_ref, ws_ref, meta_ref):
    n_slots = xy_ref.shape[1]
    n_esteps = N_EXPERTS // MOE_EXPERTS_PER_STEP
    step = pl.program_id(1)
    tn = (((1,), (1,)), ((), ()))

    @pl.when(step < ns)
    def _route_and_sort():
        lt = logt_ref[...]
        neg = jnp.float32(-jnp.inf)
        big = jnp.float32(99.0)
        row8 = lax.broadcasted_iota(jnp.int32, (8, ts), 0).astype(F32)
        is_grp = row8 < N_GROUPS
        lg = jnp.where(is_grp, lt[0:8, :], neg)
        gmax = jnp.max(lg, axis=0, keepdims=True)
        grp = jnp.min(jnp.where(lg == gmax, row8, big), axis=0, keepdims=True)
        p_grp = 1.0 / jnp.sum(jnp.where(is_grp, jnp.exp(lg - gmax), 0.0), axis=0, keepdims=True)
        le = lt[EXPERT_ROW0:EXPERT_ROW0 + EXPERTS_PER_GROUP, :]
        for g in range(1, N_GROUPS):
            lo = EXPERT_ROW0 + g * EXPERTS_PER_GROUP
            le = jnp.where(grp == g, lt[lo:lo + EXPERTS_PER_GROUP, :], le)
        top1 = jnp.max(le, axis=0, keepdims=True)
        i1 = jnp.min(jnp.where(le == top1, row8, big), axis=0, keepdims=True)
        le2 = jnp.where(row8 == i1, neg, le)
        top2 = jnp.max(le2, axis=0, keepdims=True)
        i2 = jnp.min(jnp.where(le2 == top2, row8, big), axis=0, keepdims=True)
        e2 = jnp.exp(top2 - top1)
        w1 = p_grp / (1.0 + e2)
        w2 = p_grp * e2 / (1.0 + e2)

        row_e = lax.broadcasted_iota(jnp.int32, (N_EXPERTS, ts), 0).astype(F32)
        a1 = row_e == grp * EXPERTS_PER_GROUP + i1
        a2 = row_e == grp * EXPERTS_PER_GROUP + i2
        at = jnp.where(a1, 1.0, 0.0) + jnp.where(a2, 1.0, 0.0)
        rank = _dot(at.astype(BF16), utri_ref[...])
        cnt = rank[:, ts - 1:ts] + at[:, ts - 1:ts]
        cntp = jnp.floor((cnt + (SEG_ALIGN - 1)) * (1.0 / SEG_ALIGN)) * SEG_ALIGN
        cntp_b = jnp.broadcast_to(cntp, (N_EXPERTS, 128))
        r_e = lax.broadcasted_iota(jnp.int32, (N_EXPERTS, N_EXPERTS), 0)
        c_e = lax.broadcasted_iota(jnp.int32, (N_EXPERTS, N_EXPERTS), 1)
        off = _dot(jnp.where(r_e > c_e, 1.0, 0.0).astype(BF16), cntp_b.astype(BF16))
        slot = off[:, 0:1] + rank
        s1 = jnp.sum(jnp.where(a1, slot, 0.0), axis=0, keepdims=True)
        s2 = jnp.sum(jnp.where(a2, slot, 0.0), axis=0, keepdims=True)
        srow = lax.broadcasted_iota(jnp.int32, (n_slots, ts), 0).astype(F32)
        pg1 = jnp.where(srow == s1, 1.0, 0.0)
        pg2 = jnp.where(srow == s2, 1.0, 0.0)
        pg = (pg1 + pg2).astype(BF16)
        pg_ref[step] = pg
        xy_ref[step] = _dot(pg, tok_ref[...]).astype(BF16)

        wrows = jnp.concatenate(
            [jnp.concatenate([p1, p2], axis=1) for p1, p2 in zip(_split3_f32(w1), _split3_f32(w2))]
            + [jnp.zeros((5, 2 * ts), F32)], axis=0).astype(BF16)
        wsl = lax.dot_general(jnp.concatenate([pg1.astype(BF16), pg2.astype(BF16)], axis=1), wrows, tn,
                              preferred_element_type=F32)
        ws_ref[step] = jnp.broadcast_to(wsl[:, 0:1] + wsl[:, 1:2] + wsl[:, 2:3], (n_slots, 128))
        meta_ref[step, 0:N_EXPERTS, :] = off.astype(jnp.int32)
        meta_ref[step, N_EXPERTS:2 * N_EXPERTS, :] = cntp_b.astype(jnp.int32)

    @pl.when((step >= ns) & (step < ns + n_esteps))
    def _experts():
        riota = lax.broadcasted_iota(jnp.int32, (MOE_WINDOW, 1), 0)
        for el in range(MOE_EXPERTS_PER_STEP):
            e = (step - ns) * MOE_EXPERTS_PER_STEP + el
            offs = [meta_ref[j, pl.ds(e, 1), :][0, 0] for j in range(ns)]
            cnts = [meta_ref[j, pl.ds(N_EXPERTS + e, 1), :][0, 0] for j in range(ns)]
            n_win = functools.reduce(jnp.maximum, [(c + MOE_WINDOW - 1) // MOE_WINDOW for c in cnts])

            def window(w, carry, el=el, offs=offs, cnts=cnts):
                starts, pieces = [], []
                for j in range(ns):
                    st = jnp.minimum(offs[j] + w * MOE_WINDOW, n_slots - MOE_WINDOW)
                    starts.append(pl.multiple_of(st, SEG_ALIGN))
                    pieces.append(xy_ref[j, pl.ds(starts[j], MOE_WINDOW), :])
                lhs = jnp.concatenate(pieces, axis=0)
                gate = _dot(lhs, wg_ref[el])
                h = gate * _sigmoid(gate) * _dot(lhs, wu_ref[el])
                y = _dot(h.astype(BF16), wd_ref[el])
                for j in range(ns):
                    rows = starts[j] + riota
                    valid = (rows >= offs[j] + w * MOE_WINDOW) & (rows < offs[j] + cnts[j])
                    yw = (y[j * MOE_WINDOW:(j + 1) * MOE_WINDOW, :]
                          * ws_ref[j, pl.ds(starts[j], MOE_WINDOW), 0:1])
                    xy_ref[j, pl.ds(starts[j], MOE_WINDOW), :] = jnp.where(
                        valid, yw, pieces[j].astype(F32)).astype(BF16)
                return carry

            lax.fori_loop(0, n_win, window, 0)

    @pl.when(step >= ns + n_esteps)
    def _combine():
        j = step - ns - n_esteps
        c = lax.dot_general(pg_ref[j], xy_ref[j], (((0,), (0,)), ((), ())), preferred_element_type=F32)
        o_ref[...] = _rms_norm(x1_ref[...] + c, fw_ref[...])


def _moe(tok, logt, x1, w, ts, ns):
    n = tok.shape[0]
    n_slots = _moe_slots(ts)
    n_esteps = N_EXPERTS // MOE_EXPERTS_PER_STEP
    ne = MOE_EXPERTS_PER_STEP
    sub_in = lambda s, k: s * ns + jnp.clip(k, 0, ns - 1)
    sub_out = lambda s, k: s * ns + jnp.clip(k - ns - n_esteps, 0, ns - 1)
    expert = lambda k: jnp.clip(k - ns, 0, n_esteps - 1)
    utri = (jnp.arange(ts)[:, None] < jnp.arange(ts)[None, :]).astype(BF16)
    return pl.pallas_call(
        functools.partial(_moe_kernel, ts, ns),
        grid=(n // (ts * ns), 2 * ns + n_esteps),
        in_specs=[pl.BlockSpec((ts, D_MODEL), lambda s, k: (sub_in(s, k), 0)),
                  pl.BlockSpec((ROUTER_ROWS, ts), lambda s, k: (0, sub_in(s, k))),
                  pl.BlockSpec((ts, D_MODEL), lambda s, k: (sub_out(s, k), 0)),
                  pl.BlockSpec((ts, ts), lambda s, k: (0, 0)),
                  pl.BlockSpec((ne, D_MODEL, D_EXPERT), lambda s, k: (expert(k), 0, 0)),
                  pl.BlockSpec((ne, D_MODEL, D_EXPERT), lambda s, k: (expert(k), 0, 0)),
                  pl.BlockSpec((ne, D_EXPERT, D_MODEL), lambda s, k: (expert(k), 0, 0)),
                  pl.BlockSpec((1, D_MODEL), lambda s, k: (0, 0))],
        out_specs=pl.BlockSpec((ts, D_MODEL), lambda s, k: (sub_out(s, k), 0)),
        out_shape=jax.ShapeDtypeStruct((n, D_MODEL), F32),
        scratch_shapes=[pltpu.VMEM((ns, n_slots, D_MODEL), BF16), pltpu.VMEM((ns, n_slots, ts), BF16),
                        pltpu.VMEM((ns, n_slots, 128), F32), pltpu.VMEM((ns, 2 * N_EXPERTS, 128), jnp.int32)],
        compiler_params=pltpu.CompilerParams(dimension_semantics=("parallel", "arbitrary"),
                                             vmem_limit_bytes=VMEM_LIMIT),
        name="moe",
    )(tok, logt, x1, utri, w["wg"], w["wu"], w["wd"], w["fw"])


def _prep_weights(norm_mix_w, w_in, sgu_ln_w, sgu_ln_b, sgu_w_s, sgu_b, rwkv_mu, w_lora_up, w_bias,
                  a_lora_up, a_bias, g_lora_up, k_k, k_a, r_k, lnx_w, lnx_b, proj_a, proj_b, w_out,
                  norm_ffn_w, router_group, router_group_bias, router_expert, router_expert_bias,
                  moe_w_gate, moe_w_up, moe_w_down, norm_final_w):
    row = lambda z: z.reshape(1, -1).astype(F32)
    causal = jnp.tril(jnp.ones((CHUNK, CHUNK), dtype=bool))
    zeros_lora = jnp.zeros((64, D_MODEL), F32)
    head = jnp.arange(128) // B_HEAD_DIM
    ones_bd = (head[:, None] == head[None, :]).astype(BF16)
    pad = EXPERT_ROW0 - N_GROUPS
    return {
        "nw": row(norm_mix_w), "win": w_in.astype(BF16), "lnw": row(sgu_ln_w), "lnb": row(sgu_ln_b),
        "wsc": jnp.where(causal[None], sgu_w_s, 0.0).astype(BF16),
        "bsb": jnp.repeat(jnp.transpose(sgu_b), A_HEAD_DIM, axis=1),
        "ws0": jnp.repeat(sgu_w_s[:, 0, 0], A_HEAD_DIM).reshape(1, -1),
        "bs0": jnp.repeat(sgu_b[:, 0], A_HEAD_DIM).reshape(1, -1),
        "mu": row(rwkv_mu),
        "wl": jnp.concatenate([w_lora_up, zeros_lora], axis=0).astype(BF16), "wb": row(w_bias),
        "al": jnp.concatenate([zeros_lora, a_lora_up], axis=0).astype(BF16), "ab": row(a_bias),
        "gl": g_lora_up.astype(BF16), "kk": row(k_k), "ka": row(k_a), "rk": row(r_k),
        "pa": proj_a.astype(BF16), "ones2": jnp.concatenate([ones_bd, ones_bd], axis=0),
        "lxw": row(lnx_w), "lxb": row(lnx_b), "pb": proj_b.astype(BF16), "wo": w_out.astype(BF16),
        "nfw": row(norm_ffn_w),
        "rwt": jnp.transpose(jnp.concatenate([router_group, jnp.zeros((D_MODEL, pad), F32), router_expert], axis=1)),
        "rbt": jnp.concatenate([router_group_bias, jnp.zeros((pad,), F32), router_expert_bias]).reshape(-1, 1),
        "wg": moe_w_gate.astype(BF16), "wu": moe_w_up.astype(BF16), "wd": moe_w_down.astype(BF16),
        "fw": row(norm_final_w),
    }


def kernel(x_prompt, x_sample, state_wkv, state_shift, norm_mix_w, w_in, sgu_ln_w, sgu_ln_b, sgu_w_s, sgu_b, rwkv_mu, w_lora_up, w_bias, a_lora_up, a_bias, g_lora_up, k_k, k_a, r_k, lnx_w, lnx_b, proj_a, proj_b, w_out, norm_ffn_w, router_group, router_group_bias, router_expert, router_expert_bias, moe_w_gate, moe_w_up, moe_w_down, norm_final_w):
    layer = [z[0] for z in (norm_mix_w, w_in, sgu_ln_w, sgu_ln_b, sgu_w_s, sgu_b, rwkv_mu, w_lora_up, w_bias,
                            a_lora_up, a_bias, g_lora_up, k_k, k_a, r_k, lnx_w, lnx_b, proj_a, proj_b, w_out,
                            norm_ffn_w, router_group, router_group_bias, router_expert, router_expert_bias,
                            moe_w_gate, moe_w_up, moe_w_down)]
    w = _prep_weights(*layer, norm_final_w)
    bsz, t_len, _ = x_prompt.shape
    n_s = x_sample.shape[0]

    (outa, gateb, g, bonus, r, lw, k, v, kn, bs, last) = _front_prompt(x_prompt, w, 256)
    y, st = _scan_prompt(r, lw, k, v, kn, bs, 128, N_PAIRS)
    flat = lambda z: z.reshape(bsz * t_len, D_MODEL)
    x1, tok, logt = _back(True, y, flat(bonus), flat(g), flat(gateb), flat(outa), flat(x_prompt), w, 512)
    y_prompt = _moe(tok, logt, x1, w, 512, 4).reshape(bsz, t_len, D_MODEL)
    st = st.reshape(bsz, N_PAIRS, 2, B_HEAD_DIM, 2, B_HEAD_DIM)
    wkv_p = jnp.stack([st[:, :, 0, :, 0, :], st[:, :, 1, :, 1, :]], axis=2)
    wkv_p = jnp.swapaxes(wkv_p.reshape(bsz, B_HEADS, B_HEAD_DIM, B_HEAD_DIM), -1, -2)

    xs2 = x_sample.reshape(n_s, D_MODEL)
    (outa, gateb, g, bonus, r, lw, k, v, kn, bs, cols_s, vn_s) = _front_sample(xs2, state_shift[0], w)
    y, wkv_s = _scan_sample(r, lw, k, v, kn, bs, state_wkv, 4)
    x1, tok, logt = _back(False, y, bonus, g, gateb, outa, xs2, w, n_s)
    y_sample = _moe(tok, logt, x1, w, n_s, 1).reshape(n_s, 1, D_MODEL)

    return (y_prompt, y_sample, wkv_p[None], last.reshape(1, bsz, SHIFT_WIDTH), wkv_s, cols_s[None],
            vn_s.reshape(1, n_s, 1, A_HEADS, A_HEAD_DIM))
```

```python
import functools
import math

import jax
import jax.numpy as jnp
from jax import lax
from jax.experimental import pallas as pl
from jax.experimental.pallas import tpu as pltpu

F32 = jnp.float32
BF16 = jnp.bfloat16
HIGHEST = lax.Precision.HIGHEST

D_MODEL = 1024
CHUNK = 128
A_HEADS = 8
A_HEAD_DIM = 128
B_HEADS = 16
B_HEAD_DIM = 64
N_PAIRS = B_HEADS // 2
SHIFT_WIDTH = 3328
OFF_V = 1024
OFF_SHIFT = 2048
OFF_GATE_A = OFF_SHIFT + SHIFT_WIDTH
OFF_GATE_B = OFF_GATE_A + D_MODEL
IN_COLS = OFF_GATE_B + D_MODEL
N_GROUPS = 4
EXPERTS_PER_GROUP = 8
N_EXPERTS = 32
D_EXPERT = 256
ROUTER_ROWS = 40
EXPERT_ROW0 = 8
SEG_ALIGN = 16
BACK_ROWS = 256
MOE_WINDOW = 64
MOE_EXPERTS_PER_STEP = 4
RMS_EPS = 1e-6
LN_EPS = 1e-5
GN_EPS = 64e-5

SCAN_CHUNK = 64
FRONT_BLOCK = 256
VMEM_LIMIT = 56 * 1024 * 1024


def _gelu(x):
    return x * (0.5 * (1.0 + jnp.tanh(math.sqrt(2.0 / math.pi) * (x + 0.044715 * (x * x * x)))))


def _sigmoid(x):
    return 1.0 / (1.0 + jnp.exp(-x))


def _softplus(z):
    return jnp.maximum(z, 0.0) + jnp.log(1.0 + jnp.exp(-jnp.abs(z)))


def _rms_norm(x, g):
    return x * lax.rsqrt(jnp.mean(x * x, axis=-1, keepdims=True) + RMS_EPS) * g


def _dot(a, b):
    return jnp.dot(a, b, preferred_element_type=F32)


def _dot_hi(a, b):
    return jnp.dot(a, b, preferred_element_type=F32, precision=HIGHEST)


def _head_sums(z, ones2):
    outs = []
    for p in range(z.shape[1] // 128):
        zp = z[:, p * 128:(p + 1) * 128]
        hi = zp.astype(BF16)
        lo = (zp - hi.astype(F32)).astype(BF16)
        outs.append(_dot(jnp.concatenate([hi, lo], axis=1), ones2))
    return jnp.concatenate(outs, axis=1)


def _front_kernel(is_sample, tm, *refs):
    if is_sample:
        (x_ref, prev_ref, nw_ref, win_ref, lnw_ref, lnb_ref, ws0_ref, bs0_ref, mu_ref, wl_ref, wb_ref,
         al_ref, ab_ref, gl_ref, kk_ref, ka_ref, rk_ref, pa_ref, ones2_ref,
         outa_ref, gateb_ref, g_ref, bonus_ref, r_ref, lw_ref, k_ref, v_ref, kn_ref, bs_ref,
         cols_ref, vn_ref, ya_ref) = refs
    else:
        (x_ref, nw_ref, win_ref, lnw_ref, lnb_ref, wsc_ref, bsb_ref, mu_ref, wl_ref, wb_ref,
         al_ref, ab_ref, gl_ref, kk_ref, ka_ref, rk_ref, pa_ref, ones2_ref,
         outa_ref, gateb_ref, g_ref, bonus_ref, r_ref, lw_ref, k_ref, v_ref, kn_ref, bs_ref,
         last_ref, ya_ref, carry_ref) = refs

        @pl.when(pl.program_id(1) == 0)
        def _():
            carry_ref[...] = jnp.zeros_like(carry_ref)

    xb = _rms_norm(x_ref[...], nw_ref[...]).astype(BF16)
    ones2 = ones2_ref[...]

    def proj(lo, width):
        return _dot(xb, win_ref[:, lo:lo + width])

    def shifted(lo, width, cols=None):
        cs = slice(lo, lo + width)
        if cols is None:
            cols = proj(OFF_SHIFT + lo, width)
        if is_sample:
            prev = prev_ref[:, cs]
            cols_ref[:, cs] = cols
        else:
            row = lax.broadcasted_iota(jnp.int32, cols.shape, 0)
            prev = jnp.where(row == 0, carry_ref[0:1, cs], pltpu.roll(cols, 1, 0))
            carry_ref[0:1, cs] = cols[tm - 1:tm, :]
            last_ref[:, cs] = cols[tm - 1:tm, :]
        return cols + (prev - cols) * mu_ref[:, cs]

    tail = shifted(3 * D_MODEL, 256)
    wa = tail[:, 0:128]
    twa = jnp.tanh(wa).astype(BF16)
    wab = wa.astype(BF16)
    sgd = _sigmoid(tail[:, 128:256]).astype(BF16)

    def block_matmuls(j):
        lo = j * FRONT_BLOCK
        cs = slice(lo, lo + FRONT_BLOCK)
        return (proj(OFF_SHIFT + lo, FRONT_BLOCK), proj(OFF_SHIFT + D_MODEL + lo, FRONT_BLOCK),
                proj(OFF_SHIFT + 2 * D_MODEL + lo, FRONT_BLOCK), _dot(twa, wl_ref[:, cs]), _dot(wab, al_ref[:, cs]),
                _dot(sgd, gl_ref[:, cs]), proj(lo, FRONT_BLOCK), proj(OFF_V + lo, FRONT_BLOCK),
                proj(OFF_GATE_B + lo, FRONT_BLOCK))

    n_blocks = D_MODEL // FRONT_BLOCK
    raw = block_matmuls(0)
    gate_a_raw = None
    for j in range(n_blocks):
        lo = j * FRONT_BLOCK
        cs = slice(lo, lo + FRONT_BLOCK)
        r_raw, k_raw, v_raw, lw_dot, a_dot, g_dot, u_raw, va_raw, gb_raw = raw
        if j + 1 < n_blocks:
            raw = block_matmuls(j + 1)
        else:
            gate_a_raw = proj(OFF_GATE_A, D_MODEL)

        r = shifted(lo, FRONT_BLOCK, r_raw)
        k = shifted(D_MODEL + lo, FRONT_BLOCK, k_raw)
        vb = shifted(2 * D_MODEL + lo, FRONT_BLOCK, v_raw)
        w_log = -_softplus(-(wb_ref[:, cs] + lw_dot)) - 0.5
        logw = -jnp.exp(w_log)
        a = _sigmoid(ab_ref[:, cs] + a_dot)
        g_ref[:, cs] = g_dot
        kk = k * kk_ref[:, cs]
        kkn = kk / jnp.maximum(jnp.sqrt(_head_sums(kk * kk, ones2)), 1e-12)
        k2 = k * (1.0 + (a - 1.0) * ka_ref[:, cs])
        bonus_ref[:, cs] = _head_sums(r * k2 * rk_ref[:, cs], ones2) * vb
        bsc = kkn * a
        outs = ((r_ref, r), (lw_ref, logw), (k_ref, k2), (v_ref, vb), (kn_ref, kkn), (bs_ref, bsc))
        for o_ref, val in outs:
            if is_sample:
                o_ref[:, cs] = val
            else:
                for q in range(FRONT_BLOCK // 128):
                    o_ref[lo // 128 + q] = val[:, q * 128:(q + 1) * 128]

        u2 = _gelu(u_raw)
        v2 = _gelu(va_raw)
        for q in range(FRONT_BLOCK // A_HEAD_DIM):
            h = lo // A_HEAD_DIM + q
            hs = slice(h * A_HEAD_DIM, (h + 1) * A_HEAD_DIM)
            u = u2[:, q * A_HEAD_DIM:(q + 1) * A_HEAD_DIM]
            v = v2[:, q * A_HEAD_DIM:(q + 1) * A_HEAD_DIM]
            mean = jnp.mean(v, axis=-1, keepdims=True)
            vc = v - mean
            var = jnp.mean(vc * vc, axis=-1, keepdims=True)
            vn = vc * lax.rsqrt(var + LN_EPS) * lnw_ref[:, hs] + lnb_ref[:, hs]
            if is_sample:
                vn_ref[:, hs] = vn
                ya_ref[:, hs] = (u * (vn * ws0_ref[:, hs] + bs0_ref[:, hs])).astype(BF16)
            else:
                vnb = vn.astype(BF16)
                for c in range(tm // CHUNK):
                    rs = slice(c * CHUNK, (c + 1) * CHUNK)
                    s = _dot(wsc_ref[h], vnb[rs, :]) + bsb_ref[:, hs]
                    ya_ref[rs, hs] = (u[rs, :] * s).astype(BF16)

        gateb_ref[:, cs] = _sigmoid(gb_raw)

    outa_ref[...] = _sigmoid(gate_a_raw) * _dot(ya_ref[...], pa_ref[...])


def _const_spec(shape, grid_rank):
    zeros = (0,) * len(shape)
    if grid_rank == 1:
        return pl.BlockSpec(shape, lambda i: zeros, pipeline_mode=pl.Buffered(1))
    return pl.BlockSpec(shape, lambda b, i: zeros, pipeline_mode=pl.Buffered(1))


def _front_prompt(x, w, tm):
    bsz, t_len, _ = x.shape
    nt = t_len // tm
    consts = [w["nw"], w["win"], w["lnw"], w["lnb"], w["wsc"], w["bsb"], w["mu"], w["wl"], w["wb"],
              w["al"], w["ab"], w["gl"], w["kk"], w["ka"], w["rk"], w["pa"], w["ones2"]]
    tok = lambda: pl.BlockSpec((None, tm, D_MODEL), lambda b, i: (b, i, 0))
    pair = lambda: pl.BlockSpec((None, N_PAIRS, tm, 128), lambda b, i: (b, 0, i, 0))
    tok_shape = jax.ShapeDtypeStruct((bsz, t_len, D_MODEL), F32)
    pair_shape = jax.ShapeDtypeStruct((bsz, N_PAIRS, t_len, 128), F32)
    return pl.pallas_call(
        functools.partial(_front_kernel, False, tm),
        grid=(bsz, nt),
        in_specs=[tok()] + [_const_spec(c.shape, 2) for c in consts],
        out_specs=[tok(), tok(), tok(), tok()] + [pair() for _ in range(6)]
        + [pl.BlockSpec((None, 1, SHIFT_WIDTH), lambda b, i: (b, 0, 0))],
        out_shape=[tok_shape] * 4 + [pair_shape] * 6 + [jax.ShapeDtypeStruct((bsz, 1, SHIFT_WIDTH), F32)],
        scratch_shapes=[pltpu.VMEM((tm, D_MODEL), BF16), pltpu.VMEM((8, SHIFT_WIDTH), F32)],
        compiler_params=pltpu.CompilerParams(dimension_semantics=("parallel", "arbitrary"),
                                             vmem_limit_bytes=VMEM_LIMIT),
        name="front_prompt",
    )(x, *consts)


def _front_sample(x, prev, w):
    n = x.shape[0]
    consts = [w["nw"], w["win"], w["lnw"], w["lnb"], w["ws0"], w["bs0"], w["mu"], w["wl"], w["wb"],
              w["al"], w["ab"], w["gl"], w["kk"], w["ka"], w["rk"], w["pa"], w["ones2"]]
    tok = lambda: pl.BlockSpec((n, D_MODEL), lambda i: (0, 0))
    wide = lambda: pl.BlockSpec((n, SHIFT_WIDTH), lambda i: (0, 0))
    tok_shape = jax.ShapeDtypeStruct((n, D_MODEL), F32)
    return pl.pallas_call(
        functools.partial(_front_kernel, True, n),
        grid=(1,),
        in_specs=[tok(), wide()] + [_const_spec(c.shape, 1) for c in consts],
        out_specs=[tok() for _ in range(10)] + [wide(), tok()],
        out_shape=[tok_shape] * 10 + [jax.ShapeDtypeStruct((n, SHIFT_WIDTH), F32), tok_shape],
        scratch_shapes=[pltpu.VMEM((n, D_MODEL), BF16)],
        compiler_params=pltpu.CompilerParams(dimension_semantics=("arbitrary",),
                                             vmem_limit_bytes=VMEM_LIMIT),
        name="front_sample",
    )(x, prev, *consts)


def _split3(x):
    hi = x.astype(BF16)
    r1 = x - hi.astype(F32)
    mid = r1.astype(BF16)
    lo = (r1 - mid.astype(F32)).astype(BF16)
    return hi, mid, lo


def _scan_kernel(tt, npp, r_ref, lw_ref, k_ref, v_ref, kn_ref, bs_ref, y_ref, sout_ref, st_ref):
    c_len = SCAN_CHUNK
    i = pl.program_id(2)

    @pl.when(i == 0)
    def _():
        st_ref[...] = jnp.zeros_like(st_ref)

    lane = lax.broadcasted_iota(jnp.int32, (c_len, 128), 1)
    head0 = lane < B_HEAD_DIM
    row2 = lax.broadcasted_iota(jnp.int32, (128, 128), 0)
    col2 = lax.broadcasted_iota(jnp.int32, (128, 128), 1)
    tpos = jnp.bitwise_and(row2, c_len - 1)
    spos = jnp.bitwise_and(col2, c_len - 1)
    strict = tpos > spos
    incl = tpos >= spos
    eye = row2 == col2
    rc = lax.broadcasted_iota(jnp.int32, (c_len, 3 * c_len), 0)
    cc = jnp.bitwise_and(lax.broadcasted_iota(jnp.int32, (c_len, 3 * c_len), 1), c_len - 1)
    tri3 = jnp.where(rc >= cc, 1.0, 0.0).astype(BF16)

    def pair_diag(z):
        return jnp.concatenate([jnp.where(head0, z, 0.0), jnp.where(head0, 0.0, z)], axis=0).astype(BF16)

    def bdot(a, b):
        return _dot(a.astype(BF16), b.astype(BF16))

    n_chunks = tt // c_len
    insts = [(c, q) for c in range(n_chunks) for q in range(npp)]
    rows = lambda c: slice(c * c_len, (c + 1) * c_len)

    log_p = {}
    for c, q in insts:
        log_p[c, q] = _dot(tri3, jnp.concatenate(_split3(lw_ref[q, rows(c), :]), axis=0))

    ops, gram = {}, {}
    for c, q in insts:
        lp = log_p[c, q]
        kn = kn_ref[q, rows(c), :]
        bs = bs_ref[q, rows(c), :]
        kx = k_ref[q, rows(c), :]
        log_pc = lp[c_len - 1:c_len, :]
        e_inv = jnp.exp(-lp)
        e_dec = jnp.exp(log_pc - lp)
        a_m = pair_diag(-kn * jnp.exp(lp - lw_ref[q, rows(c), :]))
        r_m = pair_diag(r_ref[q, rows(c), :] * jnp.exp(lp))
        v_m = pair_diag(v_ref[q, rows(c), :])
        btkt = jnp.concatenate([pair_diag(bs * e_dec), pair_diag(kx * e_dec)], axis=0)
        pc_col = jnp.sum(jnp.where(eye, jnp.exp(log_pc), 0.0), axis=1, keepdims=True)
        ops[c, q] = (a_m, r_m, v_m, btkt, pc_col)
        gram[c, q] = lax.dot_general(jnp.concatenate([a_m, r_m], axis=0),
                                     jnp.concatenate([pair_diag(bs * e_inv), pair_diag(kx * e_inv)], axis=0),
                                     (((1,), (1,)), ((), ())), preferred_element_type=F32)

    n_sum, pw, l_r, lakv = {}, {}, {}, {}
    for c, q in insts:
        g = gram[c, q]
        l_ab = jnp.where(strict, g[0:128, 0:128], 0.0)
        n_sum[c, q] = l_ab
        pw[c, q] = l_ab.astype(BF16)
        l_r[c, q] = jnp.concatenate([jnp.where(incl, g[128:256, 0:128], 0.0),
                                     jnp.where(incl, g[128:256, 128:256], 0.0)], axis=1).astype(BF16)
        lakv[c, q] = _dot(jnp.where(strict, g[0:128, 128:256], 0.0).astype(BF16), ops[c, q][2])

    pw_f = {}
    for c, q in insts:
        pw_f[c, q] = _dot(pw[c, q], pw[c, q])
        pw[c, q] = pw_f[c, q].astype(BF16)
    for _ in range(int(math.log2(c_len)) - 2):
        prod = {}
        for c, q in insts:
            prod[c, q] = _dot(pw[c, q], jnp.concatenate([pw[c, q], n_sum[c, q].astype(BF16)], axis=1))
        for c, q in insts:
            n_sum[c, q] = n_sum[c, q] + pw_f[c, q] + prod[c, q][:, 128:256]
            pw_f[c, q] = prod[c, q][:, 0:128]
            pw[c, q] = pw_f[c, q].astype(BF16)
    for c, q in insts:
        n_sum[c, q] = n_sum[c, q] + pw_f[c, q] + bdot(pw[c, q], n_sum[c, q])

    wu = {}
    for c, q in insts:
        rhs = jnp.concatenate([ops[c, q][0].astype(F32), lakv[c, q]], axis=1)
        wu[c, q] = rhs + bdot(n_sum[c, q], rhs)

    for c in range(n_chunks):
        xs, st = {}, {}
        for q in range(npp):
            st[q] = st_ref[q]
            xs[q] = bdot(jnp.concatenate([wu[c, q][:, 0:128].astype(BF16), ops[c, q][1]], axis=0), st[q])
        for q in range(npp):
            _, _, v_m, btkt, pc_col = ops[c, q]
            uv = jnp.concatenate([(xs[q][0:128, :] + wu[c, q][:, 128:256]).astype(BF16), v_m], axis=0)
            y_m = xs[q][128:256, :] + _dot(l_r[c, q], uv)
            y_ref[q, rows(c), :] = y_m[0:c_len, :] + y_m[c_len:128, :]
            st_ref[q] = pc_col * st[q] + lax.dot_general(btkt, uv, (((0,), (0,)), ((), ())),
                                                         preferred_element_type=F32)

    @pl.when(i == pl.num_programs(2) - 1)
    def _():
        n = B_HEAD_DIM
        for q in range(npp):
            full_t = st_ref[q].T
            sout_ref[2 * q] = full_t[0:n, 0:n]
            sout_ref[2 * q + 1] = full_t[n:128, n:128]


def _scan_prompt(r, lw, k, v, kn, bs, tt, npp):
    bsz, n_pairs, t_len, _ = r.shape
    blk = lambda: pl.BlockSpec((None, npp, tt, 128), lambda b, p, i: (b, p, i, 0))
    return pl.pallas_call(
        functools.partial(_scan_kernel, tt, npp),
        grid=(bsz, n_pairs // npp, t_len // tt),
        in_specs=[blk() for _ in range(6)],
        out_specs=[blk(), pl.BlockSpec((None, 2 * npp, B_HEAD_DIM, B_HEAD_DIM), lambda b, p, i: (b, p, 0, 0))],
        out_shape=[jax.ShapeDtypeStruct((bsz, n_pairs, t_len, 128), F32),
                   jax.ShapeDtypeStruct((bsz, 2 * n_pairs, B_HEAD_DIM, B_HEAD_DIM), F32)],
        scratch_shapes=[pltpu.VMEM((npp, 128, 128), F32)],
        compiler_params=pltpu.CompilerParams(dimension_semantics=("parallel", "parallel", "arbitrary"),
                                             vmem_limit_bytes=VMEM_LIMIT),
        name="scan_prompt",
    )(r, lw, k, v, kn, bs)


def _step_kernel(tb, r_ref, lw_ref, k_ref, v_ref, kn_ref, bs_ref, s_ref, y_ref, sout_ref):
    n = B_HEAD_DIM
    row = lax.broadcasted_iota(jnp.int32, (128, 128), 0)
    col = lax.broadcasted_iota(jnp.int32, (128, 128), 1)
    same_head = (row < n) == (col < n)
    eye = jnp.where(row == col, 1.0, 0.0).astype(BF16)
    zero_f = jnp.zeros((n, n), F32)
    zero_b = jnp.zeros((128, 128), BF16)
    tn = (((1,), (1,)), ((), ()))
    insts = [(t, p) for t in range(tb) for p in range(N_PAIRS)]
    ps = lambda p: slice(p * 128, (p + 1) * 128)
    rep = lambda z: jnp.broadcast_to(z, (128, 128))

    s0, sav = {}, {}
    for t, p in insts:
        s0[t, p] = jnp.concatenate([jnp.concatenate([s_ref[t, 2 * p], zero_f], axis=1),
                                    jnp.concatenate([zero_f, s_ref[t, 2 * p + 1]], axis=1)], axis=0)
        v_row = v_ref[t, :, ps(p)]
        v_hi = v_row.astype(BF16)
        v_lo = (v_row - v_hi.astype(F32)).astype(BF16)
        lhs = jnp.concatenate([s0[t, p].astype(BF16), eye, eye], axis=1)
        rhs = jnp.concatenate(
            [jnp.concatenate([rep((-kn_ref[t, :, ps(p)]).astype(BF16)), zero_b, zero_b], axis=1),
             jnp.concatenate([zero_b, rep(v_hi), rep(v_lo)], axis=1)], axis=0)
        sav[t, p] = lax.dot_general(lhs, rhs, tn, preferred_element_type=F32)
    s1 = {}
    for t, p in insts:
        upd = (s0[t, p] * jnp.exp(lw_ref[t, :, ps(p)]) + sav[t, p][:, 0:128] * bs_ref[t, :, ps(p)]
               + sav[t, p][:, 128:256] * k_ref[t, :, ps(p)])
        upd = jnp.where(same_head, upd, 0.0)
        sout_ref[t, 2 * p] = upd[0:n, 0:n]
        sout_ref[t, 2 * p + 1] = upd[n:128, n:128]
        s1[t, p] = upd.astype(BF16)
    for t, p in insts:
        r8 = jnp.broadcast_to(r_ref[t, :, ps(p)].astype(BF16), (8, 128))
        y_ref[t, :, ps(p)] = lax.dot_general(r8, s1[t, p], tn, preferred_element_type=F32)[0:1, :]


def _scan_sample(r, lw, k, v, kn, bs, s0, tb):
    n = r.shape[0]
    vecs = [z.reshape(n, 1, D_MODEL) for z in (r, lw, k, v, kn, bs)]
    vec = lambda: pl.BlockSpec((tb, 1, D_MODEL), lambda i: (i, 0, 0))
    st = lambda: pl.BlockSpec((None, tb, B_HEADS, B_HEAD_DIM, B_HEAD_DIM), lambda i: (0, i, 0, 0, 0))
    y, s1 = pl.pallas_call(
        functools.partial(_step_kernel, tb),
        grid=(n // tb,),
        in_specs=[vec() for _ in range(6)] + [st()],
        out_specs=[vec(), st()],
        out_shape=[jax.ShapeDtypeStruct((n, 1, D_MODEL), F32), jax.ShapeDtypeStruct(s0.shape, F32)],
        compiler_params=pltpu.CompilerParams(dimension_semantics=("parallel",)),
        name="scan_sample",
    )(*vecs, s0)
    return y.reshape(n, D_MODEL), s1


def _back_kernel(paired, y_ref, bonus_ref, g_ref, gateb_ref, outa_ref, x_ref, lxw_ref, lxb_ref, pb_ref,
                 wo_ref, nfw_ref, rw_ref, rb_ref, ones2_ref, x1_ref, tok_ref, logt_ref):
    tm = x_ref.shape[0]
    n_parts = max(1, tm // BACK_ROWS)
    ones2 = ones2_ref[...]
    inv_n = 1.0 / B_HEAD_DIM
    parts = range(n_parts)
    rows = lambda i: slice(i * (tm // n_parts), (i + 1) * (tm // n_parts))

    if paired:
        y = [jnp.concatenate([y_ref[p, rows(i), :] for p in range(N_PAIRS)], axis=1) for i in parts]
    else:
        y = [y_ref[rows(i), :] for i in parts]
    mean = [_head_sums(y[i], ones2) * inv_n for i in parts]
    yc = [y[i] - mean[i] for i in parts]
    var = [_head_sums(yc[i] * yc[i], ones2) * inv_n for i in parts]
    yb = [((yc[i] * lax.rsqrt(var[i] + GN_EPS) * lxw_ref[...] + lxb_ref[...] + bonus_ref[rows(i), :])
           * g_ref[rows(i), :]).astype(BF16) for i in parts]
    mb = [_dot(yb[i], pb_ref[...]) for i in parts]
    merged = [(outa_ref[rows(i), :] + gateb_ref[rows(i), :] * mb[i]).astype(BF16) for i in parts]
    out = [_dot(merged[i], wo_ref[...]) for i in parts]
    tok = []
    for i in parts:
        x1 = x_ref[rows(i), :] + out[i]
        x1_ref[rows(i), :] = x1
        tok.append(_rms_norm(x1, nfw_ref[...]))
        tok_ref[rows(i), :] = tok[i].astype(BF16)
    for i in parts:
        t_hi = tok[i].astype(BF16)
        t_lo = (tok[i] - t_hi.astype(F32)).astype(BF16)
        logt_ref[:, rows(i)] = lax.dot_general(
            rw_ref[...], jnp.concatenate([t_hi, t_hi, t_lo], axis=1), (((1,), (1,)), ((), ())),
            preferred_element_type=F32) + rb_ref[...]


def _back(paired, y, bonus, g, gateb, outa, x, w, tm):
    n = x.shape[0]
    consts = [w["lxw"], w["lxb"], w["pb"], w["wo"], w["nfw"], w["rw3"], w["rbt"], w["ones2"]]
    tok = lambda: pl.BlockSpec((tm, D_MODEL), lambda i: (i, 0))
    if paired:
        nt = y.shape[2] // tm
        y_spec = pl.BlockSpec((None, N_PAIRS, tm, 128), lambda i: (i // nt, 0, i % nt, 0))
    else:
        y_spec = tok()
    return pl.pallas_call(
        functools.partial(_back_kernel, paired),
        grid=(n // tm,),
        in_specs=[y_spec] + [tok() for _ in range(5)] + [_const_spec(c.shape, 1) for c in consts],
        out_specs=[tok(), tok(), pl.BlockSpec((ROUTER_ROWS, tm), lambda i: (0, i))],
        out_shape=[jax.ShapeDtypeStruct((n, D_MODEL), F32), jax.ShapeDtypeStruct((n, D_MODEL), BF16),
                   jax.ShapeDtypeStruct((ROUTER_ROWS, n), F32)],
        compiler_params=pltpu.CompilerParams(dimension_semantics=("parallel",), vmem_limit_bytes=VMEM_LIMIT),
        name="back_prompt" if paired else "back_sample",
    )(y, bonus, g, gateb, outa, x, *consts)


def _moe_slots(ts):
    n = 2 * ts + N_EXPERTS * (SEG_ALIGN - 1)
    return -(-n // MOE_WINDOW) * MOE_WINDOW


def _split3_f32(x):
    hi = x.astype(BF16).astype(F32)
    mid = (x - hi).astype(BF16).astype(F32)
    lo = (x - hi - mid).astype(BF16).astype(F32)
    return hi, mid, lo


def _moe_kernel(ts, ns, tok_ref, logt_ref, x1_ref, utri_ref, wg_ref, wu_ref, wd_ref, fw_ref, o_ref,
                xy_ref, pg_ref, ws_ref, meta_ref):
    n_slots = xy_ref.shape[1]
    n_esteps = N_EXPERTS // MOE_EXPERTS_PER_STEP
    step = pl.program_id(1)
    tn = (((1,), (1,)), ((), ()))

    @pl.when(step < ns)
    def _route_and_sort():
        lt = logt_ref[...]
        neg = jnp.float32(-jnp.inf)
        big = jnp.float32(99.0)
        row8 = lax.broadcasted_iota(jnp.int32, (8, ts), 0).astype(F32)
        is_grp = row8 < N_GROUPS
        lg = jnp.where(is_grp, lt[0:8, :], neg)
        gmax = jnp.max(lg, axis=0, keepdims=True)
        grp = jnp.min(jnp.where(lg == gmax, row8, big), axis=0, keepdims=True)
        p_grp = 1.0 / jnp.sum(jnp.where(is_grp, jnp.exp(lg - gmax), 0.0), axis=0, keepdims=True)
        le = lt[EXPERT_ROW0:EXPERT_ROW0 + EXPERTS_PER_GROUP, :]
        for g in range(1, N_GROUPS):
            lo = EXPERT_ROW0 + g * EXPERTS_PER_GROUP
            le = jnp.where(grp == g, lt[lo:lo + EXPERTS_PER_GROUP, :], le)
        top1 = jnp.max(le, axis=0, keepdims=True)
        i1 = jnp.min(jnp.where(le == top1, row8, big), axis=0, keepdims=True)
        le2 = jnp.where(row8 == i1, neg, le)
        top2 = jnp.max(le2, axis=0, keepdims=True)
        i2 = jnp.min(jnp.where(le2 == top2, row8, big), axis=0, keepdims=True)
        e2 = jnp.exp(top2 - top1)
        w1 = p_grp / (1.0 + e2)
        w2 = p_grp * e2 / (1.0 + e2)

        row_e = lax.broadcasted_iota(jnp.int32, (N_EXPERTS, ts), 0).astype(F32)
        a1 = row_e == grp * EXPERTS_PER_GROUP + i1
        a2 = row_e == grp * EXPERTS_PER_GROUP + i2
        at = jnp.where(a1, 1.0, 0.0) + jnp.where(a2, 1.0, 0.0)
        rank = _dot(at.astype(BF16), utri_ref[...])
        cnt = rank[:, ts - 1:ts] + at[:, ts - 1:ts]
        cntp = jnp.floor((cnt + (SEG_ALIGN - 1)) * (1.0 / SEG_ALIGN)) * SEG_ALIGN
        cntp_b = jnp.broadcast_to(cntp, (N_EXPERTS, 128))
        r_e = lax.broadcasted_iota(jnp.int32, (N_EXPERTS, N_EXPERTS), 0)
        c_e = lax.broadcasted_iota(jnp.int32, (N_EXPERTS, N_EXPERTS), 1)
        off = _dot(jnp.where(r_e > c_e, 1.0, 0.0).astype(BF16), cntp_b.astype(BF16))
        slot = off[:, 0:1] + rank
        s1 = jnp.sum(jnp.where(a1, slot, 0.0), axis=0, keepdims=True)
        s2 = jnp.sum(jnp.where(a2, slot, 0.0), axis=0, keepdims=True)
        srow = lax.broadcasted_iota(jnp.int32, (n_slots, ts), 0).astype(F32)
        pg1 = jnp.where(srow == s1, 1.0, 0.0)
        pg2 = jnp.where(srow == s2, 1.0, 0.0)
        pg = (pg1 + pg2).astype(BF16)
        pg_ref[step] = pg
        xy_ref[step] = _dot(pg, tok_ref[...]).astype(BF16)

        wrows = jnp.concatenate(
            [jnp.concatenate([p1, p2], axis=1) for p1, p2 in zip(_split3_f32(w1), _split3_f32(w2))]
            + [jnp.zeros((5, 2 * ts), F32)], axis=0).astype(BF16)
        wsl = lax.dot_general(jnp.concatenate([pg1.astype(BF16), pg2.astype(BF16)], axis=1), wrows, tn,
                              preferred_element_type=F32)
        ws_ref[step] = jnp.broadcast_to(wsl[:, 0:1] + wsl[:, 1:2] + wsl[:, 2:3], (n_slots, 128))
        meta_ref[step, 0:N_EXPERTS, :] = off.astype(jnp.int32)
        meta_ref[step, N_EXPERTS:2 * N_EXPERTS, :] = cntp_b.astype(jnp.int32)

    @pl.when((step >= ns) & (step < ns + n_esteps))
    def _experts():
        riota = lax.broadcasted_iota(jnp.int32, (MOE_WINDOW, 1), 0)
        for el in range(MOE_EXPERTS_PER_STEP):
            e = (step - ns) * MOE_EXPERTS_PER_STEP + el
            offs = [meta_ref[j, pl.ds(e, 1), :][0, 0] for j in range(ns)]
            cnts = [meta_ref[j, pl.ds(N_EXPERTS + e, 1), :][0, 0] for j in range(ns)]
            n_win = functools.reduce(jnp.maximum, [(c + MOE_WINDOW - 1) // MOE_WINDOW for c in cnts])

            def window(w, carry, el=el, offs=offs, cnts=cnts):
                starts, pieces = [], []
                for j in range(ns):
                    st = jnp.minimum(offs[j] + w * MOE_WINDOW, n_slots - MOE_WINDOW)
                    starts.append(pl.multiple_of(st, SEG_ALIGN))
                    pieces.append(xy_ref[j, pl.ds(starts[j], MOE_WINDOW), :])
                lhs = jnp.concatenate(pieces, axis=0)
                gate = _dot(lhs, wg_ref[el])
                h = gate * _sigmoid(gate) * _dot(lhs, wu_ref[el])
                y = _dot(h.astype(BF16), wd_ref[el])
                for j in range(ns):
                    rows = starts[j] + riota
                    valid = (rows >= offs[j] + w * MOE_WINDOW) & (rows < offs[j] + cnts[j])
                    yw = (y[j * MOE_WINDOW:(j + 1) * MOE_WINDOW, :]
                          * ws_ref[j, pl.ds(starts[j], MOE_WINDOW), 0:1])
                    xy_ref[j, pl.ds(starts[j], MOE_WINDOW), :] = jnp.where(
                        valid, yw, pieces[j].astype(F32)).astype(BF16)
                return carry

            lax.fori_loop(0, n_win, window, 0)

    @pl.when(step >= ns + n_esteps)
    def _combine():
        j = step - ns - n_esteps
        c = lax.dot_general(pg_ref[j], xy_ref[j], (((0,), (0,)), ((), ())), preferred_element_type=F32)
        o_ref[...] = _rms_norm(x1_ref[...] + c, fw_ref[...])


def _moe(tok, logt, x1, w, ts, ns):
    n = tok.shape[0]
    n_slots = _moe_slots(ts)
    n_esteps = N_EXPERTS // MOE_EXPERTS_PER_STEP
    ne = MOE_EXPERTS_PER_STEP
    sub_in = lambda s, k: s * ns + jnp.clip(k, 0, ns - 1)
    sub_out = lambda s, k: s * ns + jnp.clip(k - ns - n_esteps, 0, ns - 1)
    expert = lambda k: jnp.clip(k - ns, 0, n_esteps - 1)
    utri = (jnp.arange(ts)[:, None] < jnp.arange(ts)[None, :]).astype(BF16)
    return pl.pallas_call(
        functools.partial(_moe_kernel, ts, ns),
        grid=(n // (ts * ns), 2 * ns + n_esteps),
        in_specs=[pl.BlockSpec((ts, D_MODEL), lambda s, k: (sub_in(s, k), 0)),
                  pl.BlockSpec((ROUTER_ROWS, ts), lambda s, k: (0, sub_in(s, k))),
                  pl.BlockSpec((ts, D_MODEL), lambda s, k: (sub_out(s, k), 0)),
                  pl.BlockSpec((ts, ts), lambda s, k: (0, 0)),
                  pl.BlockSpec((ne, D_MODEL, D_EXPERT), lambda s, k: (expert(k), 0, 0)),
                  pl.BlockSpec((ne, D_MODEL, D_EXPERT), lambda s, k: (expert(k), 0, 0)),
                  pl.BlockSpec((ne, D_EXPERT, D_MODEL), lambda s, k: (expert(k), 0, 0)),
                  pl.BlockSpec((1, D_MODEL), lambda s, k: (0, 0))],
        out_specs=pl.BlockSpec((ts, D_MODEL), lambda s, k: (sub_out(s, k), 0)),
        out_shape=jax.ShapeDtypeStruct((n, D_MODEL), F32),
        scratch_shapes=[pltpu.VMEM((ns, n_slots, D_MODEL), BF16), pltpu.VMEM((ns, n_slots, ts), BF16),
                        pltpu.VMEM((ns, n_slots, 128), F32), pltpu.VMEM((ns, 2 * N_EXPERTS, 128), jnp.int32)],
        compiler_params=pltpu.CompilerParams(dimension_semantics=("parallel", "arbitrary"),
                                             vmem_limit_bytes=VMEM_LIMIT),
        name="moe",
    )(tok, logt, x1, utri, w["wg"], w["wu"], w["wd"], w["fw"])


def _prep_weights(norm_mix_w, w_in, sgu_ln_w, sgu_ln_b, sgu_w_s, sgu_b, rwkv_mu, w_lora_up, w_bias,
                  a_lora_up, a_bias, g_lora_up, k_k, k_a, r_k, lnx_w, lnx_b, proj_a, proj_b, w_out,
                  norm_ffn_w, router_group, router_group_bias, router_expert, router_expert_bias,
                  moe_w_gate, moe_w_up, moe_w_down, norm_final_w):
    row = lambda z: z.reshape(1, -1).astype(F32)
    causal = jnp.tril(jnp.ones((CHUNK, CHUNK), dtype=bool))
    zeros_lora = jnp.zeros((64, D_MODEL), F32)
    head = jnp.arange(128) // B_HEAD_DIM
    ones_bd = (head[:, None] == head[None, :]).astype(BF16)
    pad = EXPERT_ROW0 - N_GROUPS
    rw = jnp.transpose(jnp.concatenate([router_group, jnp.zeros((D_MODEL, pad), F32), router_expert],
                                       axis=1))
    rw_hi = rw.astype(BF16)
    return {
        "nw": row(norm_mix_w), "win": w_in.astype(BF16), "lnw": row(sgu_ln_w), "lnb": row(sgu_ln_b),
        "wsc": jnp.where(causal[None], sgu_w_s, 0.0).astype(BF16),
        "bsb": jnp.repeat(jnp.transpose(sgu_b), A_HEAD_DIM, axis=1),
        "ws0": jnp.repeat(sgu_w_s[:, 0, 0], A_HEAD_DIM).reshape(1, -1),
        "bs0": jnp.repeat(sgu_b[:, 0], A_HEAD_DIM).reshape(1, -1),
        "mu": row(rwkv_mu),
        "wl": jnp.concatenate([w_lora_up, zeros_lora], axis=0).astype(BF16), "wb": row(w_bias),
        "al": jnp.concatenate([zeros_lora, a_lora_up], axis=0).astype(BF16), "ab": row(a_bias),
        "gl": g_lora_up.astype(BF16), "kk": row(k_k), "ka": row(k_a), "rk": row(r_k),
        "pa": proj_a.astype(BF16), "ones2": jnp.concatenate([ones_bd, ones_bd], axis=0),
        "lxw": row(lnx_w), "lxb": row(lnx_b), "pb": proj_b.astype(BF16), "wo": w_out.astype(BF16),
        "nfw": row(norm_ffn_w),
        "rw3": jnp.concatenate([rw_hi, (rw - rw_hi.astype(F32)).astype(BF16), rw_hi], axis=1),
        "rbt": jnp.concatenate([router_group_bias, jnp.zeros((pad,), F32), router_expert_bias]).reshape(-1, 1),
        "wg": moe_w_gate.astype(BF16), "wu": moe_w_up.astype(BF16), "wd": moe_w_down.astype(BF16),
        "fw": row(norm_final_w),
    }


def kernel(x_prompt, x_sample, state_wkv, state_shift, norm_mix_w, w_in, sgu_ln_w, sgu_ln_b, sgu_w_s, sgu_b, rwkv_mu, w_lora_up, w_bias, a_lora_up, a_bias, g_lora_up, k_k, k_a, r_k, lnx_w, lnx_b, proj_a, proj_b, w_out, norm_ffn_w, router_group, router_group_bias, router_expert, router_expert_bias, moe_w_gate, moe_w_up, moe_w_down, norm_final_w):
    layer = [z[0] for z in (norm_mix_w, w_in, sgu_ln_w, sgu_ln_b, sgu_w_s, sgu_b, rwkv_mu, w_lora_up, w_bias,
                            a_lora_up, a_bias, g_lora_up, k_k, k_a, r_k, lnx_w, lnx_b, proj_a, proj_b, w_out,
                            norm_ffn_w, router_group, router_group_bias, router_expert, router_expert_bias,
                            moe_w_gate, moe_w_up, moe_w_down)]
    w = _prep_weights(*layer, norm_final_w)
    bsz, t_len, _ = x_prompt.shape
    n_s = x_sample.shape[0]

    (outa, gateb, g, bonus, r, lw, k, v, kn, bs, last) = _front_prompt(x_prompt, w, 256)
    y, wkv_p = _scan_prompt(r, lw, k, v, kn, bs, 128, N_PAIRS)
    flat = lambda z: z.reshape(bsz * t_len, D_MODEL)
    x1, tok, logt = _back(True, y, flat(bonus), flat(g), flat(gateb), flat(outa), flat(x_prompt), w, 512)
    y_prompt = _moe(tok, logt, x1, w, 512, 4).reshape(bsz, t_len, D_MODEL)

    xs2 = x_sample.reshape(n_s, D_MODEL)
    (outa, gateb, g, bonus, r, lw, k, v, kn, bs, cols_s, vn_s) = _front_sample(xs2, state_shift[0], w)
    y, wkv_s = _scan_sample(r, lw, k, v, kn, bs, state_wkv, 4)
    x1, tok, logt = _back(False, y, bonus, g, gateb, outa, xs2, w, n_s)
    y_sample = _moe(tok, logt, x1, w, n_s, 1).reshape(n_s, 1, D_MODEL)

    return (y_prompt, y_sample, wkv_p[None], last.reshape(1, bsz, SHIFT_WIDTH), wkv_s, cols_s[None],
            vn_s.reshape(1, n_s, 1, A_HEADS, A_HEAD_DIM))
```

```python
import functools
import math

import jax
import jax.numpy as jnp
from jax import lax
from jax.experimental import pallas as pl
from jax.experimental.pallas import tpu as pltpu

F32 = jnp.float32
BF16 = jnp.bfloat16
HIGHEST = lax.Precision.HIGHEST

D_MODEL = 1024
CHUNK = 128
A_HEADS = 8
A_HEAD_DIM = 128
B_HEADS = 16
B_HEAD_DIM = 64
N_PAIRS = B_HEADS // 2
SHIFT_WIDTH = 3328
OFF_V = 1024
OFF_SHIFT = 2048
OFF_GATE_A = OFF_SHIFT + SHIFT_WIDTH
OFF_GATE_B = OFF_GATE_A + D_MODEL
IN_COLS = OFF_GATE_B + D_MODEL
N_GROUPS = 4
EXPERTS_PER_GROUP = 8
N_EXPERTS = 32
D_EXPERT = 256
ROUTER_ROWS = 40
EXPERT_ROW0 = 8
SEG_ALIGN = 16
BACK_ROWS = 256
MOE_WINDOW = 64
MOE_EXPERTS_PER_STEP = 4
RMS_EPS = 1e-6
LN_EPS = 1e-5
GN_EPS = 64e-5

SCAN_CHUNK = 64
FRONT_BLOCK = 256
VMEM_LIMIT = 56 * 1024 * 1024


def _gelu(x):
    return x * (0.5 * (1.0 + jnp.tanh(math.sqrt(2.0 / math.pi) * (x + 0.044715 * (x * x * x)))))


def _sigmoid(x):
    return 1.0 / (1.0 + jnp.exp(-x))


def _softplus(z):
    return jnp.maximum(z, 0.0) + jnp.log(1.0 + jnp.exp(-jnp.abs(z)))


def _rms_norm(x, g):
    return x * lax.rsqrt(jnp.mean(x * x, axis=-1, keepdims=True) + RMS_EPS) * g


def _dot(a, b):
    return jnp.dot(a, b, preferred_element_type=F32)


def _dot_hi(a, b):
    return jnp.dot(a, b, preferred_element_type=F32, precision=HIGHEST)


def _head_sums(z, ones2):
    outs = []
    for p in range(z.shape[1] // 128):
        zp = z[:, p * 128:(p + 1) * 128]
        hi = zp.astype(BF16)
        lo = (zp - hi.astype(F32)).astype(BF16)
        outs.append(_dot(jnp.concatenate([hi, lo], axis=1), ones2))
    return jnp.concatenate(outs, axis=1)


def _front_kernel(is_sample, tm, *refs):
    if is_sample:
        (x_ref, prev_ref, nw_ref, win_ref, lnw_ref, lnb_ref, ws0_ref, bs0_ref, mu_ref, wl_ref, wb_ref,
         al_ref, ab_ref, gl_ref, kk_ref, ka_ref, rk_ref, pa_ref, ones2_ref,
         outa_ref, gateb_ref, g_ref, bonus_ref, r_ref, lw_ref, k_ref, v_ref, kn_ref, bs_ref,
         cols_ref, vn_ref, ya_ref) = refs
    else:
        (x_ref, nw_ref, win_ref, lnw_ref, lnb_ref, wsc_ref, bsb_ref, mu_ref, wl_ref, wb_ref,
         al_ref, ab_ref, gl_ref, kk_ref, ka_ref, rk_ref, pa_ref, ones2_ref,
         outa_ref, gateb_ref, g_ref, bonus_ref, r_ref, lw_ref, k_ref, v_ref, kn_ref, bs_ref,
         last_ref, ya_ref, carry_ref) = refs

        @pl.when(pl.program_id(1) == 0)
        def _():
            carry_ref[...] = jnp.zeros_like(carry_ref)

    xb = _rms_norm(x_ref[...], nw_ref[...]).astype(BF16)
    ones2 = ones2_ref[...]

    def proj(lo, width):
        return _dot(xb, win_ref[:, lo:lo + width])

    def shifted(lo, width, cols=None):
        cs = slice(lo, lo + width)
        if cols is None:
            cols = proj(OFF_SHIFT + lo, width)
        if is_sample:
            prev = prev_ref[:, cs]
            cols_ref[:, cs] = cols
        else:
            row = lax.broadcasted_iota(jnp.int32, cols.shape, 0)
            prev = jnp.where(row == 0, carry_ref[0:1, cs], pltpu.roll(cols, 1, 0))
            carry_ref[0:1, cs] = cols[tm - 1:tm, :]
            last_ref[:, cs] = cols[tm - 1:tm, :]
        return cols + (prev - cols) * mu_ref[:, cs]

    tail = shifted(3 * D_MODEL, 256)
    wa = tail[:, 0:128]
    twa = jnp.tanh(wa).astype(BF16)
    wab = wa.astype(BF16)
    sgd = _sigmoid(tail[:, 128:256]).astype(BF16)

    def block_matmuls(j):
        lo = j * FRONT_BLOCK
        cs = slice(lo, lo + FRONT_BLOCK)
        return (proj(OFF_SHIFT + lo, FRONT_BLOCK), proj(OFF_SHIFT + D_MODEL + lo, FRONT_BLOCK),
                proj(OFF_SHIFT + 2 * D_MODEL + lo, FRONT_BLOCK), _dot(twa, wl_ref[:, cs]), _dot(wab, al_ref[:, cs]),
                _dot(sgd, gl_ref[:, cs]), proj(lo, FRONT_BLOCK), proj(OFF_V + lo, FRONT_BLOCK),
                proj(OFF_GATE_B + lo, FRONT_BLOCK))

    n_blocks = D_MODEL // FRONT_BLOCK
    raw = block_matmuls(0)
    gate_a_raw = None
    for j in range(n_blocks):
        lo = j * FRONT_BLOCK
        cs = slice(lo, lo + FRONT_BLOCK)
        r_raw, k_raw, v_raw, lw_dot, a_dot, g_dot, u_raw, va_raw, gb_raw = raw
        if j + 1 < n_blocks:
            raw = block_matmuls(j + 1)
        else:
            gate_a_raw = proj(OFF_GATE_A, D_MODEL)

        r = shifted(lo, FRONT_BLOCK, r_raw)
        k = shifted(D_MODEL + lo, FRONT_BLOCK, k_raw)
        vb = shifted(2 * D_MODEL + lo, FRONT_BLOCK, v_raw)
        w_log = -_softplus(-(wb_ref[:, cs] + lw_dot)) - 0.5
        logw = -jnp.exp(w_log)
        a = _sigmoid(ab_ref[:, cs] + a_dot)
        g_ref[:, cs] = g_dot
        kk = k * kk_ref[:, cs]
        kkn = kk / jnp.maximum(jnp.sqrt(_head_sums(kk * kk, ones2)), 1e-12)
        k2 = k * (1.0 + (a - 1.0) * ka_ref[:, cs])
        bonus_ref[:, cs] = _head_sums(r * k2 * rk_ref[:, cs], ones2) * vb
        bsc = kkn * a
        outs = ((r_ref, r), (lw_ref, logw), (k_ref, k2), (v_ref, vb), (kn_ref, kkn), (bs_ref, bsc))
        for o_ref, val in outs:
            if is_sample:
                o_ref[:, cs] = val
            else:
                for q in range(FRONT_BLOCK // 128):
                    o_ref[lo // 128 + q] = val[:, q * 128:(q + 1) * 128]

        u2 = _gelu(u_raw)
        v2 = _gelu(va_raw)
        for q in range(FRONT_BLOCK // A_HEAD_DIM):
            h = lo // A_HEAD_DIM + q
            hs = slice(h * A_HEAD_DIM, (h + 1) * A_HEAD_DIM)
            u = u2[:, q * A_HEAD_DIM:(q + 1) * A_HEAD_DIM]
            v = v2[:, q * A_HEAD_DIM:(q + 1) * A_HEAD_DIM]
            mean = jnp.mean(v, axis=-1, keepdims=True)
            vc = v - mean
            var = jnp.mean(vc * vc, axis=-1, keepdims=True)
            vn = vc * lax.rsqrt(var + LN_EPS) * lnw_ref[:, hs] + lnb_ref[:, hs]
            if is_sample:
                vn_ref[:, hs] = vn
                ya_ref[:, hs] = (u * (vn * ws0_ref[:, hs] + bs0_ref[:, hs])).astype(BF16)
            else:
                vnb = vn.astype(BF16)
                for c in range(tm // CHUNK):
                    rs = slice(c * CHUNK, (c + 1) * CHUNK)
                    s = _dot(wsc_ref[h], vnb[rs, :]) + bsb_ref[:, hs]
                    ya_ref[rs, hs] = (u[rs, :] * s).astype(BF16)

        gateb_ref[:, cs] = _sigmoid(gb_raw)

    outa_ref[...] = _sigmoid(gate_a_raw) * _dot(ya_ref[...], pa_ref[...])


def _const_spec(shape, grid_rank):
    zeros = (0,) * len(shape)
    if grid_rank == 1:
        return pl.BlockSpec(shape, lambda i: zeros, pipeline_mode=pl.Buffered(1))
    return pl.BlockSpec(shape, lambda b, i: zeros, pipeline_mode=pl.Buffered(1))


def _front_prompt(x, w, tm):
    bsz, t_len, _ = x.shape
    nt = t_len // tm
    consts = [w["nw"], w["win"], w["lnw"], w["lnb"], w["wsc"], w["bsb"], w["mu"], w["wl"], w["wb"],
              w["al"], w["ab"], w["gl"], w["kk"], w["ka"], w["rk"], w["pa"], w["ones2"]]
    tok = lambda: pl.BlockSpec((None, tm, D_MODEL), lambda b, i: (b, i, 0))
    pair = lambda: pl.BlockSpec((None, N_PAIRS, tm, 128), lambda b, i: (b, 0, i, 0))
    tok_shape = jax.ShapeDtypeStruct((bsz, t_len, D_MODEL), F32)
    pair_shape = jax.ShapeDtypeStruct((bsz, N_PAIRS, t_len, 128), F32)
    return pl.pallas_call(
        functools.partial(_front_kernel, False, tm),
        grid=(bsz, nt),
        in_specs=[tok()] + [_const_spec(c.shape, 2) for c in consts],
        out_specs=[tok(), tok(), tok(), tok()] + [pair() for _ in range(6)]
        + [pl.BlockSpec((None, 1, SHIFT_WIDTH), lambda b, i: (b, 0, 0))],
        out_shape=[tok_shape] * 4 + [pair_shape] * 6 + [jax.ShapeDtypeStruct((bsz, 1, SHIFT_WIDTH), F32)],
        scratch_shapes=[pltpu.VMEM((tm, D_MODEL), BF16), pltpu.VMEM((8, SHIFT_WIDTH), F32)],
        compiler_params=pltpu.CompilerParams(dimension_semantics=("parallel", "arbitrary"),
                                             vmem_limit_bytes=VMEM_LIMIT),
        name="front_prompt",
    )(x, *consts)


def _front_sample(x, prev, w):
    n = x.shape[0]
    consts = [w["nw"], w["win"], w["lnw"], w["lnb"], w["ws0"], w["bs0"], w["mu"], w["wl"], w["wb"],
              w["al"], w["ab"], w["gl"], w["kk"], w["ka"], w["rk"], w["pa"], w["ones2"]]
    tok = lambda: pl.BlockSpec((n, D_MODEL), lambda i: (0, 0))
    wide = lambda: pl.BlockSpec((n, SHIFT_WIDTH), lambda i: (0, 0))
    tok_shape = jax.ShapeDtypeStruct((n, D_MODEL), F32)
    return pl.pallas_call(
        functools.partial(_front_kernel, True, n),
        grid=(1,),
        in_specs=[tok(), wide()] + [_const_spec(c.shape, 1) for c in consts],
        out_specs=[tok() for _ in range(10)] + [wide(), tok()],
        out_shape=[tok_shape] * 10 + [jax.ShapeDtypeStruct((n, SHIFT_WIDTH), F32), tok_shape],
        scratch_shapes=[pltpu.VMEM((n, D_MODEL), BF16)],
        compiler_params=pltpu.CompilerParams(dimension_semantics=("arbitrary",),
                                             vmem_limit_bytes=VMEM_LIMIT),
        name="front_sample",
    )(x, prev, *consts)


def _split3(x):
    hi = x.astype(BF16)
    r1 = x - hi.astype(F32)
    mid = r1.astype(BF16)
    lo = (r1 - mid.astype(F32)).astype(BF16)
    return hi, mid, lo


def _scan_kernel(tt, npp, r_ref, lw_ref, k_ref, v_ref, kn_ref, bs_ref, y_ref, sout_ref, st_ref):
    c_len = SCAN_CHUNK
    i = pl.program_id(2)

    @pl.when(i == 0)
    def _():
        st_ref[...] = jnp.zeros_like(st_ref)

    lane = lax.broadcasted_iota(jnp.int32, (c_len, 128), 1)
    head0 = lane < B_HEAD_DIM
    m = 2 * c_len
    tpos = jnp.bitwise_and(lax.broadcasted_iota(jnp.int32, (m, m), 0), c_len - 1)
    spos = jnp.bitwise_and(lax.broadcasted_iota(jnp.int32, (m, m), 1), c_len - 1)
    strict = tpos > spos
    incl = tpos >= spos
    eye = lax.broadcasted_iota(jnp.int32, (128, 128), 0) == lax.broadcasted_iota(jnp.int32, (128, 128), 1)
    rc = lax.broadcasted_iota(jnp.int32, (c_len, 3 * c_len), 0)
    cc = jnp.bitwise_and(lax.broadcasted_iota(jnp.int32, (c_len, 3 * c_len), 1), c_len - 1)
    tri3 = jnp.where(rc >= cc, 1.0, 0.0).astype(BF16)

    def pair_diag(z):
        return jnp.concatenate([jnp.where(head0, z, 0.0), jnp.where(head0, 0.0, z)], axis=0).astype(BF16)

    def bdot(a, b):
        return _dot(a.astype(BF16), b.astype(BF16))

    n_chunks = tt // c_len
    insts = [(c, q) for c in range(n_chunks) for q in range(npp)]
    rows = lambda c: slice(c * c_len, (c + 1) * c_len)

    log_p = {}
    for c, q in insts:
        log_p[c, q] = _dot(tri3, jnp.concatenate(_split3(lw_ref[q, rows(c), :]), axis=0))

    ops, gram = {}, {}
    for c, q in insts:
        lp = log_p[c, q]
        kn = kn_ref[q, rows(c), :]
        bs = bs_ref[q, rows(c), :]
        kx = k_ref[q, rows(c), :]
        log_pc = lp[c_len - 1:c_len, :]
        e_inv = jnp.exp(-lp)
        e_dec = jnp.exp(log_pc - lp)
        a_m = pair_diag(-kn * jnp.exp(lp - lw_ref[q, rows(c), :]))
        r_m = pair_diag(r_ref[q, rows(c), :] * jnp.exp(lp))
        v_m = pair_diag(v_ref[q, rows(c), :])
        btkt = jnp.concatenate([pair_diag(bs * e_dec), pair_diag(kx * e_dec)], axis=0)
        pc_col = jnp.sum(jnp.where(eye, jnp.exp(log_pc), 0.0), axis=1, keepdims=True)
        ops[c, q] = (a_m, r_m, v_m, btkt, pc_col)
        gram[c, q] = lax.dot_general(jnp.concatenate([a_m, r_m], axis=0),
                                     jnp.concatenate([pair_diag(bs * e_inv), pair_diag(kx * e_inv)], axis=0),
                                     (((1,), (1,)), ((), ())), preferred_element_type=F32)

    n_sum, pw, l_r, lakv = {}, {}, {}, {}
    for c, q in insts:
        g = gram[c, q]
        l_ab = jnp.where(strict, g[0:m, 0:m], 0.0)
        n_sum[c, q] = l_ab
        pw[c, q] = l_ab.astype(BF16)
        l_r[c, q] = jnp.concatenate([jnp.where(incl, g[m:2 * m, 0:m], 0.0),
                                     jnp.where(incl, g[m:2 * m, m:2 * m], 0.0)], axis=1).astype(BF16)
        lakv[c, q] = _dot(jnp.where(strict, g[0:m, m:2 * m], 0.0).astype(BF16), ops[c, q][2])

    pw_f = {}
    for c, q in insts:
        pw_f[c, q] = _dot(pw[c, q], pw[c, q])
        pw[c, q] = pw_f[c, q].astype(BF16)
    for _ in range(int(math.log2(c_len)) - 2):
        prod = {}
        for c, q in insts:
            prod[c, q] = _dot(pw[c, q], jnp.concatenate([pw[c, q], n_sum[c, q].astype(BF16)], axis=1))
        for c, q in insts:
            n_sum[c, q] = n_sum[c, q] + pw_f[c, q] + prod[c, q][:, m:2 * m]
            pw_f[c, q] = prod[c, q][:, 0:m]
            pw[c, q] = pw_f[c, q].astype(BF16)
    for c, q in insts:
        n_sum[c, q] = n_sum[c, q] + pw_f[c, q] + bdot(pw[c, q], n_sum[c, q])

    wu = {}
    for c, q in insts:
        rhs = jnp.concatenate([ops[c, q][0].astype(F32), lakv[c, q]], axis=1)
        wu[c, q] = rhs + bdot(n_sum[c, q], rhs)

    for c in range(n_chunks):
        xs, st = {}, {}
        for q in range(npp):
            st[q] = st_ref[q]
            xs[q] = bdot(jnp.concatenate([wu[c, q][:, 0:128].astype(BF16), ops[c, q][1]], axis=0), st[q])
        for q in range(npp):
            _, _, v_m, btkt, pc_col = ops[c, q]
            uv = jnp.concatenate([(xs[q][0:m, :] + wu[c, q][:, 128:256]).astype(BF16), v_m], axis=0)
            y_m = xs[q][m:2 * m, :] + _dot(l_r[c, q], uv)
            y_ref[q, rows(c), :] = y_m[0:c_len, :] + y_m[c_len:m, :]
            st_ref[q] = pc_col * st[q] + lax.dot_general(btkt, uv, (((0,), (0,)), ((), ())),
                                                         preferred_element_type=F32)

    @pl.when(i == pl.num_programs(2) - 1)
    def _():
        n = B_HEAD_DIM
        for q in range(npp):
            full_t = st_ref[q].T
            sout_ref[2 * q] = full_t[0:n, 0:n]
            sout_ref[2 * q + 1] = full_t[n:128, n:128]


def _scan_prompt(r, lw, k, v, kn, bs, tt, npp):
    bsz, n_pairs, t_len, _ = r.shape
    blk = lambda: pl.BlockSpec((None, npp, tt, 128), lambda b, p, i: (b, p, i, 0))
    return pl.pallas_call(
        functools.partial(_scan_kernel, tt, npp),
        grid=(bsz, n_pairs // npp, t_len // tt),
        in_specs=[blk() for _ in range(6)],
        out_specs=[blk(), pl.BlockSpec((None, 2 * npp, B_HEAD_DIM, B_HEAD_DIM), lambda b, p, i: (b, p, 0, 0))],
        out_shape=[jax.ShapeDtypeStruct((bsz, n_pairs, t_len, 128), F32),
                   jax.ShapeDtypeStruct((bsz, 2 * n_pairs, B_HEAD_DIM, B_HEAD_DIM), F32)],
        scratch_shapes=[pltpu.VMEM((npp, 128, 128), F32)],
        compiler_params=pltpu.CompilerParams(dimension_semantics=("parallel", "parallel", "arbitrary"),
                                             vmem_limit_bytes=VMEM_LIMIT),
        name="scan_prompt",
    )(r, lw, k, v, kn, bs)


def _step_kernel(tb, r_ref, lw_ref, k_ref, v_ref, kn_ref, bs_ref, s_ref, y_ref, sout_ref):
    n = B_HEAD_DIM
    row = lax.broadcasted_iota(jnp.int32, (128, 128), 0)
    col = lax.broadcasted_iota(jnp.int32, (128, 128), 1)
    same_head = (row < n) == (col < n)
    eye = jnp.where(row == col, 1.0, 0.0).astype(BF16)
    zero_f = jnp.zeros((n, n), F32)
    zero_b = jnp.zeros((128, 128), BF16)
    tn = (((1,), (1,)), ((), ()))
    insts = [(t, p) for t in range(tb) for p in range(N_PAIRS)]
    ps = lambda p: slice(p * 128, (p + 1) * 128)
    rep = lambda z: jnp.broadcast_to(z, (128, 128))

    s0, sav = {}, {}
    for t, p in insts:
        s0[t, p] = jnp.concatenate([jnp.concatenate([s_ref[t, 2 * p], zero_f], axis=1),
                                    jnp.concatenate([zero_f, s_ref[t, 2 * p + 1]], axis=1)], axis=0)
        v_row = v_ref[t, :, ps(p)]
        v_hi = v_row.astype(BF16)
        v_lo = (v_row - v_hi.astype(F32)).astype(BF16)
        lhs = jnp.concatenate([s0[t, p].astype(BF16), eye, eye], axis=1)
        rhs = jnp.concatenate(
            [jnp.concatenate([rep((-kn_ref[t, :, ps(p)]).astype(BF16)), zero_b, zero_b], axis=1),
             jnp.concatenate([zero_b, rep(v_hi), rep(v_lo)], axis=1)], axis=0)
        sav[t, p] = lax.dot_general(lhs, rhs, tn, preferred_element_type=F32)
    s1 = {}
    for t, p in insts:
        upd = (s0[t, p] * jnp.exp(lw_ref[t, :, ps(p)]) + sav[t, p][:, 0:128] * bs_ref[t, :, ps(p)]
               + sav[t, p][:, 128:256] * k_ref[t, :, ps(p)])
        upd = jnp.where(same_head, upd, 0.0)
        sout_ref[t, 2 * p] = upd[0:n, 0:n]
        sout_ref[t, 2 * p + 1] = upd[n:128, n:128]
        s1[t, p] = upd.astype(BF16)
    for t, p in insts:
        r8 = jnp.broadcast_to(r_ref[t, :, ps(p)].astype(BF16), (8, 128))
        y_ref[t, :, ps(p)] = lax.dot_general(r8, s1[t, p], tn, preferred_element_type=F32)[0:1, :]


def _scan_sample(r, lw, k, v, kn, bs, s0, tb):
    n = r.shape[0]
    vecs = [z.reshape(n, 1, D_MODEL) for z in (r, lw, k, v, kn, bs)]
    vec = lambda: pl.BlockSpec((tb, 1, D_MODEL), lambda i: (i, 0, 0))
    st = lambda: pl.BlockSpec((None, tb, B_HEADS, B_HEAD_DIM, B_HEAD_DIM), lambda i: (0, i, 0, 0, 0))
    y, s1 = pl.pallas_call(
        functools.partial(_step_kernel, tb),
        grid=(n // tb,),
        in_specs=[vec() for _ in range(6)] + [st()],
        out_specs=[vec(), st()],
        out_shape=[jax.ShapeDtypeStruct((n, 1, D_MODEL), F32), jax.ShapeDtypeStruct(s0.shape, F32)],
        compiler_params=pltpu.CompilerParams(dimension_semantics=("parallel",)),
        name="scan_sample",
    )(*vecs, s0)
    return y.reshape(n, D_MODEL), s1


def _back_kernel(paired, y_ref, bonus_ref, g_ref, gateb_ref, outa_ref, x_ref, lxw_ref, lxb_ref, pb_ref,
                 wo_ref, nfw_ref, rw_ref, rb_ref, ones2_ref, x1_ref, tok_ref, logt_ref):
    tm = x_ref.shape[0]
    n_parts = max(1, tm // BACK_ROWS)
    ones2 = ones2_ref[...]
    inv_n = 1.0 / B_HEAD_DIM
    parts = range(n_parts)
    rows = lambda i: slice(i * (tm // n_parts), (i + 1) * (tm // n_parts))

    if paired:
        y = [jnp.concatenate([y_ref[p, rows(i), :] for p in range(N_PAIRS)], axis=1) for i in parts]
    else:
        y = [y_ref[rows(i), :] for i in parts]
    mean = [_head_sums(y[i], ones2) * inv_n for i in parts]
    yc = [y[i] - mean[i] for i in parts]
    var = [_head_sums(yc[i] * yc[i], ones2) * inv_n for i in parts]
    yb = [((yc[i] * lax.rsqrt(var[i] + GN_EPS) * lxw_ref[...] + lxb_ref[...] + bonus_ref[rows(i), :])
           * g_ref[rows(i), :]).astype(BF16) for i in parts]
    mb = [_dot(yb[i], pb_ref[...]) for i in parts]
    merged = [(outa_ref[rows(i), :] + gateb_ref[rows(i), :] * mb[i]).astype(BF16) for i in parts]
    out = [_dot(merged[i], wo_ref[...]) for i in parts]
    tok = []
    for i in parts:
        x1 = x_ref[rows(i), :] + out[i]
        x1_ref[rows(i), :] = x1
        tok.append(_rms_norm(x1, nfw_ref[...]))
        tok_ref[rows(i), :] = tok[i].astype(BF16)
    for i in parts:
        t_hi = tok[i].astype(BF16)
        t_lo = (tok[i] - t_hi.astype(F32)).astype(BF16)
        logt_ref[:, rows(i)] = lax.dot_general(
            rw_ref[...], jnp.concatenate([t_hi, t_hi, t_lo], axis=1), (((1,), (1,)), ((), ())),
            preferred_element_type=F32) + rb_ref[...]


def _back(paired, y, bonus, g, gateb, outa, x, w, tm):
    n = x.shape[0]
    consts = [w["lxw"], w["lxb"], w["pb"], w["wo"], w["nfw"], w["rw3"], w["rbt"], w["ones2"]]
    tok = lambda: pl.BlockSpec((tm, D_MODEL), lambda i: (i, 0))
    if paired:
        nt = y.shape[2] // tm
        y_spec = pl.BlockSpec((None, N_PAIRS, tm, 128), lambda i: (i // nt, 0, i % nt, 0))
    else:
        y_spec = tok()
    return pl.pallas_call(
        functools.partial(_back_kernel, paired),
        grid=(n // tm,),
        in_specs=[y_spec] + [tok() for _ in range(5)] + [_const_spec(c.shape, 1) for c in consts],
        out_specs=[tok(), tok(), pl.BlockSpec((ROUTER_ROWS, tm), lambda i: (0, i))],
        out_shape=[jax.ShapeDtypeStruct((n, D_MODEL), F32), jax.ShapeDtypeStruct((n, D_MODEL), BF16),
                   jax.ShapeDtypeStruct((ROUTER_ROWS, n), F32)],
        compiler_params=pltpu.CompilerParams(dimension_semantics=("parallel",), vmem_limit_bytes=VMEM_LIMIT),
        name="back_prompt" if paired else "back_sample",
    )(y, bonus, g, gateb, outa, x, *consts)


def _moe_slots(ts):
    n = 2 * ts + N_EXPERTS * (SEG_ALIGN - 1)
    return -(-n // MOE_WINDOW) * MOE_WINDOW


def _split3_f32(x):
    hi = x.astype(BF16).astype(F32)
    mid = (x - hi).astype(BF16).astype(F32)
    lo = (x - hi - mid).astype(BF16).astype(F32)
    return hi, mid, lo


def _moe_kernel(ts, ns, tok_ref, logt_ref, x1_ref, utri_ref, wg_ref, wu_ref, wd_ref, fw_ref, o_ref,
                xy_ref, pg_ref, ws_ref, meta_ref):
    n_slots = 2 * xy_ref.shape[1]
    half_w = MOE_WINDOW // 2
    n_esteps = N_EXPERTS // MOE_EXPERTS_PER_STEP
    step = pl.program_id(1)
    tn = (((1,), (1,)), ((), ()))

    @pl.when(step < ns)
    def _route_and_sort():
        lt = logt_ref[...]
        neg = jnp.float32(-jnp.inf)
        big = jnp.float32(99.0)
        row8 = lax.broadcasted_iota(jnp.int32, (8, ts), 0).astype(F32)
        is_grp = row8 < N_GROUPS
        lg = jnp.where(is_grp, lt[0:8, :], neg)
        gmax = jnp.max(lg, axis=0, keepdims=True)
        grp = jnp.min(jnp.where(lg == gmax, row8, big), axis=0, keepdims=True)
        p_grp = 1.0 / jnp.sum(jnp.where(is_grp, jnp.exp(lg - gmax), 0.0), axis=0, keepdims=True)
        le = lt[EXPERT_ROW0:EXPERT_ROW0 + EXPERTS_PER_GROUP, :]
        for g in range(1, N_GROUPS):
            lo = EXPERT_ROW0 + g * EXPERTS_PER_GROUP
            le = jnp.where(grp == g, lt[lo:lo + EXPERTS_PER_GROUP, :], le)
        top1 = jnp.max(le, axis=0, keepdims=True)
        i1 = jnp.min(jnp.where(le == top1, row8, big), axis=0, keepdims=True)
        le2 = jnp.where(row8 == i1, neg, le)
        top2 = jnp.max(le2, axis=0, keepdims=True)
        i2 = jnp.min(jnp.where(le2 == top2, row8, big), axis=0, keepdims=True)
        e2 = jnp.exp(top2 - top1)
        w1 = p_grp / (1.0 + e2)
        w2 = p_grp * e2 / (1.0 + e2)

        row_e = lax.broadcasted_iota(jnp.int32, (N_EXPERTS, ts), 0).astype(F32)
        a1 = row_e == grp * EXPERTS_PER_GROUP + i1
        a2 = row_e == grp * EXPERTS_PER_GROUP + i2
        at = jnp.where(a1, 1.0, 0.0) + jnp.where(a2, 1.0, 0.0)
        rank = _dot(at.astype(BF16), utri_ref[...])
        cnt = rank[:, ts - 1:ts] + at[:, ts - 1:ts]
        cntp = jnp.floor((cnt + (SEG_ALIGN - 1)) * (1.0 / SEG_ALIGN)) * SEG_ALIGN
        cntp_b = jnp.broadcast_to(cntp, (N_EXPERTS, 128))
        r_e = lax.broadcasted_iota(jnp.int32, (N_EXPERTS, N_EXPERTS), 0)
        c_e = lax.broadcasted_iota(jnp.int32, (N_EXPERTS, N_EXPERTS), 1)
        off = _dot(jnp.where(r_e > c_e, 1.0, 0.0).astype(BF16), cntp_b.astype(BF16))
        slot = off[:, 0:1] + rank
        s1 = jnp.sum(jnp.where(a1, slot, 0.0), axis=0, keepdims=True)
        s2 = jnp.sum(jnp.where(a2, slot, 0.0), axis=0, keepdims=True)
        srow = lax.broadcasted_iota(jnp.int32, (n_slots, ts), 0).astype(F32)
        pg1 = jnp.where(srow == s1, 1.0, 0.0)
        pg2 = jnp.where(srow == s2, 1.0, 0.0)
        pg = (pg1 + pg2).astype(BF16)
        pg_ref[step] = pg
        xy_ref[step] = pltpu.bitcast(_dot(pg, tok_ref[...]).astype(BF16), jnp.int32)

        wrows = jnp.concatenate(
            [jnp.concatenate([p1, p2], axis=1) for p1, p2 in zip(_split3_f32(w1), _split3_f32(w2))]
            + [jnp.zeros((5, 2 * ts), F32)], axis=0).astype(BF16)
        wsl = lax.dot_general(jnp.concatenate([pg1.astype(BF16), pg2.astype(BF16)], axis=1), wrows, tn,
                              preferred_element_type=F32)
        ws_ref[step] = jnp.broadcast_to(wsl[:, 0:1] + wsl[:, 1:2] + wsl[:, 2:3], (n_slots, 128))
        meta_ref[step, 0:N_EXPERTS, :] = off.astype(jnp.int32)
        meta_ref[step, N_EXPERTS:2 * N_EXPERTS, :] = cntp_b.astype(jnp.int32)

    @pl.when((step >= ns) & (step < ns + n_esteps))
    def _experts():
        tile_row = (lax.broadcasted_iota(jnp.int32, (half_w, D_MODEL), 0) // 8) * SEG_ALIGN
        experts = range(MOE_EXPERTS_PER_STEP)
        offs, cnts = {}, {}
        for el in experts:
            e = (step - ns) * MOE_EXPERTS_PER_STEP + el
            offs[el] = [meta_ref[j, pl.ds(e, 1), :][0, 0] for j in range(ns)]
            cnts[el] = [meta_ref[j, pl.ds(N_EXPERTS + e, 1), :][0, 0] for j in range(ns)]
        n_win = functools.reduce(jnp.maximum, [(c + MOE_WINDOW - 1) // MOE_WINDOW
                                               for el in experts for c in cnts[el]])

        def window(w, carry):
            starts, lhs = {}, {}
            for el in experts:
                for j in range(ns):
                    st = jnp.minimum(offs[el][j] + w * MOE_WINDOW, n_slots - MOE_WINDOW)
                    starts[el, j] = pl.multiple_of(st, SEG_ALIGN)
                lhs[el] = jnp.concatenate(
                    [pltpu.bitcast(xy_ref[j, pl.ds(pl.multiple_of(starts[el, j] // 2, 8), half_w), :], BF16)
                     for j in range(ns)], axis=0)
            gate = {el: _dot(lhs[el], wg_ref[el]) for el in experts}
            up = {el: _dot(lhs[el], wu_ref[el]) for el in experts}
            y = {el: _dot((gate[el] * _sigmoid(gate[el]) * up[el]).astype(BF16), wd_ref[el]) for el in experts}
            for el in experts:
                for j in range(ns):
                    rows = starts[el, j] + tile_row
                    own = (rows >= offs[el][j] + w * MOE_WINDOW) & (rows < offs[el][j] + cnts[el][j])
                    yw = (y[el][j * MOE_WINDOW:(j + 1) * MOE_WINDOW, :]
                          * ws_ref[j, pl.ds(starts[el, j], MOE_WINDOW), 0:1])
                    pltpu.store(xy_ref.at[j, pl.ds(pl.multiple_of(starts[el, j] // 2, 8), half_w), :],
                                pltpu.bitcast(yw.astype(BF16), jnp.int32), mask=own)
            return carry

        lax.fori_loop(0, n_win, window, 0)

    @pl.when(step >= ns + n_esteps)
    def _combine():
        j = step - ns - n_esteps
        c = lax.dot_general(pg_ref[j], pltpu.bitcast(xy_ref[j], BF16), (((0,), (0,)), ((), ())),
                            preferred_element_type=F32)
        o_ref[...] = _rms_norm(x1_ref[...] + c, fw_ref[...])


def _moe(tok, logt, x1, w, ts, ns):
    n = tok.shape[0]
    n_slots = _moe_slots(ts)
    n_esteps = N_EXPERTS // MOE_EXPERTS_PER_STEP
    ne = MOE_EXPERTS_PER_STEP
    sub_in = lambda s, k: s * ns + jnp.clip(k, 0, ns - 1)
    sub_out = lambda s, k: s * ns + jnp.clip(k - ns - n_esteps, 0, ns - 1)
    expert = lambda k: jnp.clip(k - ns, 0, n_esteps - 1)
    utri = (jnp.arange(ts)[:, None] < jnp.arange(ts)[None, :]).astype(BF16)
    return pl.pallas_call(
        functools.partial(_moe_kernel, ts, ns),
        grid=(n // (ts * ns), 2 * ns + n_esteps),
        in_specs=[pl.BlockSpec((ts, D_MODEL), lambda s, k: (sub_in(s, k), 0)),
                  pl.BlockSpec((ROUTER_ROWS, ts), lambda s, k: (0, sub_in(s, k))),
                  pl.BlockSpec((ts, D_MODEL), lambda s, k: (sub_out(s, k), 0)),
                  pl.BlockSpec((ts, ts), lambda s, k: (0, 0)),
                  pl.BlockSpec((ne, D_MODEL, D_EXPERT), lambda s, k: (expert(k), 0, 0)),
                  pl.BlockSpec((ne, D_MODEL, D_EXPERT), lambda s, k: (expert(k), 0, 0)),
                  pl.BlockSpec((ne, D_EXPERT, D_MODEL), lambda s, k: (expert(k), 0, 0)),
                  pl.BlockSpec((1, D_MODEL), lambda s, k: (0, 0))],
        out_specs=pl.BlockSpec((ts, D_MODEL), lambda s, k: (sub_out(s, k), 0)),
        out_shape=jax.ShapeDtypeStruct((n, D_MODEL), F32),
        scratch_shapes=[pltpu.VMEM((ns, n_slots // 2, D_MODEL), jnp.int32), pltpu.VMEM((ns, n_slots, ts), BF16),
                        pltpu.VMEM((ns, n_slots, 128), F32), pltpu.VMEM((ns, 2 * N_EXPERTS, 128), jnp.int32)],
        compiler_params=pltpu.CompilerParams(dimension_semantics=("parallel", "arbitrary"),
                                             vmem_limit_bytes=VMEM_LIMIT),
        name="moe",
    )(tok, logt, x1, utri, w["wg"], w["wu"], w["wd"], w["fw"])


def _prep_weights(norm_mix_w, w_in, sgu_ln_w, sgu_ln_b, sgu_w_s, sgu_b, rwkv_mu, w_lora_up, w_bias,
                  a_lora_up, a_bias, g_lora_up, k_k, k_a, r_k, lnx_w, lnx_b, proj_a, proj_b, w_out,
                  norm_ffn_w, router_group, router_group_bias, router_expert, router_expert_bias,
                  moe_w_gate, moe_w_up, moe_w_down, norm_final_w):
    row = lambda z: z.reshape(1, -1).astype(F32)
    causal = jnp.tril(jnp.ones((CHUNK, CHUNK), dtype=bool))
    zeros_lora = jnp.zeros((64, D_MODEL), F32)
    head = jnp.arange(128) // B_HEAD_DIM
    ones_bd = (head[:, None] == head[None, :]).astype(BF16)
    pad = EXPERT_ROW0 - N_GROUPS
    rw = jnp.transpose(jnp.concatenate([router_group, jnp.zeros((D_MODEL, pad), F32), router_expert],
                                       axis=1))
    rw_hi = rw.astype(BF16)
    return {
        "nw": row(norm_mix_w), "win": w_in.astype(BF16), "lnw": row(sgu_ln_w), "lnb": row(sgu_ln_b),
        "wsc": jnp.where(causal[None], sgu_w_s, 0.0).astype(BF16),
        "bsb": jnp.repeat(jnp.transpose(sgu_b), A_HEAD_DIM, axis=1),
        "ws0": jnp.repeat(sgu_w_s[:, 0, 0], A_HEAD_DIM).reshape(1, -1),
        "bs0": jnp.repeat(sgu_b[:, 0], A_HEAD_DIM).reshape(1, -1),
        "mu": row(rwkv_mu),
        "wl": jnp.concatenate([w_lora_up, zeros_lora], axis=0).astype(BF16), "wb": row(w_bias),
        "al": jnp.concatenate([zeros_lora, a_lora_up], axis=0).astype(BF16), "ab": row(a_bias),
        "gl": g_lora_up.astype(BF16), "kk": row(k_k), "ka": row(k_a), "rk": row(r_k),
        "pa": proj_a.astype(BF16), "ones2": jnp.concatenate([ones_bd, ones_bd], axis=0),
        "lxw": row(lnx_w), "lxb": row(lnx_b), "pb": proj_b.astype(BF16), "wo": w_out.astype(BF16),
        "nfw": row(norm_ffn_w),
        "rw3": jnp.concatenate([rw_hi, (rw - rw_hi.astype(F32)).astype(BF16), rw_hi], axis=1),
        "rbt": jnp.concatenate([router_group_bias, jnp.zeros((pad,), F32), router_expert_bias]).reshape(-1, 1),
        "wg": moe_w_gate.astype(BF16), "wu": moe_w_up.astype(BF16), "wd": moe_w_down.astype(BF16),
        "fw": row(norm_final_w),
    }


def kernel(x_prompt, x_sample, state_wkv, state_shift, norm_mix_w, w_in, sgu_ln_w, sgu_ln_b, sgu_w_s, sgu_b, rwkv_mu, w_lora_up, w_bias, a_lora_up, a_bias, g_lora_up, k_k, k_a, r_k, lnx_w, lnx_b, proj_a, proj_b, w_out, norm_ffn_w, router_group, router_group_bias, router_expert, router_expert_bias, moe_w_gate, moe_w_up, moe_w_down, norm_final_w):
    layer = [z[0] for z in (norm_mix_w, w_in, sgu_ln_w, sgu_ln_b, sgu_w_s, sgu_b, rwkv_mu, w_lora_up, w_bias,
                            a_lora_up, a_bias, g_lora_up, k_k, k_a, r_k, lnx_w, lnx_b, proj_a, proj_b, w_out,
                            norm_ffn_w, router_group, router_group_bias, router_expert, router_expert_bias,
                            moe_w_gate, moe_w_up, moe_w_down)]
    w = _prep_weights(*layer, norm_final_w)
    bsz, t_len, _ = x_prompt.shape
    n_s = x_sample.shape[0]

    (outa, gateb, g, bonus, r, lw, k, v, kn, bs, last) = _front_prompt(x_prompt, w, 256)
    y, wkv_p = _scan_prompt(r, lw, k, v, kn, bs, 128, N_PAIRS)
    flat = lambda z: z.reshape(bsz * t_len, D_MODEL)
    x1, tok, logt = _back(True, y, flat(bonus), flat(g), flat(gateb), flat(outa), flat(x_prompt), w, 512)
    y_prompt = _moe(tok, logt, x1, w, 512, 4).reshape(bsz, t_len, D_MODEL)

    xs2 = x_sample.reshape(n_s, D_MODEL)
    (outa, gateb, g, bonus, r, lw, k, v, kn, bs, cols_s, vn_s) = _front_sample(xs2, state_shift[0], w)
    y, wkv_s = _scan_sample(r, lw, k, v, kn, bs, state_wkv, 4)
    x1, tok, logt = _back(False, y, bonus, g, gateb, outa, xs2, w, n_s)
    y_sample = _moe(tok, logt, x1, w, n_s, 1).reshape(n_s, 1, D_MODEL)

    return (y_prompt, y_sample, wkv_p[None], last.reshape(1, bsz, SHIFT_WIDTH), wkv_s, cols_s[None],
            vn_s.reshape(1, n_s, 1, A_HEADS, A_HEAD_DIM))
```

```python
import functools
import math

import jax
import jax.numpy as jnp
from jax import lax
from jax.experimental import pallas as pl
from jax.experimental.pallas import tpu as pltpu

F32 = jnp.float32
BF16 = jnp.bfloat16
HIGHEST = lax.Precision.HIGHEST

D_MODEL = 1024
CHUNK = 128
A_HEADS = 8
A_HEAD_DIM = 128
B_HEADS = 16
B_HEAD_DIM = 64
N_PAIRS = B_HEADS // 2
SHIFT_WIDTH = 3328
OFF_V = 1024
OFF_SHIFT = 2048
OFF_GATE_A = OFF_SHIFT + SHIFT_WIDTH
OFF_GATE_B = OFF_GATE_A + D_MODEL
IN_COLS = OFF_GATE_B + D_MODEL
N_GROUPS = 4
EXPERTS_PER_GROUP = 8
N_EXPERTS = 32
D_EXPERT = 256
ROUTER_ROWS = 40
EXPERT_ROW0 = 8
SEG_ALIGN = 16
BACK_ROWS = 256
MOE_WINDOW = 64
MOE_EXPERTS_PER_STEP = 4
RMS_EPS = 1e-6
LN_EPS = 1e-5
GN_EPS = 64e-5

SCAN_CHUNK = 64
FRONT_BLOCK = 256
VMEM_LIMIT = 56 * 1024 * 1024


def _gelu(x):
    return x * (0.5 * (1.0 + jnp.tanh(math.sqrt(2.0 / math.pi) * (x + 0.044715 * (x * x * x)))))


def _sigmoid(x):
    return 1.0 / (1.0 + jnp.exp(-x))


def _softplus(z):
    return jnp.maximum(z, 0.0) + jnp.log(1.0 + jnp.exp(-jnp.abs(z)))


def _rms_norm(x, g):
    return x * lax.rsqrt(jnp.mean(x * x, axis=-1, keepdims=True) + RMS_EPS) * g


def _dot(a, b):
    return jnp.dot(a, b, preferred_element_type=F32)


def _dot_hi(a, b):
    return jnp.dot(a, b, preferred_element_type=F32, precision=HIGHEST)


def _head_sums(z, ones2):
    outs = []
    for p in range(z.shape[1] // 128):
        zp = z[:, p * 128:(p + 1) * 128]
        hi = zp.astype(BF16)
        lo = (zp - hi.astype(F32)).astype(BF16)
        outs.append(_dot(jnp.concatenate([hi, lo], axis=1), ones2))
    return jnp.concatenate(outs, axis=1)


def _front_kernel(is_sample, tm, *refs):
    if is_sample:
        (x_ref, prev_ref, nw_ref, win_ref, lnw_ref, lnb_ref, ws0_ref, bs0_ref, mu_ref, wl_ref, wb_ref,
         al_ref, ab_ref, gl_ref, kk_ref, ka_ref, rk_ref, pa_ref, ones2_ref,
         outa_ref, gateb_ref, g_ref, bonus_ref, r_ref, lw_ref, k_ref, v_ref, kn_ref, bs_ref,
         cols_ref, vn_ref, ya_ref) = refs
    else:
        (x_ref, nw_ref, win_ref, lnw_ref, lnb_ref, wsc_ref, bsb_ref, mu_ref, wl_ref, wb_ref,
         al_ref, ab_ref, gl_ref, kk_ref, ka_ref, rk_ref, pa_ref, ones2_ref,
         outa_ref, gateb_ref, g_ref, bonus_ref, r_ref, lw_ref, k_ref, v_ref, kn_ref, bs_ref,
         last_ref, ya_ref, carry_ref) = refs

        @pl.when(pl.program_id(1) == 0)
        def _():
            carry_ref[...] = jnp.zeros_like(carry_ref)

    xb = _rms_norm(x_ref[...], nw_ref[...]).astype(BF16)
    ones2 = ones2_ref[...]

    def proj(lo, width):
        return _dot(xb, win_ref[:, lo:lo + width])

    def shifted(lo, width, cols=None):
        cs = slice(lo, lo + width)
        if cols is None:
            cols = proj(OFF_SHIFT + lo, width)
        if is_sample:
            prev = prev_ref[:, cs]
            cols_ref[:, cs] = cols
        else:
            row = lax.broadcasted_iota(jnp.int32, cols.shape, 0)
            prev = jnp.where(row == 0, carry_ref[0:1, cs], pltpu.roll(cols, 1, 0))
            carry_ref[0:1, cs] = cols[tm - 1:tm, :]
            last_ref[:, cs] = cols[tm - 1:tm, :]
        return cols + (prev - cols) * mu_ref[:, cs]

    tail = shifted(3 * D_MODEL, 256)
    wa = tail[:, 0:128]
    twa = jnp.tanh(wa).astype(BF16)
    wab = wa.astype(BF16)
    sgd = _sigmoid(tail[:, 128:256]).astype(BF16)

    def block_matmuls(j):
        lo = j * FRONT_BLOCK
        cs = slice(lo, lo + FRONT_BLOCK)
        return (proj(OFF_SHIFT + lo, FRONT_BLOCK), proj(OFF_SHIFT + D_MODEL + lo, FRONT_BLOCK),
                proj(OFF_SHIFT + 2 * D_MODEL + lo, FRONT_BLOCK), _dot(twa, wl_ref[:, cs]), _dot(wab, al_ref[:, cs]),
                _dot(sgd, gl_ref[:, cs]), proj(lo, FRONT_BLOCK), proj(OFF_V + lo, FRONT_BLOCK),
                proj(OFF_GATE_B + lo, FRONT_BLOCK))

    n_blocks = D_MODEL // FRONT_BLOCK
    raw = block_matmuls(0)
    gate_a_raw = None
    for j in range(n_blocks):
        lo = j * FRONT_BLOCK
        cs = slice(lo, lo + FRONT_BLOCK)
        r_raw, k_raw, v_raw, lw_dot, a_dot, g_dot, u_raw, va_raw, gb_raw = raw
        if j + 1 < n_blocks:
            raw = block_matmuls(j + 1)
        else:
            gate_a_raw = proj(OFF_GATE_A, D_MODEL)

        r = shifted(lo, FRONT_BLOCK, r_raw)
        k = shifted(D_MODEL + lo, FRONT_BLOCK, k_raw)
        vb = shifted(2 * D_MODEL + lo, FRONT_BLOCK, v_raw)
        w_log = -_softplus(-(wb_ref[:, cs] + lw_dot)) - 0.5
        logw = -jnp.exp(w_log)
        a = _sigmoid(ab_ref[:, cs] + a_dot)
        g_ref[:, cs] = g_dot
        kk = k * kk_ref[:, cs]
        kkn = kk / jnp.maximum(jnp.sqrt(_head_sums(kk * kk, ones2)), 1e-12)
        k2 = k * (1.0 + (a - 1.0) * ka_ref[:, cs])
        bonus_ref[:, cs] = _head_sums(r * k2 * rk_ref[:, cs], ones2) * vb
        bsc = kkn * a
        outs = ((r_ref, r), (lw_ref, logw), (k_ref, k2), (v_ref, vb), (kn_ref, kkn), (bs_ref, bsc))
        for o_ref, val in outs:
            if is_sample:
                o_ref[:, cs] = val
            else:
                for q in range(FRONT_BLOCK // 128):
                    o_ref[lo // 128 + q] = val[:, q * 128:(q + 1) * 128]

        u2 = _gelu(u_raw)
        v2 = _gelu(va_raw)
        for q in range(FRONT_BLOCK // A_HEAD_DIM):
            h = lo // A_HEAD_DIM + q
            hs = slice(h * A_HEAD_DIM, (h + 1) * A_HEAD_DIM)
            u = u2[:, q * A_HEAD_DIM:(q + 1) * A_HEAD_DIM]
            v = v2[:, q * A_HEAD_DIM:(q + 1) * A_HEAD_DIM]
            mean = jnp.mean(v, axis=-1, keepdims=True)
            vc = v - mean
            var = jnp.mean(vc * vc, axis=-1, keepdims=True)
            vn = vc * lax.rsqrt(var + LN_EPS) * lnw_ref[:, hs] + lnb_ref[:, hs]
            if is_sample:
                vn_ref[:, hs] = vn
                ya_ref[:, hs] = (u * (vn * ws0_ref[:, hs] + bs0_ref[:, hs])).astype(BF16)
            else:
                vnb = vn.astype(BF16)
                for c in range(tm // CHUNK):
                    rs = slice(c * CHUNK, (c + 1) * CHUNK)
                    s = _dot(wsc_ref[h], vnb[rs, :]) + bsb_ref[:, hs]
                    ya_ref[rs, hs] = (u[rs, :] * s).astype(BF16)

        gateb_ref[:, cs] = _sigmoid(gb_raw)

    outa_ref[...] = _sigmoid(gate_a_raw) * _dot(ya_ref[...], pa_ref[...])


def _const_spec(shape, grid_rank):
    zeros = (0,) * len(shape)
    if grid_rank == 1:
        return pl.BlockSpec(shape, lambda i: zeros, pipeline_mode=pl.Buffered(1))
    return pl.BlockSpec(shape, lambda b, i: zeros, pipeline_mode=pl.Buffered(1))


def _front_prompt(x, w, tm):
    bsz, t_len, _ = x.shape
    nt = t_len // tm
    consts = [w["nw"], w["win"], w["lnw"], w["lnb"], w["wsc"], w["bsb"], w["mu"], w["wl"], w["wb"],
              w["al"], w["ab"], w["gl"], w["kk"], w["ka"], w["rk"], w["pa"], w["ones2"]]
    tok = lambda: pl.BlockSpec((None, tm, D_MODEL), lambda b, i: (b, i, 0))
    pair = lambda: pl.BlockSpec((None, N_PAIRS, tm, 128), lambda b, i: (b, 0, i, 0))
    tok_shape = jax.ShapeDtypeStruct((bsz, t_len, D_MODEL), F32)
    pair_shape = jax.ShapeDtypeStruct((bsz, N_PAIRS, t_len, 128), F32)
    return pl.pallas_call(
        functools.partial(_front_kernel, False, tm),
        grid=(bsz, nt),
        in_specs=[tok()] + [_const_spec(c.shape, 2) for c in consts],
        out_specs=[tok(), tok(), tok(), tok()] + [pair() for _ in range(6)]
        + [pl.BlockSpec((None, 1, SHIFT_WIDTH), lambda b, i: (b, 0, 0))],
        out_shape=[tok_shape] * 4 + [pair_shape] * 6 + [jax.ShapeDtypeStruct((bsz, 1, SHIFT_WIDTH), F32)],
        scratch_shapes=[pltpu.VMEM((tm, D_MODEL), BF16), pltpu.VMEM((8, SHIFT_WIDTH), F32)],
        compiler_params=pltpu.CompilerParams(dimension_semantics=("parallel", "arbitrary"),
                                             vmem_limit_bytes=VMEM_LIMIT),
        name="front_prompt",
    )(x, *consts)


def _front_sample(x, prev, w):
    n = x.shape[0]
    consts = [w["nw"], w["win"], w["lnw"], w["lnb"], w["ws0"], w["bs0"], w["mu"], w["wl"], w["wb"],
              w["al"], w["ab"], w["gl"], w["kk"], w["ka"], w["rk"], w["pa"], w["ones2"]]
    tok = lambda: pl.BlockSpec((n, D_MODEL), lambda i: (0, 0))
    wide = lambda: pl.BlockSpec((n, SHIFT_WIDTH), lambda i: (0, 0))
    tok_shape = jax.ShapeDtypeStruct((n, D_MODEL), F32)
    return pl.pallas_call(
        functools.partial(_front_kernel, True, n),
        grid=(1,),
        in_specs=[tok(), wide()] + [_const_spec(c.shape, 1) for c in consts],
        out_specs=[tok() for _ in range(10)] + [wide(), tok()],
        out_shape=[tok_shape] * 10 + [jax.ShapeDtypeStruct((n, SHIFT_WIDTH), F32), tok_shape],
        scratch_shapes=[pltpu.VMEM((n, D_MODEL), BF16)],
        compiler_params=pltpu.CompilerParams(dimension_semantics=("arbitrary",),
                                             vmem_limit_bytes=VMEM_LIMIT),
        name="front_sample",
    )(x, prev, *consts)


def _split3(x):
    hi = x.astype(BF16)
    r1 = x - hi.astype(F32)
    mid = r1.astype(BF16)
    lo = (r1 - mid.astype(F32)).astype(BF16)
    return hi, mid, lo


def _scan_kernel(tt, npp, r_ref, lw_ref, k_ref, v_ref, kn_ref, bs_ref, y_ref, sout_ref, st_ref):
    c_len = SCAN_CHUNK
    i = pl.program_id(2)

    @pl.when(i == 0)
    def _():
        st_ref[...] = jnp.zeros_like(st_ref)

    lane = lax.broadcasted_iota(jnp.int32, (c_len, 128), 1)
    head0 = lane < B_HEAD_DIM
    m = 2 * c_len
    tpos = jnp.bitwise_and(lax.broadcasted_iota(jnp.int32, (m, m), 0), c_len - 1)
    spos = jnp.bitwise_and(lax.broadcasted_iota(jnp.int32, (m, m), 1), c_len - 1)
    strict = tpos > spos
    incl = tpos >= spos
    eye = lax.broadcasted_iota(jnp.int32, (128, 128), 0) == lax.broadcasted_iota(jnp.int32, (128, 128), 1)
    rc = lax.broadcasted_iota(jnp.int32, (c_len, 3 * c_len), 0)
    cc = jnp.bitwise_and(lax.broadcasted_iota(jnp.int32, (c_len, 3 * c_len), 1), c_len - 1)
    tri3 = jnp.where(rc >= cc, 1.0, 0.0).astype(BF16)

    def pair_diag(z):
        return jnp.concatenate([jnp.where(head0, z, 0.0), jnp.where(head0, 0.0, z)], axis=0).astype(BF16)

    def bdot(a, b):
        return _dot(a.astype(BF16), b.astype(BF16))

    n_chunks = tt // c_len
    insts = [(c, q) for c in range(n_chunks) for q in range(npp)]
    rows = lambda c: slice(c * c_len, (c + 1) * c_len)

    log_p = {}
    for c, q in insts:
        log_p[c, q] = _dot(tri3, jnp.concatenate(_split3(lw_ref[q, rows(c), :]), axis=0))

    ops, gram = {}, {}
    for c, q in insts:
        lp = log_p[c, q]
        kn = kn_ref[q, rows(c), :]
        bs = bs_ref[q, rows(c), :]
        kx = k_ref[q, rows(c), :]
        log_pc = lp[c_len - 1:c_len, :]
        e_inv = jnp.exp(-lp)
        e_dec = jnp.exp(log_pc - lp)
        a_m = pair_diag(-kn * jnp.exp(lp - lw_ref[q, rows(c), :]))
        r_m = pair_diag(r_ref[q, rows(c), :] * jnp.exp(lp))
        v_m = pair_diag(v_ref[q, rows(c), :])
        btkt = jnp.concatenate([pair_diag(bs * e_dec), pair_diag(kx * e_dec)], axis=0)
        pc_col = jnp.sum(jnp.where(eye, jnp.exp(log_pc), 0.0), axis=1, keepdims=True)
        ops[c, q] = (a_m, r_m, v_m, btkt, pc_col)
        gram[c, q] = lax.dot_general(jnp.concatenate([a_m, r_m], axis=0),
                                     jnp.concatenate([pair_diag(bs * e_inv), pair_diag(kx * e_inv)], axis=0),
                                     (((1,), (1,)), ((), ())), preferred_element_type=F32)

    n_sum, pw, l_r, lakv = {}, {}, {}, {}
    for c, q in insts:
        g = gram[c, q]
        l_ab = jnp.where(strict, g[0:m, 0:m], 0.0)
        n_sum[c, q] = l_ab
        pw[c, q] = l_ab.astype(BF16)
        l_r[c, q] = jnp.concatenate([jnp.where(incl, g[m:2 * m, 0:m], 0.0),
                                     jnp.where(incl, g[m:2 * m, m:2 * m], 0.0)], axis=1).astype(BF16)
        lakv[c, q] = _dot(jnp.where(strict, g[0:m, m:2 * m], 0.0).astype(BF16), ops[c, q][2])

    pw_f = {}
    for c, q in insts:
        pw_f[c, q] = _dot(pw[c, q], pw[c, q])
        pw[c, q] = pw_f[c, q].astype(BF16)
    for _ in range(int(math.log2(c_len)) - 2):
        prod = {}
        for c, q in insts:
            prod[c, q] = _dot(pw[c, q], jnp.concatenate([pw[c, q], n_sum[c, q].astype(BF16)], axis=1))
        for c, q in insts:
            n_sum[c, q] = n_sum[c, q] + pw_f[c, q] + prod[c, q][:, m:2 * m]
            pw_f[c, q] = prod[c, q][:, 0:m]
            pw[c, q] = pw_f[c, q].astype(BF16)
    for c, q in insts:
        n_sum[c, q] = n_sum[c, q] + pw_f[c, q] + bdot(pw[c, q], n_sum[c, q])

    wu = {}
    for c, q in insts:
        rhs = jnp.concatenate([ops[c, q][0].astype(F32), lakv[c, q]], axis=1)
        wu[c, q] = rhs + bdot(n_sum[c, q], rhs)

    for c in range(n_chunks):
        xs, st = {}, {}
        for q in range(npp):
            st[q] = st_ref[q]
            xs[q] = bdot(jnp.concatenate([wu[c, q][:, 0:128].astype(BF16), ops[c, q][1]], axis=0), st[q])
        for q in range(npp):
            _, _, v_m, btkt, pc_col = ops[c, q]
            uv = jnp.concatenate([(xs[q][0:m, :] + wu[c, q][:, 128:256]).astype(BF16), v_m], axis=0)
            y_m = xs[q][m:2 * m, :] + _dot(l_r[c, q], uv)
            y_ref[q, rows(c), :] = y_m[0:c_len, :] + y_m[c_len:m, :]
            st_ref[q] = pc_col * st[q] + lax.dot_general(btkt, uv, (((0,), (0,)), ((), ())),
                                                         preferred_element_type=F32)

    @pl.when(i == pl.num_programs(2) - 1)
    def _():
        n = B_HEAD_DIM
        for q in range(npp):
            full_t = st_ref[q].T
            sout_ref[2 * q] = full_t[0:n, 0:n]
            sout_ref[2 * q + 1] = full_t[n:128, n:128]


def _scan_prompt(r, lw, k, v, kn, bs, tt, npp):
    bsz, n_pairs, t_len, _ = r.shape
    blk = lambda: pl.BlockSpec((None, npp, tt, 128), lambda b, p, i: (b, p, i, 0))
    return pl.pallas_call(
        functools.partial(_scan_kernel, tt, npp),
        grid=(bsz, n_pairs // npp, t_len // tt),
        in_specs=[blk() for _ in range(6)],
        out_specs=[blk(), pl.BlockSpec((None, 2 * npp, B_HEAD_DIM, B_HEAD_DIM), lambda b, p, i: (b, p, 0, 0))],
        out_shape=[jax.ShapeDtypeStruct((bsz, n_pairs, t_len, 128), F32),
                   jax.ShapeDtypeStruct((bsz, 2 * n_pairs, B_HEAD_DIM, B_HEAD_DIM), F32)],
        scratch_shapes=[pltpu.VMEM((npp, 128, 128), F32)],
        compiler_params=pltpu.CompilerParams(dimension_semantics=("parallel", "parallel", "arbitrary"),
                                             vmem_limit_bytes=VMEM_LIMIT),
        name="scan_prompt",
    )(r, lw, k, v, kn, bs)


def _step_kernel(r_ref, lw_ref, k_ref, v_ref, kn_ref, bs_ref, s_ref, y_ref, sout_ref):
    a = -kn_ref[...]
    w = jnp.exp(lw_ref[...])
    kx = k_ref[...]
    r = r_ref[...]
    b = bs_ref[...]
    ys = []
    for i in range(B_HEAD_DIM):
        s0 = s_ref[i]
        sa = jnp.sum(s0 * a, axis=0, keepdims=True)
        s1 = s0 * w + sa * b + v_ref[i:i + 1, :] * kx
        sout_ref[i] = s1
        ys.append(jnp.sum(s1 * r, axis=0, keepdims=True))
    y_ref[...] = jnp.concatenate(ys, axis=0)


def _scan_sample(r, lw, k, v, kn, bs, s0):
    n = r.shape[0]
    vecs = [jnp.transpose(z) for z in (r, lw, k, v, kn, bs)]
    s0_t = jnp.transpose(s0[0], (1, 2, 3, 0))
    vec = lambda: pl.BlockSpec((B_HEAD_DIM, n), lambda h: (h, 0))
    st = lambda: pl.BlockSpec((None, B_HEAD_DIM, B_HEAD_DIM, n), lambda h: (h, 0, 0, 0))
    y_t, s1_t = pl.pallas_call(
        _step_kernel,
        grid=(B_HEADS,),
        in_specs=[vec() for _ in range(6)] + [st()],
        out_specs=[vec(), st()],
        out_shape=[jax.ShapeDtypeStruct((D_MODEL, n), F32), jax.ShapeDtypeStruct(s0_t.shape, F32)],
        compiler_params=pltpu.CompilerParams(dimension_semantics=("parallel",)),
        name="scan_sample",
    )(*vecs, s0_t)
    return jnp.transpose(y_t), jnp.transpose(s1_t, (3, 0, 1, 2))[None]


def _back_kernel(paired, y_ref, bonus_ref, g_ref, gateb_ref, outa_ref, x_ref, lxw_ref, lxb_ref, pb_ref,
                 wo_ref, nfw_ref, rw_ref, rb_ref, ones2_ref, x1_ref, tok_ref, logt_ref):
    tm = x_ref.shape[0]
    n_parts = max(1, tm // BACK_ROWS)
    ones2 = ones2_ref[...]
    inv_n = 1.0 / B_HEAD_DIM
    parts = range(n_parts)
    rows = lambda i: slice(i * (tm // n_parts), (i + 1) * (tm // n_parts))

    if paired:
        y = [jnp.concatenate([y_ref[p, rows(i), :] for p in range(N_PAIRS)], axis=1) for i in parts]
    else:
        y = [y_ref[rows(i), :] for i in parts]
    mean = [_head_sums(y[i], ones2) * inv_n for i in parts]
    yc = [y[i] - mean[i] for i in parts]
    var = [_head_sums(yc[i] * yc[i], ones2) * inv_n for i in parts]
    yb = [((yc[i] * lax.rsqrt(var[i] + GN_EPS) * lxw_ref[...] + lxb_ref[...] + bonus_ref[rows(i), :])
           * g_ref[rows(i), :]).astype(BF16) for i in parts]
    mb = [_dot(yb[i], pb_ref[...]) for i in parts]
    merged = [(outa_ref[rows(i), :] + gateb_ref[rows(i), :] * mb[i]).astype(BF16) for i in parts]
    out = [_dot(merged[i], wo_ref[...]) for i in parts]
    tok = []
    for i in parts:
        x1 = x_ref[rows(i), :] + out[i]
        x1_ref[rows(i), :] = x1
        tok.append(_rms_norm(x1, nfw_ref[...]))
        tok_ref[rows(i), :] = tok[i].astype(BF16)
    for i in parts:
        t_hi = tok[i].astype(BF16)
        t_lo = (tok[i] - t_hi.astype(F32)).astype(BF16)
        logt_ref[:, rows(i)] = lax.dot_general(
            rw_ref[...], jnp.concatenate([t_hi, t_hi, t_lo], axis=1), (((1,), (1,)), ((), ())),
            preferred_element_type=F32) + rb_ref[...]


def _back(paired, y, bonus, g, gateb, outa, x, w, tm):
    n = x.shape[0]
    consts = [w["lxw"], w["lxb"], w["pb"], w["wo"], w["nfw"], w["rw3"], w["rbt"], w["ones2"]]
    tok = lambda: pl.BlockSpec((tm, D_MODEL), lambda i: (i, 0))
    if paired:
        nt = y.shape[2] // tm
        y_spec = pl.BlockSpec((None, N_PAIRS, tm, 128), lambda i: (i // nt, 0, i % nt, 0))
    else:
        y_spec = tok()
    return pl.pallas_call(
        functools.partial(_back_kernel, paired),
        grid=(n // tm,),
        in_specs=[y_spec] + [tok() for _ in range(5)] + [_const_spec(c.shape, 1) for c in consts],
        out_specs=[tok(), tok(), pl.BlockSpec((ROUTER_ROWS, tm), lambda i: (0, i))],
        out_shape=[jax.ShapeDtypeStruct((n, D_MODEL), F32), jax.ShapeDtypeStruct((n, D_MODEL), BF16),
                   jax.ShapeDtypeStruct((ROUTER_ROWS, n), F32)],
        compiler_params=pltpu.CompilerParams(dimension_semantics=("parallel",), vmem_limit_bytes=VMEM_LIMIT),
        name="back_prompt" if paired else "back_sample",
    )(y, bonus, g, gateb, outa, x, *consts)


def _moe_slots(ts):
    n = 2 * ts + N_EXPERTS * (SEG_ALIGN - 1)
    return -(-n // MOE_WINDOW) * MOE_WINDOW


def _split3_f32(x):
    hi = x.astype(BF16).astype(F32)
    mid = (x - hi).astype(BF16).astype(F32)
    lo = (x - hi - mid).astype(BF16).astype(F32)
    return hi, mid, lo


def _moe_kernel(ts, ns, tok_ref, logt_ref, x1_ref, utri_ref, wg_ref, wu_ref, wd_ref, fw_ref, o_ref,
                xy_ref, pg_ref, ws_ref, meta_ref):
    n_slots = 2 * xy_ref.shape[1]
    half_w = MOE_WINDOW // 2
    n_esteps = N_EXPERTS // MOE_EXPERTS_PER_STEP
    step = pl.program_id(1)
    tn = (((1,), (1,)), ((), ()))

    @pl.when(step < ns)
    def _route_and_sort():
        lt = logt_ref[...]
        neg = jnp.float32(-jnp.inf)
        big = jnp.float32(99.0)
        row8 = lax.broadcasted_iota(jnp.int32, (8, ts), 0).astype(F32)
        is_grp = row8 < N_GROUPS
        lg = jnp.where(is_grp, lt[0:8, :], neg)
        gmax = jnp.max(lg, axis=0, keepdims=True)
        grp = jnp.min(jnp.where(lg == gmax, row8, big), axis=0, keepdims=True)
        p_grp = 1.0 / jnp.sum(jnp.where(is_grp, jnp.exp(lg - gmax), 0.0), axis=0, keepdims=True)
        le = lt[EXPERT_ROW0:EXPERT_ROW0 + EXPERTS_PER_GROUP, :]
        for g in range(1, N_GROUPS):
            lo = EXPERT_ROW0 + g * EXPERTS_PER_GROUP
            le = jnp.where(grp == g, lt[lo:lo + EXPERTS_PER_GROUP, :], le)
        top1 = jnp.max(le, axis=0, keepdims=True)
        i1 = jnp.min(jnp.where(le == top1, row8, big), axis=0, keepdims=True)
        le2 = jnp.where(row8 == i1, neg, le)
        top2 = jnp.max(le2, axis=0, keepdims=True)
        i2 = jnp.min(jnp.where(le2 == top2, row8, big), axis=0, keepdims=True)
        e2 = jnp.exp(top2 - top1)
        w1 = p_grp / (1.0 + e2)
        w2 = p_grp * e2 / (1.0 + e2)

        row_e = lax.broadcasted_iota(jnp.int32, (N_EXPERTS, ts), 0).astype(F32)
        a1 = row_e == grp * EXPERTS_PER_GROUP + i1
        a2 = row_e == grp * EXPERTS_PER_GROUP + i2
        at = jnp.where(a1, 1.0, 0.0) + jnp.where(a2, 1.0, 0.0)
        rank = _dot(at.astype(BF16), utri_ref[...])
        cnt = rank[:, ts - 1:ts] + at[:, ts - 1:ts]
        cntp = jnp.floor((cnt + (SEG_ALIGN - 1)) * (1.0 / SEG_ALIGN)) * SEG_ALIGN
        cntp_b = jnp.broadcast_to(cntp, (N_EXPERTS, 128))
        r_e = lax.broadcasted_iota(jnp.int32, (N_EXPERTS, N_EXPERTS), 0)
        c_e = lax.broadcasted_iota(jnp.int32, (N_EXPERTS, N_EXPERTS), 1)
        off = _dot(jnp.where(r_e > c_e, 1.0, 0.0).astype(BF16), cntp_b.astype(BF16))
        slot = off[:, 0:1] + rank
        s1 = jnp.sum(jnp.where(a1, slot, 0.0), axis=0, keepdims=True)
        s2 = jnp.sum(jnp.where(a2, slot, 0.0), axis=0, keepdims=True)
        srow = lax.broadcasted_iota(jnp.int32, (n_slots, ts), 0).astype(F32)
        pg1 = jnp.where(srow == s1, 1.0, 0.0)
        pg2 = jnp.where(srow == s2, 1.0, 0.0)
        pg = (pg1 + pg2).astype(BF16)
        pg_ref[step] = pg
        xy_ref[step] = pltpu.bitcast(_dot(pg, tok_ref[...]).astype(BF16), jnp.int32)

        wrows = jnp.concatenate(
            [jnp.concatenate([p1, p2], axis=1) for p1, p2 in zip(_split3_f32(w1), _split3_f32(w2))]
            + [jnp.zeros((5, 2 * ts), F32)], axis=0).astype(BF16)
        wsl = lax.dot_general(jnp.concatenate([pg1.astype(BF16), pg2.astype(BF16)], axis=1), wrows, tn,
                              preferred_element_type=F32)
        ws_ref[step] = jnp.broadcast_to(wsl[:, 0:1] + wsl[:, 1:2] + wsl[:, 2:3], (n_slots, 128))
        meta_ref[step, 0:N_EXPERTS, :] = off.astype(jnp.int32)
        meta_ref[step, N_EXPERTS:2 * N_EXPERTS, :] = cntp_b.astype(jnp.int32)

    @pl.when((step >= ns) & (step < ns + n_esteps))
    def _experts():
        tile_row = (lax.broadcasted_iota(jnp.int32, (half_w, D_MODEL), 0) // 8) * SEG_ALIGN
        experts = range(MOE_EXPERTS_PER_STEP)
        offs, cnts = {}, {}
        for el in experts:
            e = (step - ns) * MOE_EXPERTS_PER_STEP + el
            offs[el] = [meta_ref[j, pl.ds(e, 1), :][0, 0] for j in range(ns)]
            cnts[el] = [meta_ref[j, pl.ds(N_EXPERTS + e, 1), :][0, 0] for j in range(ns)]
        n_win = functools.reduce(jnp.maximum, [(c + MOE_WINDOW - 1) // MOE_WINDOW
                                               for el in experts for c in cnts[el]])

        def window(w, carry):
            starts, lhs = {}, {}
            for el in experts:
                for j in range(ns):
                    st = jnp.minimum(offs[el][j] + w * MOE_WINDOW, n_slots - MOE_WINDOW)
                    starts[el, j] = pl.multiple_of(st, SEG_ALIGN)
                lhs[el] = jnp.concatenate(
                    [pltpu.bitcast(xy_ref[j, pl.ds(pl.multiple_of(starts[el, j] // 2, 8), half_w), :], BF16)
                     for j in range(ns)], axis=0)
            gate = {el: _dot(lhs[el], wg_ref[el]) for el in experts}
            up = {el: _dot(lhs[el], wu_ref[el]) for el in experts}
            y = {el: _dot((gate[el] * _sigmoid(gate[el]) * up[el]).astype(BF16), wd_ref[el]) for el in experts}
            for el in experts:
                for j in range(ns):
                    rows = starts[el, j] + tile_row
                    own = (rows >= offs[el][j] + w * MOE_WINDOW) & (rows < offs[el][j] + cnts[el][j])
                    yw = (y[el][j * MOE_WINDOW:(j + 1) * MOE_WINDOW, :]
                          * ws_ref[j, pl.ds(starts[el, j], MOE_WINDOW), 0:1])
                    pltpu.store(xy_ref.at[j, pl.ds(pl.multiple_of(starts[el, j] // 2, 8), half_w), :],
                                pltpu.bitcast(yw.astype(BF16), jnp.int32), mask=own)
            return carry

        lax.fori_loop(0, n_win, window, 0)

    @pl.when(step >= ns + n_esteps)
    def _combine():
        j = step - ns - n_esteps
        c = lax.dot_general(pg_ref[j], pltpu.bitcast(xy_ref[j], BF16), (((0,), (0,)), ((), ())),
                            preferred_element_type=F32)
        o_ref[...] = _rms_norm(x1_ref[...] + c, fw_ref[...])


def _moe(tok, logt, x1, w, ts, ns):
    n = tok.shape[0]
    n_slots = _moe_slots(ts)
    n_esteps = N_EXPERTS // MOE_EXPERTS_PER_STEP
    ne = MOE_EXPERTS_PER_STEP
    sub_in = lambda s, k: s * ns + jnp.clip(k, 0, ns - 1)
    sub_out = lambda s, k: s * ns + jnp.clip(k - ns - n_esteps, 0, ns - 1)
    expert = lambda k: jnp.clip(k - ns, 0, n_esteps - 1)
    utri = (jnp.arange(ts)[:, None] < jnp.arange(ts)[None, :]).astype(BF16)
    return pl.pallas_call(
        functools.partial(_moe_kernel, ts, ns),
        grid=(n // (ts * ns), 2 * ns + n_esteps),
        in_specs=[pl.BlockSpec((ts, D_MODEL), lambda s, k: (sub_in(s, k), 0)),
                  pl.BlockSpec((ROUTER_ROWS, ts), lambda s, k: (0, sub_in(s, k))),
                  pl.BlockSpec((ts, D_MODEL), lambda s, k: (sub_out(s, k), 0)),
                  pl.BlockSpec((ts, ts), lambda s, k: (0, 0)),
                  pl.BlockSpec((ne, D_MODEL, D_EXPERT), lambda s, k: (expert(k), 0, 0)),
                  pl.BlockSpec((ne, D_MODEL, D_EXPERT), lambda s, k: (expert(k), 0, 0)),
                  pl.BlockSpec((ne, D_EXPERT, D_MODEL), lambda s, k: (expert(k), 0, 0)),
                  pl.BlockSpec((1, D_MODEL), lambda s, k: (0, 0))],
        out_specs=pl.BlockSpec((ts, D_MODEL), lambda s, k: (sub_out(s, k), 0)),
        out_shape=jax.ShapeDtypeStruct((n, D_MODEL), F32),
        scratch_shapes=[pltpu.VMEM((ns, n_slots // 2, D_MODEL), jnp.int32), pltpu.VMEM((ns, n_slots, ts), BF16),
                        pltpu.VMEM((ns, n_slots, 128), F32), pltpu.VMEM((ns, 2 * N_EXPERTS, 128), jnp.int32)],
        compiler_params=pltpu.CompilerParams(dimension_semantics=("parallel", "arbitrary"),
                                             vmem_limit_bytes=VMEM_LIMIT),
        name="moe",
    )(tok, logt, x1, utri, w["wg"], w["wu"], w["wd"], w["fw"])


def _prep_weights(norm_mix_w, w_in, sgu_ln_w, sgu_ln_b, sgu_w_s, sgu_b, rwkv_mu, w_lora_up, w_bias,
                  a_lora_up, a_bias, g_lora_up, k_k, k_a, r_k, lnx_w, lnx_b, proj_a, proj_b, w_out,
                  norm_ffn_w, router_group, router_group_bias, router_expert, router_expert_bias,
                  moe_w_gate, moe_w_up, moe_w_down, norm_final_w):
    row = lambda z: z.reshape(1, -1).astype(F32)
    causal = jnp.tril(jnp.ones((CHUNK, CHUNK), dtype=bool))
    zeros_lora = jnp.zeros((64, D_MODEL), F32)
    head = jnp.arange(128) // B_HEAD_DIM
    ones_bd = (head[:, None] == head[None, :]).astype(BF16)
    pad = EXPERT_ROW0 - N_GROUPS
    rw = jnp.transpose(jnp.concatenate([router_group, jnp.zeros((D_MODEL, pad), F32), router_expert],
                                       axis=1))
    rw_hi = rw.astype(BF16)
    return {
        "nw": row(norm_mix_w), "win": w_in.astype(BF16), "lnw": row(sgu_ln_w), "lnb": row(sgu_ln_b),
        "wsc": jnp.where(causal[None], sgu_w_s, 0.0).astype(BF16),
        "bsb": jnp.repeat(jnp.transpose(sgu_b), A_HEAD_DIM, axis=1),
        "ws0": jnp.repeat(sgu_w_s[:, 0, 0], A_HEAD_DIM).reshape(1, -1),
        "bs0": jnp.repeat(sgu_b[:, 0], A_HEAD_DIM).reshape(1, -1),
        "mu": row(rwkv_mu),
        "wl": jnp.concatenate([w_lora_up, zeros_lora], axis=0).astype(BF16), "wb": row(w_bias),
        "al": jnp.concatenate([zeros_lora, a_lora_up], axis=0).astype(BF16), "ab": row(a_bias),
        "gl": g_lora_up.astype(BF16), "kk": row(k_k), "ka": row(k_a), "rk": row(r_k),
        "pa": proj_a.astype(BF16), "ones2": jnp.concatenate([ones_bd, ones_bd], axis=0),
        "lxw": row(lnx_w), "lxb": row(lnx_b), "pb": proj_b.astype(BF16), "wo": w_out.astype(BF16),
        "nfw": row(norm_ffn_w),
        "rw3": jnp.concatenate([rw_hi, (rw - rw_hi.astype(F32)).astype(BF16), rw_hi], axis=1),
        "rbt": jnp.concatenate([router_group_bias, jnp.zeros((pad,), F32), router_expert_bias]).reshape(-1, 1),
        "wg": moe_w_gate.astype(BF16), "wu": moe_w_up.astype(BF16), "wd": moe_w_down.astype(BF16),
        "fw": row(norm_final_w),
    }


def kernel(x_prompt, x_sample, state_wkv, state_shift, norm_mix_w, w_in, sgu_ln_w, sgu_ln_b, sgu_w_s, sgu_b, rwkv_mu, w_lora_up, w_bias, a_lora_up, a_bias, g_lora_up, k_k, k_a, r_k, lnx_w, lnx_b, proj_a, proj_b, w_out, norm_ffn_w, router_group, router_group_bias, router_expert, router_expert_bias, moe_w_gate, moe_w_up, moe_w_down, norm_final_w):
    layer = [z[0] for z in (norm_mix_w, w_in, sgu_ln_w, sgu_ln_b, sgu_w_s, sgu_b, rwkv_mu, w_lora_up, w_bias,
                            a_lora_up, a_bias, g_lora_up, k_k, k_a, r_k, lnx_w, lnx_b, proj_a, proj_b, w_out,
                            norm_ffn_w, router_group, router_group_bias, router_expert, router_expert_bias,
                            moe_w_gate, moe_w_up, moe_w_down)]
    w = _prep_weights(*layer, norm_final_w)
    bsz, t_len, _ = x_prompt.shape
    n_s = x_sample.shape[0]

    (outa, gateb, g, bonus, r, lw, k, v, kn, bs, last) = _front_prompt(x_prompt, w, 256)
    y, wkv_p = _scan_prompt(r, lw, k, v, kn, bs, 128, N_PAIRS)
    flat = lambda z: z.reshape(bsz * t_len, D_MODEL)
    x1, tok, logt = _back(True, y, flat(bonus), flat(g), flat(gateb), flat(outa), flat(x_prompt), w, 512)
    y_prompt = _moe(tok, logt, x1, w, 512, 4).reshape(bsz, t_len, D_MODEL)

    xs2 = x_sample.reshape(n_s, D_MODEL)
    (outa, gateb, g, bonus, r, lw, k, v, kn, bs, cols_s, vn_s) = _front_sample(xs2, state_shift[0], w)
    y, wkv_s = _scan_sample(r, lw, k, v, kn, bs, state_wkv)
    x1, tok, logt = _back(False, y, bonus, g, gateb, outa, xs2, w, n_s)
    y_sample = _moe(tok, logt, x1, w, n_s, 1).reshape(n_s, 1, D_MODEL)

    return (y_prompt, y_sample, wkv_p[None], last.reshape(1, bsz, SHIFT_WIDTH), wkv_s, cols_s[None],
            vn_s.reshape(1, n_s, 1, A_HEADS, A_HEAD_DIM))
```

```python
import functools
import math

import jax
import jax.numpy as jnp
from jax import lax
from jax.experimental import pallas as pl
from jax.experimental.pallas import tpu as pltpu

F32 = jnp.float32
BF16 = jnp.bfloat16
HIGHEST = lax.Precision.HIGHEST

D_MODEL = 1024
CHUNK = 128
A_HEADS = 8
A_HEAD_DIM = 128
B_HEADS = 16
B_HEAD_DIM = 64
N_PAIRS = B_HEADS // 2
SHIFT_WIDTH = 3328
OFF_V = 1024
OFF_SHIFT = 2048
OFF_GATE_A = OFF_SHIFT + SHIFT_WIDTH
OFF_GATE_B = OFF_GATE_A + D_MODEL
IN_COLS = OFF_GATE_B + D_MODEL
N_GROUPS = 4
EXPERTS_PER_GROUP = 8
N_EXPERTS = 32
D_EXPERT = 256
ROUTER_ROWS = 40
EXPERT_ROW0 = 8
SEG_ALIGN = 16
BACK_ROWS = 256
MOE_WINDOW = 48
MOE_EXPERTS_PER_STEP = 4
RMS_EPS = 1e-6
LN_EPS = 1e-5
GN_EPS = 64e-5

SCAN_CHUNK = 64
FRONT_BLOCK = 256
VMEM_LIMIT = 56 * 1024 * 1024


def _gelu(x):
    return x * (0.5 * (1.0 + jnp.tanh(math.sqrt(2.0 / math.pi) * (x + 0.044715 * (x * x * x)))))


def _sigmoid(x):
    return 1.0 / (1.0 + jnp.exp(-x))


def _softplus(z):
    return jnp.maximum(z, 0.0) + jnp.log(1.0 + jnp.exp(-jnp.abs(z)))


def _rms_norm(x, g):
    return x * lax.rsqrt(jnp.mean(x * x, axis=-1, keepdims=True) + RMS_EPS) * g


def _dot(a, b):
    return jnp.dot(a, b, preferred_element_type=F32)


def _dot_hi(a, b):
    return jnp.dot(a, b, preferred_element_type=F32, precision=HIGHEST)


def _head_sums(z, ones2):
    outs = [_dot(z[:, p * 256:(p + 1) * 256].astype(BF16), ones2) for p in range(z.shape[1] // 256)]
    return outs[0] if len(outs) == 1 else jnp.concatenate(outs, axis=1)


def _front_kernel(is_sample, tm, *refs):
    if is_sample:
        (x_ref, prev_ref, nw_ref, win_ref, lnw_ref, lnb_ref, ws0_ref, bs0_ref, mu_ref, wl_ref, wb_ref,
         al_ref, ab_ref, gl_ref, kk_ref, ka_ref, rk_ref, pa_ref, ones2_ref,
         outa_ref, gateb_ref, g_ref, bonus_ref, r_ref, lw_ref, k_ref, v_ref, kn_ref, bs_ref,
         cols_ref, vn_ref, ya_ref) = refs
    else:
        (x_ref, nw_ref, win_ref, lnw_ref, lnb_ref, wsc_ref, bsb_ref, mu_ref, wl_ref, wb_ref,
         al_ref, ab_ref, gl_ref, kk_ref, ka_ref, rk_ref, pa_ref, ones2_ref,
         outa_ref, gateb_ref, g_ref, bonus_ref, r_ref, lw_ref, k_ref, v_ref, kn_ref, bs_ref,
         last_ref, ya_ref, carry_ref) = refs

        @pl.when(pl.program_id(1) == 0)
        def _():
            carry_ref[...] = jnp.zeros_like(carry_ref)

    xb = _rms_norm(x_ref[...], nw_ref[...]).astype(BF16)
    ones2 = ones2_ref[...]

    def proj(lo, width):
        return _dot(xb, win_ref[:, lo:lo + width])

    def shifted(lo, width, cols=None):
        cs = slice(lo, lo + width)
        if cols is None:
            cols = proj(OFF_SHIFT + lo, width)
        if is_sample:
            prev = prev_ref[:, cs]
            cols_ref[:, cs] = cols
        else:
            row = lax.broadcasted_iota(jnp.int32, cols.shape, 0)
            prev = jnp.where(row == 0, carry_ref[0:1, cs], pltpu.roll(cols, 1, 0))
            carry_ref[0:1, cs] = cols[tm - 1:tm, :]
            last_ref[:, cs] = cols[tm - 1:tm, :]
        return cols + (prev - cols) * mu_ref[:, cs]

    tail = shifted(3 * D_MODEL, 256)
    wa = tail[:, 0:128]
    twa = jnp.tanh(wa).astype(BF16)
    wab = wa.astype(BF16)
    sgd = _sigmoid(tail[:, 128:256]).astype(BF16)

    def block_matmuls(j):
        lo = j * FRONT_BLOCK
        cs = slice(lo, lo + FRONT_BLOCK)
        return (proj(OFF_SHIFT + lo, FRONT_BLOCK), proj(OFF_SHIFT + D_MODEL + lo, FRONT_BLOCK),
                proj(OFF_SHIFT + 2 * D_MODEL + lo, FRONT_BLOCK), _dot(twa, wl_ref[:, cs]), _dot(wab, al_ref[:, cs]),
                _dot(sgd, gl_ref[:, cs]), proj(lo, FRONT_BLOCK), proj(OFF_V + lo, FRONT_BLOCK),
                proj(OFF_GATE_B + lo, FRONT_BLOCK))

    n_blocks = D_MODEL // FRONT_BLOCK
    raw = block_matmuls(0)
    gate_a_raw = None
    for j in range(n_blocks):
        lo = j * FRONT_BLOCK
        cs = slice(lo, lo + FRONT_BLOCK)
        r_raw, k_raw, v_raw, lw_dot, a_dot, g_dot, u_raw, va_raw, gb_raw = raw
        if j + 1 < n_blocks:
            raw = block_matmuls(j + 1)
        else:
            gate_a_raw = proj(OFF_GATE_A, D_MODEL)

        r = shifted(lo, FRONT_BLOCK, r_raw)
        k = shifted(D_MODEL + lo, FRONT_BLOCK, k_raw)
        vb = shifted(2 * D_MODEL + lo, FRONT_BLOCK, v_raw)
        w_log = -_softplus(-(wb_ref[:, cs] + lw_dot)) - 0.5
        logw = -jnp.exp(w_log)
        a = _sigmoid(ab_ref[:, cs] + a_dot)
        g_ref[:, cs] = g_dot
        kk = k * kk_ref[:, cs]
        kkn = kk / jnp.maximum(jnp.sqrt(_head_sums(kk * kk, ones2)), 1e-12)
        k2 = k * (1.0 + (a - 1.0) * ka_ref[:, cs])
        bonus_ref[:, cs] = _head_sums(r * k2 * rk_ref[:, cs], ones2) * vb
        bsc = kkn * a
        outs = ((r_ref, r), (lw_ref, logw), (k_ref, k2), (v_ref, vb), (kn_ref, kkn), (bs_ref, bsc))
        for o_ref, val in outs:
            if is_sample:
                o_ref[:, cs] = val
            else:
                for q in range(FRONT_BLOCK // 128):
                    o_ref[lo // 128 + q] = val[:, q * 128:(q + 1) * 128]

        u2 = _gelu(u_raw)
        v2 = _gelu(va_raw)
        vnb = []
        for q in range(FRONT_BLOCK // A_HEAD_DIM):
            h = lo // A_HEAD_DIM + q
            hs = slice(h * A_HEAD_DIM, (h + 1) * A_HEAD_DIM)
            u = u2[:, q * A_HEAD_DIM:(q + 1) * A_HEAD_DIM]
            v = v2[:, q * A_HEAD_DIM:(q + 1) * A_HEAD_DIM]
            mean = jnp.mean(v, axis=-1, keepdims=True)
            vc = v - mean
            var = jnp.mean(vc * vc, axis=-1, keepdims=True)
            vn = vc * lax.rsqrt(var + LN_EPS) * lnw_ref[:, hs] + lnb_ref[:, hs]
            if is_sample:
                vn_ref[:, hs] = vn
                ya_ref[:, hs] = (u * (vn * ws0_ref[:, hs] + bs0_ref[:, hs])).astype(BF16)
            else:
                vnb.append(vn.astype(BF16))
        if not is_sample:
            h0 = lo // A_HEAD_DIM
            w_cat = jnp.concatenate([wsc_ref[h0], wsc_ref[h0 + 1]], axis=1)
            zero = jnp.zeros((CHUNK, A_HEAD_DIM), BF16)
            for c in range(tm // CHUNK):
                rs = slice(c * CHUNK, (c + 1) * CHUNK)
                v_bd = jnp.concatenate([jnp.concatenate([vnb[0][rs, :], zero], axis=1),
                                        jnp.concatenate([zero, vnb[1][rs, :]], axis=1)], axis=0)
                ya_ref[rs, cs] = (u2[rs, :] * (_dot(w_cat, v_bd) + bsb_ref[:, cs])).astype(BF16)

        gateb_ref[:, cs] = _sigmoid(gb_raw)

    outa_ref[...] = _sigmoid(gate_a_raw) * _dot(ya_ref[...], pa_ref[...])


def _const_spec(shape, grid_rank):
    zeros = (0,) * len(shape)
    if grid_rank == 1:
        return pl.BlockSpec(shape, lambda i: zeros, pipeline_mode=pl.Buffered(1))
    return pl.BlockSpec(shape, lambda b, i: zeros, pipeline_mode=pl.Buffered(1))


def _front_prompt(x, w, tm):
    bsz, t_len, _ = x.shape
    nt = t_len // tm
    consts = [w["nw"], w["win"], w["lnw"], w["lnb"], w["wsc"], w["bsb"], w["mu"], w["wl"], w["wb"],
              w["al"], w["ab"], w["gl"], w["kk"], w["ka"], w["rk"], w["pa"], w["ones2"]]
    tok = lambda: pl.BlockSpec((None, tm, D_MODEL), lambda b, i: (b, i, 0))
    pair = lambda: pl.BlockSpec((None, N_PAIRS, tm, 128), lambda b, i: (b, 0, i, 0))
    tok_shape = jax.ShapeDtypeStruct((bsz, t_len, D_MODEL), F32)
    pair_shape = jax.ShapeDtypeStruct((bsz, N_PAIRS, t_len, 128), F32)
    return pl.pallas_call(
        functools.partial(_front_kernel, False, tm),
        grid=(bsz, nt),
        in_specs=[tok()] + [_const_spec(c.shape, 2) for c in consts],
        out_specs=[tok(), tok(), tok(), tok()] + [pair() for _ in range(6)]
        + [pl.BlockSpec((None, 1, SHIFT_WIDTH), lambda b, i: (b, 0, 0))],
        out_shape=[tok_shape] * 4 + [pair_shape] * 6 + [jax.ShapeDtypeStruct((bsz, 1, SHIFT_WIDTH), F32)],
        scratch_shapes=[pltpu.VMEM((tm, D_MODEL), BF16), pltpu.VMEM((8, SHIFT_WIDTH), F32)],
        compiler_params=pltpu.CompilerParams(dimension_semantics=("parallel", "arbitrary"),
                                             vmem_limit_bytes=VMEM_LIMIT),
        name="front_prompt",
    )(x, *consts)


def _front_sample(x, prev, w):
    n = x.shape[0]
    consts = [w["nw"], w["win"], w["lnw"], w["lnb"], w["ws0"], w["bs0"], w["mu"], w["wl"], w["wb"],
              w["al"], w["ab"], w["gl"], w["kk"], w["ka"], w["rk"], w["pa"], w["ones2"]]
    tok = lambda: pl.BlockSpec((n, D_MODEL), lambda i: (0, 0))
    wide = lambda: pl.BlockSpec((n, SHIFT_WIDTH), lambda i: (0, 0))
    tok_shape = jax.ShapeDtypeStruct((n, D_MODEL), F32)
    return pl.pallas_call(
        functools.partial(_front_kernel, True, n),
        grid=(1,),
        in_specs=[tok(), wide()] + [_const_spec(c.shape, 1) for c in consts],
        out_specs=[tok() for _ in range(10)] + [wide(), tok()],
        out_shape=[tok_shape] * 10 + [jax.ShapeDtypeStruct((n, SHIFT_WIDTH), F32), tok_shape],
        scratch_shapes=[pltpu.VMEM((n, D_MODEL), BF16)],
        compiler_params=pltpu.CompilerParams(dimension_semantics=("arbitrary",),
                                             vmem_limit_bytes=VMEM_LIMIT),
        name="front_sample",
    )(x, prev, *consts)


def _split3(x):
    hi = x.astype(BF16)
    r1 = x - hi.astype(F32)
    mid = r1.astype(BF16)
    lo = (r1 - mid.astype(F32)).astype(BF16)
    return hi, mid, lo


def _scan_kernel(tt, npp, r_ref, lw_ref, k_ref, v_ref, kn_ref, bs_ref, y_ref, sout_ref, st_ref):
    c_len = SCAN_CHUNK
    i = pl.program_id(2)

    @pl.when(i == 0)
    def _():
        st_ref[...] = jnp.zeros_like(st_ref)

    lane = lax.broadcasted_iota(jnp.int32, (c_len, 128), 1)
    head0 = lane < B_HEAD_DIM
    m = 2 * c_len
    tpos = jnp.bitwise_and(lax.broadcasted_iota(jnp.int32, (m, m), 0), c_len - 1)
    spos = jnp.bitwise_and(lax.broadcasted_iota(jnp.int32, (m, m), 1), c_len - 1)
    strict = tpos > spos
    incl = tpos >= spos
    eye = lax.broadcasted_iota(jnp.int32, (128, 128), 0) == lax.broadcasted_iota(jnp.int32, (128, 128), 1)
    rc = lax.broadcasted_iota(jnp.int32, (c_len, 3 * c_len), 0)
    cc = jnp.bitwise_and(lax.broadcasted_iota(jnp.int32, (c_len, 3 * c_len), 1), c_len - 1)
    tri3 = jnp.where(rc >= cc, 1.0, 0.0).astype(BF16)

    def pair_diag(z):
        return jnp.concatenate([jnp.where(head0, z, 0.0), jnp.where(head0, 0.0, z)], axis=0).astype(BF16)

    def bdot(a, b):
        return _dot(a.astype(BF16), b.astype(BF16))

    n_chunks = tt // c_len
    insts = [(c, q) for c in range(n_chunks) for q in range(npp)]
    rows = lambda c: slice(c * c_len, (c + 1) * c_len)

    log_p = {}
    for c, q in insts:
        log_p[c, q] = _dot(tri3, jnp.concatenate(_split3(lw_ref[q, rows(c), :]), axis=0))

    ops, gram = {}, {}
    for c, q in insts:
        lp = log_p[c, q]
        kn = kn_ref[q, rows(c), :]
        bs = bs_ref[q, rows(c), :]
        kx = k_ref[q, rows(c), :]
        log_pc = lp[c_len - 1:c_len, :]
        e_inv = jnp.exp(-lp)
        e_dec = jnp.exp(log_pc - lp)
        a_m = pair_diag(-kn * jnp.exp(lp - lw_ref[q, rows(c), :]))
        r_m = pair_diag(r_ref[q, rows(c), :] * jnp.exp(lp))
        v_m = pair_diag(v_ref[q, rows(c), :])
        btkt = jnp.concatenate([pair_diag(bs * e_dec), pair_diag(kx * e_dec)], axis=0)
        pc_col = jnp.sum(jnp.where(eye, jnp.exp(log_pc), 0.0), axis=1, keepdims=True)
        ops[c, q] = (a_m, r_m, v_m, btkt, pc_col)
        gram[c, q] = lax.dot_general(jnp.concatenate([a_m, r_m], axis=0),
                                     jnp.concatenate([pair_diag(bs * e_inv), pair_diag(kx * e_inv)], axis=0),
                                     (((1,), (1,)), ((), ())), preferred_element_type=F32)

    n_sum, pw, l_r, lakv = {}, {}, {}, {}
    for c, q in insts:
        g = gram[c, q]
        l_ab = jnp.where(strict, g[0:m, 0:m], 0.0)
        n_sum[c, q] = l_ab
        pw[c, q] = l_ab.astype(BF16)
        l_r[c, q] = jnp.concatenate([jnp.where(incl, g[m:2 * m, 0:m], 0.0),
                                     jnp.where(incl, g[m:2 * m, m:2 * m], 0.0)], axis=1).astype(BF16)
        lakv[c, q] = _dot(jnp.where(strict, g[0:m, m:2 * m], 0.0).astype(BF16), ops[c, q][2])

    pw_f = {}
    for c, q in insts:
        pw_f[c, q] = _dot(pw[c, q], pw[c, q])
        pw[c, q] = pw_f[c, q].astype(BF16)
    for _ in range(int(math.log2(c_len)) - 2):
        prod = {}
        for c, q in insts:
            prod[c, q] = _dot(pw[c, q], jnp.concatenate([pw[c, q], n_sum[c, q].astype(BF16)], axis=1))
        for c, q in insts:
            n_sum[c, q] = n_sum[c, q] + pw_f[c, q] + prod[c, q][:, m:2 * m]
            pw_f[c, q] = prod[c, q][:, 0:m]
            pw[c, q] = pw_f[c, q].astype(BF16)
    for c, q in insts:
        n_sum[c, q] = n_sum[c, q] + pw_f[c, q] + bdot(pw[c, q], n_sum[c, q])

    wu = {}
    for c, q in insts:
        rhs = jnp.concatenate([ops[c, q][0].astype(F32), lakv[c, q]], axis=1)
        wu[c, q] = rhs + bdot(n_sum[c, q], rhs)

    for c in range(n_chunks):
        xs, st = {}, {}
        for q in range(npp):
            st[q] = st_ref[q]
            xs[q] = bdot(jnp.concatenate([wu[c, q][:, 0:128].astype(BF16), ops[c, q][1]], axis=0), st[q])
        for q in range(npp):
            _, _, v_m, btkt, pc_col = ops[c, q]
            uv = jnp.concatenate([(xs[q][0:m, :] + wu[c, q][:, 128:256]).astype(BF16), v_m], axis=0)
            y_m = xs[q][m:2 * m, :] + _dot(l_r[c, q], uv)
            y_ref[q, rows(c), :] = y_m[0:c_len, :] + y_m[c_len:m, :]
            st_ref[q] = pc_col * st[q] + lax.dot_general(btkt, uv, (((0,), (0,)), ((), ())),
                                                         preferred_element_type=F32)

    @pl.when(i == pl.num_programs(2) - 1)
    def _():
        n = B_HEAD_DIM
        for q in range(npp):
            full_t = st_ref[q].T
            sout_ref[2 * q] = full_t[0:n, 0:n]
            sout_ref[2 * q + 1] = full_t[n:128, n:128]


def _scan_prompt(r, lw, k, v, kn, bs, tt, npp):
    bsz, n_pairs, t_len, _ = r.shape
    blk = lambda: pl.BlockSpec((None, npp, tt, 128), lambda b, p, i: (b, p, i, 0))
    return pl.pallas_call(
        functools.partial(_scan_kernel, tt, npp),
        grid=(bsz, n_pairs // npp, t_len // tt),
        in_specs=[blk() for _ in range(6)],
        out_specs=[blk(), pl.BlockSpec((None, 2 * npp, B_HEAD_DIM, B_HEAD_DIM), lambda b, p, i: (b, p, 0, 0))],
        out_shape=[jax.ShapeDtypeStruct((bsz, n_pairs, t_len, 128), F32),
                   jax.ShapeDtypeStruct((bsz, 2 * n_pairs, B_HEAD_DIM, B_HEAD_DIM), F32)],
        scratch_shapes=[pltpu.VMEM((npp, 128, 128), F32)],
        compiler_params=pltpu.CompilerParams(dimension_semantics=("parallel", "parallel", "arbitrary"),
                                             vmem_limit_bytes=VMEM_LIMIT),
        name="scan_prompt",
    )(r, lw, k, v, kn, bs)


def _step_kernel(r_ref, lw_ref, k_ref, v_ref, kn_ref, bs_ref, s_ref, y_ref, sout_ref):
    a = -kn_ref[...]
    w = jnp.exp(lw_ref[...])
    kx = k_ref[...]
    r = r_ref[...]
    b = bs_ref[...]
    ys = []
    for i in range(B_HEAD_DIM):
        s0 = s_ref[i]
        sa = jnp.sum(s0 * a, axis=0, keepdims=True)
        s1 = s0 * w + sa * b + v_ref[i:i + 1, :] * kx
        sout_ref[i] = s1
        ys.append(jnp.sum(s1 * r, axis=0, keepdims=True))
    y_ref[...] = jnp.concatenate(ys, axis=0)


def _scan_sample(r, lw, k, v, kn, bs, s0):
    n = r.shape[0]
    vecs = [jnp.transpose(z) for z in (r, lw, k, v, kn, bs)]
    s0_t = jnp.transpose(s0[0], (1, 2, 3, 0))
    vec = lambda: pl.BlockSpec((B_HEAD_DIM, n), lambda h: (h, 0))
    st = lambda: pl.BlockSpec((None, B_HEAD_DIM, B_HEAD_DIM, n), lambda h: (h, 0, 0, 0))
    y_t, s1_t = pl.pallas_call(
        _step_kernel,
        grid=(B_HEADS,),
        in_specs=[vec() for _ in range(6)] + [st()],
        out_specs=[vec(), st()],
        out_shape=[jax.ShapeDtypeStruct((D_MODEL, n), F32), jax.ShapeDtypeStruct(s0_t.shape, F32)],
        compiler_params=pltpu.CompilerParams(dimension_semantics=("parallel",)),
        name="scan_sample",
    )(*vecs, s0_t)
    return jnp.transpose(y_t), jnp.transpose(s1_t, (3, 0, 1, 2))[None]


def _back_kernel(paired, y_ref, bonus_ref, g_ref, gateb_ref, outa_ref, x_ref, lxw_ref, lxb_ref, pb_ref,
                 wo_ref, nfw_ref, rw_ref, rb_ref, ones2_ref, x1_ref, tok_ref, logt_ref):
    tm = x_ref.shape[0]
    n_parts = max(1, tm // BACK_ROWS)
    ones2 = ones2_ref[...]
    inv_n = 1.0 / B_HEAD_DIM
    parts = range(n_parts)
    rows = lambda i: slice(i * (tm // n_parts), (i + 1) * (tm // n_parts))

    if paired:
        y = [jnp.concatenate([y_ref[p, rows(i), :] for p in range(N_PAIRS)], axis=1) for i in parts]
    else:
        y = [y_ref[rows(i), :] for i in parts]
    mean = [_head_sums(y[i], ones2) * inv_n for i in parts]
    yc = [y[i] - mean[i] for i in parts]
    var = [_head_sums(yc[i] * yc[i], ones2) * inv_n for i in parts]
    yb = [((yc[i] * lax.rsqrt(var[i] + GN_EPS) * lxw_ref[...] + lxb_ref[...] + bonus_ref[rows(i), :])
           * g_ref[rows(i), :]).astype(BF16) for i in parts]
    mb = [_dot(yb[i], pb_ref[...]) for i in parts]
    merged = [(outa_ref[rows(i), :] + gateb_ref[rows(i), :] * mb[i]).astype(BF16) for i in parts]
    out = [_dot(merged[i], wo_ref[...]) for i in parts]
    tok = []
    for i in parts:
        x1 = x_ref[rows(i), :] + out[i]
        x1_ref[rows(i), :] = x1
        tok.append(_rms_norm(x1, nfw_ref[...]))
        tok_ref[rows(i), :] = tok[i].astype(BF16)
    for i in parts:
        t_hi = tok[i].astype(BF16)
        t_lo = (tok[i] - t_hi.astype(F32)).astype(BF16)
        logt_ref[:, rows(i)] = lax.dot_general(
            rw_ref[...], jnp.concatenate([t_hi, t_hi, t_lo], axis=1), (((1,), (1,)), ((), ())),
            preferred_element_type=F32) + rb_ref[...]


def _back(paired, y, bonus, g, gateb, outa, x, w, tm):
    n = x.shape[0]
    consts = [w["lxw"], w["lxb"], w["pb"], w["wo"], w["nfw"], w["rw3"], w["rbt"], w["ones2"]]
    tok = lambda: pl.BlockSpec((tm, D_MODEL), lambda i: (i, 0))
    if paired:
        nt = y.shape[2] // tm
        y_spec = pl.BlockSpec((None, N_PAIRS, tm, 128), lambda i: (i // nt, 0, i % nt, 0))
    else:
        y_spec = tok()
    return pl.pallas_call(
        functools.partial(_back_kernel, paired),
        grid=(n // tm,),
        in_specs=[y_spec] + [tok() for _ in range(5)] + [_const_spec(c.shape, 1) for c in consts],
        out_specs=[tok(), tok(), pl.BlockSpec((ROUTER_ROWS, tm), lambda i: (0, i))],
        out_shape=[jax.ShapeDtypeStruct((n, D_MODEL), F32), jax.ShapeDtypeStruct((n, D_MODEL), BF16),
                   jax.ShapeDtypeStruct((ROUTER_ROWS, n), F32)],
        compiler_params=pltpu.CompilerParams(dimension_semantics=("parallel",), vmem_limit_bytes=VMEM_LIMIT),
        name="back_prompt" if paired else "back_sample",
    )(y, bonus, g, gateb, outa, x, *consts)


def _moe_slots(ts):
    n = 2 * ts + N_EXPERTS * (SEG_ALIGN - 1)
    return -(-n // MOE_WINDOW) * MOE_WINDOW


def _split3_f32(x):
    hi = x.astype(BF16).astype(F32)
    mid = (x - hi).astype(BF16).astype(F32)
    lo = (x - hi - mid).astype(BF16).astype(F32)
    return hi, mid, lo


def _moe_kernel(ts, ns, tok_ref, logt_ref, x1_ref, utri_ref, wg_ref, wu_ref, wd_ref, fw_ref, o_ref,
                xy_ref, pg_ref, ws_ref, meta_ref):
    n_slots = 2 * xy_ref.shape[1]
    half_w = MOE_WINDOW // 2
    n_esteps = N_EXPERTS // MOE_EXPERTS_PER_STEP
    step = pl.program_id(1)
    tn = (((1,), (1,)), ((), ()))

    @pl.when(step < ns)
    def _route_and_sort():
        lt = logt_ref[...]
        neg = jnp.float32(-jnp.inf)
        big = jnp.float32(99.0)
        row8 = lax.broadcasted_iota(jnp.int32, (8, ts), 0).astype(F32)
        is_grp = row8 < N_GROUPS
        lg = jnp.where(is_grp, lt[0:8, :], neg)
        gmax = jnp.max(lg, axis=0, keepdims=True)
        grp = jnp.min(jnp.where(lg == gmax, row8, big), axis=0, keepdims=True)
        p_grp = 1.0 / jnp.sum(jnp.where(is_grp, jnp.exp(lg - gmax), 0.0), axis=0, keepdims=True)
        le = lt[EXPERT_ROW0:EXPERT_ROW0 + EXPERTS_PER_GROUP, :]
        for g in range(1, N_GROUPS):
            lo = EXPERT_ROW0 + g * EXPERTS_PER_GROUP
            le = jnp.where(grp == g, lt[lo:lo + EXPERTS_PER_GROUP, :], le)
        top1 = jnp.max(le, axis=0, keepdims=True)
        i1 = jnp.min(jnp.where(le == top1, row8, big), axis=0, keepdims=True)
        le2 = jnp.where(row8 == i1, neg, le)
        top2 = jnp.max(le2, axis=0, keepdims=True)
        i2 = jnp.min(jnp.where(le2 == top2, row8, big), axis=0, keepdims=True)
        e2 = jnp.exp(top2 - top1)
        w1 = p_grp / (1.0 + e2)
        w2 = p_grp * e2 / (1.0 + e2)

        row_e = lax.broadcasted_iota(jnp.int32, (N_EXPERTS, ts), 0).astype(F32)
        a1 = row_e == grp * EXPERTS_PER_GROUP + i1
        a2 = row_e == grp * EXPERTS_PER_GROUP + i2
        at = jnp.where(a1, 1.0, 0.0) + jnp.where(a2, 1.0, 0.0)
        rank = _dot(at.astype(BF16), utri_ref[...])
        cnt = rank[:, ts - 1:ts] + at[:, ts - 1:ts]
        cntp = jnp.floor((cnt + (SEG_ALIGN - 1)) * (1.0 / SEG_ALIGN)) * SEG_ALIGN
        cntp_b = jnp.broadcast_to(cntp, (N_EXPERTS, 128))
        r_e = lax.broadcasted_iota(jnp.int32, (N_EXPERTS, N_EXPERTS), 0)
        c_e = lax.broadcasted_iota(jnp.int32, (N_EXPERTS, N_EXPERTS), 1)
        off = _dot(jnp.where(r_e > c_e, 1.0, 0.0).astype(BF16), cntp_b.astype(BF16))
        slot = off[:, 0:1] + rank
        s1 = jnp.sum(jnp.where(a1, slot, 0.0), axis=0, keepdims=True)
        s2 = jnp.sum(jnp.where(a2, slot, 0.0), axis=0, keepdims=True)
        srow = lax.broadcasted_iota(jnp.int32, (n_slots, ts), 0).astype(F32)
        pg1 = jnp.where(srow == s1, 1.0, 0.0)
        pg2 = jnp.where(srow == s2, 1.0, 0.0)
        pg = (pg1 + pg2).astype(BF16)
        pg_ref[step] = pg
        xy_ref[step] = pltpu.bitcast(_dot(pg, tok_ref[...]).astype(BF16), jnp.int32)

        wrows = jnp.concatenate(
            [jnp.concatenate([p1, p2], axis=1) for p1, p2 in zip(_split3_f32(w1), _split3_f32(w2))]
            + [jnp.zeros((5, 2 * ts), F32)], axis=0).astype(BF16)
        wsl = lax.dot_general(jnp.concatenate([pg1.astype(BF16), pg2.astype(BF16)], axis=1), wrows, tn,
                              preferred_element_type=F32)
        ws_ref[step] = jnp.broadcast_to(wsl[:, 0:1] + wsl[:, 1:2] + wsl[:, 2:3], (n_slots, 128))
        meta_ref[step, 0:N_EXPERTS, :] = off.astype(jnp.int32)
        meta_ref[step, N_EXPERTS:2 * N_EXPERTS, :] = cntp_b.astype(jnp.int32)

    @pl.when((step >= ns) & (step < ns + n_esteps))
    def _experts():
        tile_row = (lax.broadcasted_iota(jnp.int32, (half_w, D_MODEL), 0) // 8) * SEG_ALIGN
        experts = range(MOE_EXPERTS_PER_STEP)
        offs, cnts = {}, {}
        for el in experts:
            e = (step - ns) * MOE_EXPERTS_PER_STEP + el
            offs[el] = [meta_ref[j, pl.ds(e, 1), :][0, 0] for j in range(ns)]
            cnts[el] = [meta_ref[j, pl.ds(N_EXPERTS + e, 1), :][0, 0] for j in range(ns)]
        n_win = functools.reduce(jnp.maximum, [(c + MOE_WINDOW - 1) // MOE_WINDOW
                                               for el in experts for c in cnts[el]])

        def window(w, carry):
            starts, lhs = {}, {}
            for el in experts:
                for j in range(ns):
                    st = jnp.minimum(offs[el][j] + w * MOE_WINDOW, n_slots - MOE_WINDOW)
                    starts[el, j] = pl.multiple_of(st, SEG_ALIGN)
                lhs[el] = jnp.concatenate(
                    [pltpu.bitcast(xy_ref[j, pl.ds(pl.multiple_of(starts[el, j] // 2, 8), half_w), :], BF16)
                     for j in range(ns)], axis=0)
            gate = {el: _dot(lhs[el], wg_ref[el]) for el in experts}
            up = {el: _dot(lhs[el], wu_ref[el]) for el in experts}
            y = {el: _dot((gate[el] * _sigmoid(gate[el]) * up[el]).astype(BF16), wd_ref[el]) for el in experts}
            for el in experts:
                for j in range(ns):
                    rows = starts[el, j] + tile_row
                    own = (rows >= offs[el][j] + w * MOE_WINDOW) & (rows < offs[el][j] + cnts[el][j])
                    yw = (y[el][j * MOE_WINDOW:(j + 1) * MOE_WINDOW, :]
                          * ws_ref[j, pl.ds(starts[el, j], MOE_WINDOW), 0:1])
                    pltpu.store(xy_ref.at[j, pl.ds(pl.multiple_of(starts[el, j] // 2, 8), half_w), :],
                                pltpu.bitcast(yw.astype(BF16), jnp.int32), mask=own)
            return carry

        lax.fori_loop(0, n_win, window, 0)

    @pl.when(step >= ns + n_esteps)
    def _combine():
        j = step - ns - n_esteps
        c = lax.dot_general(pg_ref[j], pltpu.bitcast(xy_ref[j], BF16), (((0,), (0,)), ((), ())),
                            preferred_element_type=F32)
        o_ref[...] = _rms_norm(x1_ref[...] + c, fw_ref[...])


def _moe(tok, logt, x1, w, ts, ns):
    n = tok.shape[0]
    n_slots = _moe_slots(ts)
    n_esteps = N_EXPERTS // MOE_EXPERTS_PER_STEP
    ne = MOE_EXPERTS_PER_STEP
    sub_in = lambda s, k: s * ns + jnp.clip(k, 0, ns - 1)
    sub_out = lambda s, k: s * ns + jnp.clip(k - ns - n_esteps, 0, ns - 1)
    expert = lambda k: jnp.clip(k - ns, 0, n_esteps - 1)
    utri = (jnp.arange(ts)[:, None] < jnp.arange(ts)[None, :]).astype(BF16)
    return pl.pallas_call(
        functools.partial(_moe_kernel, ts, ns),
        grid=(n // (ts * ns), 2 * ns + n_esteps),
        in_specs=[pl.BlockSpec((ts, D_MODEL), lambda s, k: (sub_in(s, k), 0)),
                  pl.BlockSpec((ROUTER_ROWS, ts), lambda s, k: (0, sub_in(s, k))),
                  pl.BlockSpec((ts, D_MODEL), lambda s, k: (sub_out(s, k), 0)),
                  pl.BlockSpec((ts, ts), lambda s, k: (0, 0)),
                  pl.BlockSpec((ne, D_MODEL, D_EXPERT), lambda s, k: (expert(k), 0, 0)),
                  pl.BlockSpec((ne, D_MODEL, D_EXPERT), lambda s, k: (expert(k), 0, 0)),
                  pl.BlockSpec((ne, D_EXPERT, D_MODEL), lambda s, k: (expert(k), 0, 0)),
                  pl.BlockSpec((1, D_MODEL), lambda s, k: (0, 0))],
        out_specs=pl.BlockSpec((ts, D_MODEL), lambda s, k: (sub_out(s, k), 0)),
        out_shape=jax.ShapeDtypeStruct((n, D_MODEL), F32),
        scratch_shapes=[pltpu.VMEM((ns, n_slots // 2, D_MODEL), jnp.int32), pltpu.VMEM((ns, n_slots, ts), BF16),
                        pltpu.VMEM((ns, n_slots, 128), F32), pltpu.VMEM((ns, 2 * N_EXPERTS, 128), jnp.int32)],
        compiler_params=pltpu.CompilerParams(dimension_semantics=("parallel", "arbitrary"),
                                             vmem_limit_bytes=VMEM_LIMIT),
        name="moe",
    )(tok, logt, x1, utri, w["wg"], w["wu"], w["wd"], w["fw"])


def _prep_weights(norm_mix_w, w_in, sgu_ln_w, sgu_ln_b, sgu_w_s, sgu_b, rwkv_mu, w_lora_up, w_bias,
                  a_lora_up, a_bias, g_lora_up, k_k, k_a, r_k, lnx_w, lnx_b, proj_a, proj_b, w_out,
                  norm_ffn_w, router_group, router_group_bias, router_expert, router_expert_bias,
                  moe_w_gate, moe_w_up, moe_w_down, norm_final_w):
    row = lambda z: z.reshape(1, -1).astype(F32)
    causal = jnp.tril(jnp.ones((CHUNK, CHUNK), dtype=bool))
    zeros_lora = jnp.zeros((64, D_MODEL), F32)
    head = jnp.arange(256) // B_HEAD_DIM
    ones_bd = (head[:, None] == head[None, :]).astype(BF16)
    pad = EXPERT_ROW0 - N_GROUPS
    rw = jnp.transpose(jnp.concatenate([router_group, jnp.zeros((D_MODEL, pad), F32), router_expert],
                                       axis=1))
    rw_hi = rw.astype(BF16)
    return {
        "nw": row(norm_mix_w), "win": w_in.astype(BF16), "lnw": row(sgu_ln_w), "lnb": row(sgu_ln_b),
        "wsc": jnp.where(causal[None], sgu_w_s, 0.0).astype(BF16),
        "bsb": jnp.repeat(jnp.transpose(sgu_b), A_HEAD_DIM, axis=1),
        "ws0": jnp.repeat(sgu_w_s[:, 0, 0], A_HEAD_DIM).reshape(1, -1),
        "bs0": jnp.repeat(sgu_b[:, 0], A_HEAD_DIM).reshape(1, -1),
        "mu": row(rwkv_mu),
        "wl": jnp.concatenate([w_lora_up, zeros_lora], axis=0).astype(BF16), "wb": row(w_bias),
        "al": jnp.concatenate([zeros_lora, a_lora_up], axis=0).astype(BF16), "ab": row(a_bias),
        "gl": g_lora_up.astype(BF16), "kk": row(k_k), "ka": row(k_a), "rk": row(r_k),
        "pa": proj_a.astype(BF16), "ones2": ones_bd,
        "lxw": row(lnx_w), "lxb": row(lnx_b), "pb": proj_b.astype(BF16), "wo": w_out.astype(BF16),
        "nfw": row(norm_ffn_w),
        "rw3": jnp.concatenate([rw_hi, (rw - rw_hi.astype(F32)).astype(BF16), rw_hi], axis=1),
        "rbt": jnp.concatenate([router_group_bias, jnp.zeros((pad,), F32), router_expert_bias]).reshape(-1, 1),
        "wg": moe_w_gate.astype(BF16), "wu": moe_w_up.astype(BF16), "wd": moe_w_down.astype(BF16),
        "fw": row(norm_final_w),
    }


def kernel(x_prompt, x_sample, state_wkv, state_shift, norm_mix_w, w_in, sgu_ln_w, sgu_ln_b, sgu_w_s, sgu_b, rwkv_mu, w_lora_up, w_bias, a_lora_up, a_bias, g_lora_up, k_k, k_a, r_k, lnx_w, lnx_b, proj_a, proj_b, w_out, norm_ffn_w, router_group, router_group_bias, router_expert, router_expert_bias, moe_w_gate, moe_w_up, moe_w_down, norm_final_w):
    layer = [z[0] for z in (norm_mix_w, w_in, sgu_ln_w, sgu_ln_b, sgu_w_s, sgu_b, rwkv_mu, w_lora_up, w_bias,
                            a_lora_up, a_bias, g_lora_up, k_k, k_a, r_k, lnx_w, lnx_b, proj_a, proj_b, w_out,
                            norm_ffn_w, router_group, router_group_bias, router_expert, router_expert_bias,
                            moe_w_gate, moe_w_up, moe_w_down)]
    w = _prep_weights(*layer, norm_final_w)
    bsz, t_len, _ = x_prompt.shape
    n_s = x_sample.shape[0]

    (outa, gateb, g, bonus, r, lw, k, v, kn, bs, last) = _front_prompt(x_prompt, w, 256)
    y, wkv_p = _scan_prompt(r, lw, k, v, kn, bs, 128, N_PAIRS)
    flat = lambda z: z.reshape(bsz * t_len, D_MODEL)
    x1, tok, logt = _back(True, y, flat(bonus), flat(g), flat(gateb), flat(outa), flat(x_prompt), w, 512)
    y_prompt = _moe(tok, logt, x1, w, 512, 4).reshape(bsz, t_len, D_MODEL)

    xs2 = x_sample.reshape(n_s, D_MODEL)
    (outa, gateb, g, bonus, r, lw, k, v, kn, bs, cols_s, vn_s) = _front_sample(xs2, state_shift[0], w)
    y, wkv_s = _scan_sample(r, lw, k, v, kn, bs, state_wkv)
    x1, tok, logt = _back(False, y, bonus, g, gateb, outa, xs2, w, n_s)
    y_sample = _moe(tok, logt, x1, w, n_s, 1).reshape(n_s, 1, D_MODEL)

    return (y_prompt, y_sample, wkv_p[None], last.reshape(1, bsz, SHIFT_WIDTH), wkv_s, cols_s[None],
            vn_s.reshape(1, n_s, 1, A_HEADS, A_HEAD_DIM))
```

```python
import functools
import math

import jax
import jax.numpy as jnp
from jax import lax
from jax.experimental import pallas as pl
from jax.experimental.pallas import tpu as pltpu

F32 = jnp.float32
BF16 = jnp.bfloat16
HIGHEST = lax.Precision.HIGHEST

D_MODEL = 1024
CHUNK = 128
A_HEADS = 8
A_HEAD_DIM = 128
B_HEADS = 16
B_HEAD_DIM = 64
N_PAIRS = B_HEADS // 2
SHIFT_WIDTH = 3328
OFF_V = 1024
OFF_SHIFT = 2048
OFF_GATE_A = OFF_SHIFT + SHIFT_WIDTH
OFF_GATE_B = OFF_GATE_A + D_MODEL
IN_COLS = OFF_GATE_B + D_MODEL
N_GROUPS = 4
EXPERTS_PER_GROUP = 8
N_EXPERTS = 32
D_EXPERT = 256
ROUTER_ROWS = 40
EXPERT_ROW0 = 8
SEG_ALIGN = 16
BRANCH_DTYPE = BF16
BACK_ROWS = 256
MOE_WINDOW = 48
MOE_EXPERTS_PER_STEP = 4
RMS_EPS = 1e-6
LN_EPS = 1e-5
GN_EPS = 64e-5

SCAN_CHUNK = 64
FRONT_BLOCK = 256
VMEM_LIMIT = 56 * 1024 * 1024


def _gelu(x):
    return x * (0.5 * (1.0 + jnp.tanh(math.sqrt(2.0 / math.pi) * (x + 0.044715 * (x * x * x)))))


def _sigmoid(x):
    return 1.0 / (1.0 + jnp.exp(-x))


def _softplus(z):
    return jnp.maximum(z, 0.0) + jnp.log(1.0 + jnp.exp(-jnp.abs(z)))


def _rms_norm(x, g):
    return x * lax.rsqrt(jnp.mean(x * x, axis=-1, keepdims=True) + RMS_EPS) * g


def _dot(a, b):
    return jnp.dot(a, b, preferred_element_type=F32)


def _dot_hi(a, b):
    return jnp.dot(a, b, preferred_element_type=F32, precision=HIGHEST)


def _head_sums(z, ones2):
    outs = [_dot(z[:, p * 256:(p + 1) * 256].astype(BF16), ones2) for p in range(z.shape[1] // 256)]
    return outs[0] if len(outs) == 1 else jnp.concatenate(outs, axis=1)


def _front_kernel(is_sample, tm, *refs):
    if is_sample:
        (x_ref, prev_ref, nw_ref, win_ref, lnw_ref, lnb_ref, ws0_ref, bs0_ref, mu_ref, wl_ref, wb_ref,
         al_ref, ab_ref, gl_ref, kk_ref, ka_ref, rk_ref, pa_ref, ones2_ref,
         outa_ref, gateb_ref, g_ref, bonus_ref, r_ref, lw_ref, k_ref, v_ref, kn_ref, bs_ref,
         cols_ref, vn_ref, ya_ref) = refs
    else:
        (x_ref, nw_ref, win_ref, lnw_ref, lnb_ref, wsc_ref, bsb_ref, mu_ref, wl_ref, wb_ref,
         al_ref, ab_ref, gl_ref, kk_ref, ka_ref, rk_ref, pa_ref, ones2_ref,
         outa_ref, gateb_ref, g_ref, bonus_ref, r_ref, lw_ref, k_ref, v_ref, kn_ref, bs_ref,
         last_ref, ya_ref, carry_ref) = refs

        @pl.when(pl.program_id(1) == 0)
        def _():
            carry_ref[...] = jnp.zeros_like(carry_ref)

    xb = _rms_norm(x_ref[...], nw_ref[...]).astype(BF16)
    ones2 = ones2_ref[...]

    def proj(lo, width):
        return _dot(xb, win_ref[:, lo:lo + width])

    def shifted(lo, width, cols=None):
        cs = slice(lo, lo + width)
        if cols is None:
            cols = proj(OFF_SHIFT + lo, width)
        if is_sample:
            prev = prev_ref[:, cs]
            cols_ref[:, cs] = cols
        else:
            row = lax.broadcasted_iota(jnp.int32, cols.shape, 0)
            prev = jnp.where(row == 0, carry_ref[0:1, cs], pltpu.roll(cols, 1, 0))
            carry_ref[0:1, cs] = cols[tm - 1:tm, :]
            last_ref[:, cs] = cols[tm - 1:tm, :]
        return cols + (prev - cols) * mu_ref[:, cs]

    tail = shifted(3 * D_MODEL, 256)
    wa = tail[:, 0:128]
    twa = jnp.tanh(wa).astype(BF16)
    wab = wa.astype(BF16)
    sgd = _sigmoid(tail[:, 128:256]).astype(BF16)

    def block_matmuls(j):
        lo = j * FRONT_BLOCK
        cs = slice(lo, lo + FRONT_BLOCK)
        return (proj(OFF_SHIFT + lo, FRONT_BLOCK), proj(OFF_SHIFT + D_MODEL + lo, FRONT_BLOCK),
                proj(OFF_SHIFT + 2 * D_MODEL + lo, FRONT_BLOCK), _dot(twa, wl_ref[:, cs]), _dot(wab, al_ref[:, cs]),
                _dot(sgd, gl_ref[:, cs]), proj(lo, FRONT_BLOCK), proj(OFF_V + lo, FRONT_BLOCK),
                proj(OFF_GATE_B + lo, FRONT_BLOCK))

    n_blocks = D_MODEL // FRONT_BLOCK
    raw = block_matmuls(0)
    gate_a_raw = None
    for j in range(n_blocks):
        lo = j * FRONT_BLOCK
        cs = slice(lo, lo + FRONT_BLOCK)
        r_raw, k_raw, v_raw, lw_dot, a_dot, g_dot, u_raw, va_raw, gb_raw = raw
        if j + 1 < n_blocks:
            raw = block_matmuls(j + 1)
        else:
            gate_a_raw = proj(OFF_GATE_A, D_MODEL)

        r = shifted(lo, FRONT_BLOCK, r_raw)
        k = shifted(D_MODEL + lo, FRONT_BLOCK, k_raw)
        vb = shifted(2 * D_MODEL + lo, FRONT_BLOCK, v_raw)
        w_log = -_softplus(-(wb_ref[:, cs] + lw_dot)) - 0.5
        logw = -jnp.exp(w_log)
        a = _sigmoid(ab_ref[:, cs] + a_dot)
        g_ref[:, cs] = g_dot.astype(g_ref.dtype)
        kk = k * kk_ref[:, cs]
        kkn = kk / jnp.maximum(jnp.sqrt(_head_sums(kk * kk, ones2)), 1e-12)
        k2 = k * (1.0 + (a - 1.0) * ka_ref[:, cs])
        bonus_ref[:, cs] = (_head_sums(r * k2 * rk_ref[:, cs], ones2) * vb).astype(bonus_ref.dtype)
        bsc = kkn * a
        outs = ((r_ref, r), (lw_ref, logw), (k_ref, k2), (v_ref, vb), (kn_ref, kkn), (bs_ref, bsc))
        for o_ref, val in outs:
            if is_sample:
                o_ref[:, cs] = val
            else:
                for q in range(FRONT_BLOCK // 128):
                    o_ref[lo // 128 + q] = val[:, q * 128:(q + 1) * 128]

        u2 = _gelu(u_raw)
        v2 = _gelu(va_raw)
        vnb = []
        for q in range(FRONT_BLOCK // A_HEAD_DIM):
            h = lo // A_HEAD_DIM + q
            hs = slice(h * A_HEAD_DIM, (h + 1) * A_HEAD_DIM)
            u = u2[:, q * A_HEAD_DIM:(q + 1) * A_HEAD_DIM]
            v = v2[:, q * A_HEAD_DIM:(q + 1) * A_HEAD_DIM]
            mean = jnp.mean(v, axis=-1, keepdims=True)
            vc = v - mean
            var = jnp.mean(vc * vc, axis=-1, keepdims=True)
            vn = vc * lax.rsqrt(var + LN_EPS) * lnw_ref[:, hs] + lnb_ref[:, hs]
            if is_sample:
                vn_ref[:, hs] = vn
                ya_ref[:, hs] = (u * (vn * ws0_ref[:, hs] + bs0_ref[:, hs])).astype(BF16)
            else:
                vnb.append(vn.astype(BF16))
        if not is_sample:
            h0 = lo // A_HEAD_DIM
            w_cat = jnp.concatenate([wsc_ref[h0], wsc_ref[h0 + 1]], axis=1)
            zero = jnp.zeros((CHUNK, A_HEAD_DIM), BF16)
            for c in range(tm // CHUNK):
                rs = slice(c * CHUNK, (c + 1) * CHUNK)
                v_bd = jnp.concatenate([jnp.concatenate([vnb[0][rs, :], zero], axis=1),
                                        jnp.concatenate([zero, vnb[1][rs, :]], axis=1)], axis=0)
                ya_ref[rs, cs] = (u2[rs, :] * (_dot(w_cat, v_bd) + bsb_ref[:, cs])).astype(BF16)

        gateb_ref[:, cs] = _sigmoid(gb_raw).astype(gateb_ref.dtype)

    outa_ref[...] = (_sigmoid(gate_a_raw) * _dot(ya_ref[...], pa_ref[...])).astype(outa_ref.dtype)


def _const_spec(shape, grid_rank):
    zeros = (0,) * len(shape)
    if grid_rank == 1:
        return pl.BlockSpec(shape, lambda i: zeros, pipeline_mode=pl.Buffered(1))
    return pl.BlockSpec(shape, lambda b, i: zeros, pipeline_mode=pl.Buffered(1))


def _front_prompt(x, w, tm):
    bsz, t_len, _ = x.shape
    nt = t_len // tm
    consts = [w["nw"], w["win"], w["lnw"], w["lnb"], w["wsc"], w["bsb"], w["mu"], w["wl"], w["wb"],
              w["al"], w["ab"], w["gl"], w["kk"], w["ka"], w["rk"], w["pa"], w["ones2"]]
    tok = lambda: pl.BlockSpec((None, tm, D_MODEL), lambda b, i: (b, i, 0))
    pair = lambda: pl.BlockSpec((None, N_PAIRS, tm, 128), lambda b, i: (b, 0, i, 0))
    tok_shape = jax.ShapeDtypeStruct((bsz, t_len, D_MODEL), F32)
    pair_shape = jax.ShapeDtypeStruct((bsz, N_PAIRS, t_len, 128), F32)
    return pl.pallas_call(
        functools.partial(_front_kernel, False, tm),
        grid=(bsz, nt),
        in_specs=[tok()] + [_const_spec(c.shape, 2) for c in consts],
        out_specs=[tok(), tok(), tok(), tok()] + [pair() for _ in range(6)]
        + [pl.BlockSpec((None, 1, SHIFT_WIDTH), lambda b, i: (b, 0, 0))],
        out_shape=[jax.ShapeDtypeStruct(tok_shape.shape, BRANCH_DTYPE)] * 4 + [pair_shape] * 6
        + [jax.ShapeDtypeStruct((bsz, 1, SHIFT_WIDTH), F32)],
        scratch_shapes=[pltpu.VMEM((tm, D_MODEL), BF16), pltpu.VMEM((8, SHIFT_WIDTH), F32)],
        compiler_params=pltpu.CompilerParams(dimension_semantics=("parallel", "arbitrary"),
                                             vmem_limit_bytes=VMEM_LIMIT),
        name="front_prompt",
    )(x, *consts)


def _front_sample(x, prev, w):
    n = x.shape[0]
    consts = [w["nw"], w["win"], w["lnw"], w["lnb"], w["ws0"], w["bs0"], w["mu"], w["wl"], w["wb"],
              w["al"], w["ab"], w["gl"], w["kk"], w["ka"], w["rk"], w["pa"], w["ones2"]]
    tok = lambda: pl.BlockSpec((n, D_MODEL), lambda i: (0, 0))
    wide = lambda: pl.BlockSpec((n, SHIFT_WIDTH), lambda i: (0, 0))
    tok_shape = jax.ShapeDtypeStruct((n, D_MODEL), F32)
    return pl.pallas_call(
        functools.partial(_front_kernel, True, n),
        grid=(1,),
        in_specs=[tok(), wide()] + [_const_spec(c.shape, 1) for c in consts],
        out_specs=[tok() for _ in range(10)] + [wide(), tok()],
        out_shape=[jax.ShapeDtypeStruct(tok_shape.shape, BRANCH_DTYPE)] * 4 + [tok_shape] * 6
        + [jax.ShapeDtypeStruct((n, SHIFT_WIDTH), F32), tok_shape],
        scratch_shapes=[pltpu.VMEM((n, D_MODEL), BF16)],
        compiler_params=pltpu.CompilerParams(dimension_semantics=("arbitrary",),
                                             vmem_limit_bytes=VMEM_LIMIT),
        name="front_sample",
    )(x, prev, *consts)


def _split3(x):
    hi = x.astype(BF16)
    r1 = x - hi.astype(F32)
    mid = r1.astype(BF16)
    lo = (r1 - mid.astype(F32)).astype(BF16)
    return hi, mid, lo


def _scan_kernel(tt, npp, r_ref, lw_ref, k_ref, v_ref, kn_ref, bs_ref, y_ref, sout_ref, st_ref):
    c_len = SCAN_CHUNK
    i = pl.program_id(2)

    @pl.when(i == 0)
    def _():
        st_ref[...] = jnp.zeros_like(st_ref)

    lane = lax.broadcasted_iota(jnp.int32, (c_len, 128), 1)
    head0 = lane < B_HEAD_DIM
    m = 2 * c_len
    tpos = jnp.bitwise_and(lax.broadcasted_iota(jnp.int32, (m, m), 0), c_len - 1)
    spos = jnp.bitwise_and(lax.broadcasted_iota(jnp.int32, (m, m), 1), c_len - 1)
    strict = tpos > spos
    incl = tpos >= spos
    eye = lax.broadcasted_iota(jnp.int32, (128, 128), 0) == lax.broadcasted_iota(jnp.int32, (128, 128), 1)
    rc = lax.broadcasted_iota(jnp.int32, (c_len, 3 * c_len), 0)
    cc = jnp.bitwise_and(lax.broadcasted_iota(jnp.int32, (c_len, 3 * c_len), 1), c_len - 1)
    tri3 = jnp.where(rc >= cc, 1.0, 0.0).astype(BF16)

    def pair_diag(z):
        return jnp.concatenate([jnp.where(head0, z, 0.0), jnp.where(head0, 0.0, z)], axis=0).astype(BF16)

    def bdot(a, b):
        return _dot(a.astype(BF16), b.astype(BF16))

    n_chunks = tt // c_len
    insts = [(c, q) for c in range(n_chunks) for q in range(npp)]
    rows = lambda c: slice(c * c_len, (c + 1) * c_len)

    log_p = {}
    for c, q in insts:
        log_p[c, q] = _dot(tri3, jnp.concatenate(_split3(lw_ref[q, rows(c), :]), axis=0))

    ops, gram = {}, {}
    for c, q in insts:
        lp = log_p[c, q]
        kn = kn_ref[q, rows(c), :]
        bs = bs_ref[q, rows(c), :]
        kx = k_ref[q, rows(c), :]
        log_pc = lp[c_len - 1:c_len, :]
        e_inv = jnp.exp(-lp)
        e_dec = jnp.exp(log_pc - lp)
        a_m = pair_diag(-kn * jnp.exp(lp - lw_ref[q, rows(c), :]))
        r_m = pair_diag(r_ref[q, rows(c), :] * jnp.exp(lp))
        v_m = pair_diag(v_ref[q, rows(c), :])
        btkt = jnp.concatenate([pair_diag(bs * e_dec), pair_diag(kx * e_dec)], axis=0)
        pc_col = jnp.sum(jnp.where(eye, jnp.exp(log_pc), 0.0), axis=1, keepdims=True)
        ops[c, q] = (a_m, r_m, v_m, btkt, pc_col)
        gram[c, q] = lax.dot_general(jnp.concatenate([a_m, r_m], axis=0),
                                     jnp.concatenate([pair_diag(bs * e_inv), pair_diag(kx * e_inv)], axis=0),
                                     (((1,), (1,)), ((), ())), preferred_element_type=F32)

    n_sum, pw, l_r, lakv = {}, {}, {}, {}
    for c, q in insts:
        g = gram[c, q]
        l_ab = jnp.where(strict, g[0:m, 0:m], 0.0)
        n_sum[c, q] = l_ab
        pw[c, q] = l_ab.astype(BF16)
        l_r[c, q] = jnp.concatenate([jnp.where(incl, g[m:2 * m, 0:m], 0.0),
                                     jnp.where(incl, g[m:2 * m, m:2 * m], 0.0)], axis=1).astype(BF16)
        lakv[c, q] = _dot(jnp.where(strict, g[0:m, m:2 * m], 0.0).astype(BF16), ops[c, q][2])

    pw_f = {}
    for c, q in insts:
        pw_f[c, q] = _dot(pw[c, q], pw[c, q])
        pw[c, q] = pw_f[c, q].astype(BF16)
    for _ in range(int(math.log2(c_len)) - 2):
        prod = {}
        for c, q in insts:
            prod[c, q] = _dot(pw[c, q], jnp.concatenate([pw[c, q], n_sum[c, q].astype(BF16)], axis=1))
        for c, q in insts:
            n_sum[c, q] = n_sum[c, q] + pw_f[c, q] + prod[c, q][:, m:2 * m]
            pw_f[c, q] = prod[c, q][:, 0:m]
            pw[c, q] = pw_f[c, q].astype(BF16)
    for c, q in insts:
        n_sum[c, q] = n_sum[c, q] + pw_f[c, q] + bdot(pw[c, q], n_sum[c, q])

    wu = {}
    for c, q in insts:
        rhs = jnp.concatenate([ops[c, q][0].astype(F32), lakv[c, q]], axis=1)
        wu[c, q] = rhs + bdot(n_sum[c, q], rhs)

    for c in range(n_chunks):
        xs, st = {}, {}
        for q in range(npp):
            st[q] = st_ref[q]
            xs[q] = bdot(jnp.concatenate([wu[c, q][:, 0:128].astype(BF16), ops[c, q][1]], axis=0), st[q])
        for q in range(npp):
            _, _, v_m, btkt, pc_col = ops[c, q]
            uv = jnp.concatenate([(xs[q][0:m, :] + wu[c, q][:, 128:256]).astype(BF16), v_m], axis=0)
            y_m = xs[q][m:2 * m, :] + _dot(l_r[c, q], uv)
            y_ref[q, rows(c), :] = y_m[0:c_len, :] + y_m[c_len:m, :]
            st_ref[q] = pc_col * st[q] + lax.dot_general(btkt, uv, (((0,), (0,)), ((), ())),
                                                         preferred_element_type=F32)

    @pl.when(i == pl.num_programs(2) - 1)
    def _():
        n = B_HEAD_DIM
        for q in range(npp):
            full_t = st_ref[q].T
            sout_ref[2 * q] = full_t[0:n, 0:n]
            sout_ref[2 * q + 1] = full_t[n:128, n:128]


def _scan_prompt(r, lw, k, v, kn, bs, tt, npp):
    bsz, n_pairs, t_len, _ = r.shape
    blk = lambda: pl.BlockSpec((None, npp, tt, 128), lambda b, p, i: (b, p, i, 0))
    return pl.pallas_call(
        functools.partial(_scan_kernel, tt, npp),
        grid=(bsz, n_pairs // npp, t_len // tt),
        in_specs=[blk() for _ in range(6)],
        out_specs=[blk(), pl.BlockSpec((None, 2 * npp, B_HEAD_DIM, B_HEAD_DIM), lambda b, p, i: (b, p, 0, 0))],
        out_shape=[jax.ShapeDtypeStruct((bsz, n_pairs, t_len, 128), F32),
                   jax.ShapeDtypeStruct((bsz, 2 * n_pairs, B_HEAD_DIM, B_HEAD_DIM), F32)],
        scratch_shapes=[pltpu.VMEM((npp, 128, 128), F32)],
        compiler_params=pltpu.CompilerParams(dimension_semantics=("parallel", "parallel", "arbitrary"),
                                             vmem_limit_bytes=VMEM_LIMIT),
        name="scan_prompt",
    )(r, lw, k, v, kn, bs)


def _step_kernel(r_ref, lw_ref, k_ref, v_ref, kn_ref, bs_ref, s_ref, y_ref, sout_ref):
    a = -kn_ref[...]
    w = jnp.exp(lw_ref[...])
    kx = k_ref[...]
    r = r_ref[...]
    b = bs_ref[...]
    ys = []
    for i in range(B_HEAD_DIM):
        s0 = s_ref[i]
        sa = jnp.sum(s0 * a, axis=0, keepdims=True)
        s1 = s0 * w + sa * b + v_ref[i:i + 1, :] * kx
        sout_ref[i] = s1
        ys.append(jnp.sum(s1 * r, axis=0, keepdims=True))
    y_ref[...] = jnp.concatenate(ys, axis=0)


def _scan_sample(r, lw, k, v, kn, bs, s0):
    n = r.shape[0]
    vecs = [jnp.transpose(z) for z in (r, lw, k, v, kn, bs)]
    s0_t = jnp.transpose(s0[0], (1, 2, 3, 0))
    vec = lambda: pl.BlockSpec((B_HEAD_DIM, n), lambda h: (h, 0))
    st = lambda: pl.BlockSpec((None, B_HEAD_DIM, B_HEAD_DIM, n), lambda h: (h, 0, 0, 0))
    y_t, s1_t = pl.pallas_call(
        _step_kernel,
        grid=(B_HEADS,),
        in_specs=[vec() for _ in range(6)] + [st()],
        out_specs=[vec(), st()],
        out_shape=[jax.ShapeDtypeStruct((D_MODEL, n), F32), jax.ShapeDtypeStruct(s0_t.shape, F32)],
        compiler_params=pltpu.CompilerParams(dimension_semantics=("parallel",)),
        name="scan_sample",
    )(*vecs, s0_t)
    return jnp.transpose(y_t), jnp.transpose(s1_t, (3, 0, 1, 2))[None]


def _back_kernel(paired, y_ref, bonus_ref, g_ref, gateb_ref, outa_ref, x_ref, lxw_ref, lxb_ref, pb_ref,
                 wo_ref, nfw_ref, rw_ref, rb_ref, ones2_ref, x1_ref, tok_ref, logt_ref):
    tm = x_ref.shape[0]
    n_parts = max(1, tm // BACK_ROWS)
    ones2 = ones2_ref[...]
    inv_n = 1.0 / B_HEAD_DIM
    parts = range(n_parts)
    rows = lambda i: slice(i * (tm // n_parts), (i + 1) * (tm // n_parts))

    if paired:
        y = [jnp.concatenate([y_ref[p, rows(i), :] for p in range(N_PAIRS)], axis=1) for i in parts]
    else:
        y = [y_ref[rows(i), :] for i in parts]
    mean = [_head_sums(y[i], ones2) * inv_n for i in parts]
    yc = [y[i] - mean[i] for i in parts]
    var = [_head_sums(yc[i] * yc[i], ones2) * inv_n for i in parts]
    f32_rows = lambda ref, i: ref[rows(i), :].astype(F32)
    yb = [((yc[i] * lax.rsqrt(var[i] + GN_EPS) * lxw_ref[...] + lxb_ref[...] + f32_rows(bonus_ref, i))
           * f32_rows(g_ref, i)).astype(BF16) for i in parts]
    mb = [_dot(yb[i], pb_ref[...]) for i in parts]
    merged = [(f32_rows(outa_ref, i) + f32_rows(gateb_ref, i) * mb[i]).astype(BF16) for i in parts]
    out = [_dot(merged[i], wo_ref[...]) for i in parts]
    tok = []
    for i in parts:
        x1 = x_ref[rows(i), :] + out[i]
        x1_ref[rows(i), :] = x1
        tok.append(_rms_norm(x1, nfw_ref[...]))
        tok_ref[rows(i), :] = tok[i].astype(BF16)
    for i in parts:
        t_hi = tok[i].astype(BF16)
        t_lo = (tok[i] - t_hi.astype(F32)).astype(BF16)
        logt_ref[:, rows(i)] = lax.dot_general(
            rw_ref[...], jnp.concatenate([t_hi, t_hi, t_lo], axis=1), (((1,), (1,)), ((), ())),
            preferred_element_type=F32) + rb_ref[...]


def _back(paired, y, bonus, g, gateb, outa, x, w, tm):
    n = x.shape[0]
    consts = [w["lxw"], w["lxb"], w["pb"], w["wo"], w["nfw"], w["rw3"], w["rbt"], w["ones2"]]
    tok = lambda: pl.BlockSpec((tm, D_MODEL), lambda i: (i, 0))
    if paired:
        nt = y.shape[2] // tm
        y_spec = pl.BlockSpec((None, N_PAIRS, tm, 128), lambda i: (i // nt, 0, i % nt, 0))
    else:
        y_spec = tok()
    return pl.pallas_call(
        functools.partial(_back_kernel, paired),
        grid=(n // tm,),
        in_specs=[y_spec] + [tok() for _ in range(5)] + [_const_spec(c.shape, 1) for c in consts],
        out_specs=[tok(), tok(), pl.BlockSpec((ROUTER_ROWS, tm), lambda i: (0, i))],
        out_shape=[jax.ShapeDtypeStruct((n, D_MODEL), F32), jax.ShapeDtypeStruct((n, D_MODEL), BF16),
                   jax.ShapeDtypeStruct((ROUTER_ROWS, n), F32)],
        compiler_params=pltpu.CompilerParams(dimension_semantics=("parallel",), vmem_limit_bytes=VMEM_LIMIT),
        name="back_prompt" if paired else "back_sample",
    )(y, bonus, g, gateb, outa, x, *consts)


def _moe_slots(ts):
    n = 2 * ts + N_EXPERTS * (SEG_ALIGN - 1)
    return -(-n // MOE_WINDOW) * MOE_WINDOW


def _split3_f32(x):
    hi = x.astype(BF16).astype(F32)
    mid = (x - hi).astype(BF16).astype(F32)
    lo = (x - hi - mid).astype(BF16).astype(F32)
    return hi, mid, lo


def _moe_kernel(ts, ns, tok_ref, logt_ref, x1_ref, utri_ref, wg_ref, wu_ref, wd_ref, fw_ref, o_ref,
                xy_ref, pg_ref, ws_ref, meta_ref):
    n_slots = 2 * xy_ref.shape[1]
    half_w = MOE_WINDOW // 2
    n_esteps = N_EXPERTS // MOE_EXPERTS_PER_STEP
    step = pl.program_id(1)
    tn = (((1,), (1,)), ((), ()))

    @pl.when(step < ns)
    def _route_and_sort():
        lt = logt_ref[...]
        neg = jnp.float32(-jnp.inf)
        big = jnp.float32(99.0)
        row8 = lax.broadcasted_iota(jnp.int32, (8, ts), 0).astype(F32)
        is_grp = row8 < N_GROUPS
        lg = jnp.where(is_grp, lt[0:8, :], neg)
        gmax = jnp.max(lg, axis=0, keepdims=True)
        grp = jnp.min(jnp.where(lg == gmax, row8, big), axis=0, keepdims=True)
        p_grp = 1.0 / jnp.sum(jnp.where(is_grp, jnp.exp(lg - gmax), 0.0), axis=0, keepdims=True)
        le = lt[EXPERT_ROW0:EXPERT_ROW0 + EXPERTS_PER_GROUP, :]
        for g in range(1, N_GROUPS):
            lo = EXPERT_ROW0 + g * EXPERTS_PER_GROUP
            le = jnp.where(grp == g, lt[lo:lo + EXPERTS_PER_GROUP, :], le)
        top1 = jnp.max(le, axis=0, keepdims=True)
        i1 = jnp.min(jnp.where(le == top1, row8, big), axis=0, keepdims=True)
        le2 = jnp.where(row8 == i1, neg, le)
        top2 = jnp.max(le2, axis=0, keepdims=True)
        i2 = jnp.min(jnp.where(le2 == top2, row8, big), axis=0, keepdims=True)
        e2 = jnp.exp(top2 - top1)
        w1 = p_grp / (1.0 + e2)
        w2 = p_grp * e2 / (1.0 + e2)

        row_e = lax.broadcasted_iota(jnp.int32, (N_EXPERTS, ts), 0).astype(F32)
        a1 = row_e == grp * EXPERTS_PER_GROUP + i1
        a2 = row_e == grp * EXPERTS_PER_GROUP + i2
        at = jnp.where(a1, 1.0, 0.0) + jnp.where(a2, 1.0, 0.0)
        rank = _dot(at.astype(BF16), utri_ref[...])
        cnt = rank[:, ts - 1:ts] + at[:, ts - 1:ts]
        cntp = jnp.floor((cnt + (SEG_ALIGN - 1)) * (1.0 / SEG_ALIGN)) * SEG_ALIGN
        cntp_b = jnp.broadcast_to(cntp, (N_EXPERTS, 128))
        r_e = lax.broadcasted_iota(jnp.int32, (N_EXPERTS, N_EXPERTS), 0)
        c_e = lax.broadcasted_iota(jnp.int32, (N_EXPERTS, N_EXPERTS), 1)
        off = _dot(jnp.where(r_e > c_e, 1.0, 0.0).astype(BF16), cntp_b.astype(BF16))
        slot = off[:, 0:1] + rank
        s1 = jnp.sum(jnp.where(a1, slot, 0.0), axis=0, keepdims=True)
        s2 = jnp.sum(jnp.where(a2, slot, 0.0), axis=0, keepdims=True)
        srow = lax.broadcasted_iota(jnp.int32, (n_slots, ts), 0).astype(F32)
        pg1 = jnp.where(srow == s1, 1.0, 0.0)
        pg2 = jnp.where(srow == s2, 1.0, 0.0)
        pg = (pg1 + pg2).astype(BF16)
        pg_ref[step] = pg
        xy_ref[step] = pltpu.bitcast(_dot(pg, tok_ref[...]).astype(BF16), jnp.int32)

        wrows = jnp.concatenate(
            [jnp.concatenate([p1, p2], axis=1) for p1, p2 in zip(_split3_f32(w1), _split3_f32(w2))]
            + [jnp.zeros((5, 2 * ts), F32)], axis=0).astype(BF16)
        wsl = lax.dot_general(jnp.concatenate([pg1.astype(BF16), pg2.astype(BF16)], axis=1), wrows, tn,
                              preferred_element_type=F32)
        ws_ref[step] = jnp.broadcast_to(wsl[:, 0:1] + wsl[:, 1:2] + wsl[:, 2:3], (n_slots, 128))
        meta_ref[step, 0:N_EXPERTS, :] = off.astype(jnp.int32)
        meta_ref[step, N_EXPERTS:2 * N_EXPERTS, :] = cntp_b.astype(jnp.int32)

    @pl.when((step >= ns) & (step < ns + n_esteps))
    def _experts():
        tile_row = (lax.broadcasted_iota(jnp.int32, (half_w, D_MODEL), 0) // 8) * SEG_ALIGN
        experts = range(MOE_EXPERTS_PER_STEP)
        offs, cnts = {}, {}
        for el in experts:
            e = (step - ns) * MOE_EXPERTS_PER_STEP + el
            offs[el] = [meta_ref[j, pl.ds(e, 1), :][0, 0] for j in range(ns)]
            cnts[el] = [meta_ref[j, pl.ds(N_EXPERTS + e, 1), :][0, 0] for j in range(ns)]
        n_win = functools.reduce(jnp.maximum, [(c + MOE_WINDOW - 1) // MOE_WINDOW
                                               for el in experts for c in cnts[el]])

        def window(w, carry):
            starts, lhs = {}, {}
            for el in experts:
                for j in range(ns):
                    st = jnp.minimum(offs[el][j] + w * MOE_WINDOW, n_slots - MOE_WINDOW)
                    starts[el, j] = pl.multiple_of(st, SEG_ALIGN)
                lhs[el] = jnp.concatenate(
                    [pltpu.bitcast(xy_ref[j, pl.ds(pl.multiple_of(starts[el, j] // 2, 8), half_w), :], BF16)
                     for j in range(ns)], axis=0)
            gate = {el: _dot(lhs[el], wg_ref[el]) for el in experts}
            up = {el: _dot(lhs[el], wu_ref[el]) for el in experts}
            y = {el: _dot((gate[el] * _sigmoid(gate[el]) * up[el]).astype(BF16), wd_ref[el]) for el in experts}
            for el in experts:
                for j in range(ns):
                    rows = starts[el, j] + tile_row
                    own = (rows >= offs[el][j] + w * MOE_WINDOW) & (rows < offs[el][j] + cnts[el][j])
                    yw = (y[el][j * MOE_WINDOW:(j + 1) * MOE_WINDOW, :]
                          * ws_ref[j, pl.ds(starts[el, j], MOE_WINDOW), 0:1])
                    pltpu.store(xy_ref.at[j, pl.ds(pl.multiple_of(starts[el, j] // 2, 8), half_w), :],
                                pltpu.bitcast(yw.astype(BF16), jnp.int32), mask=own)
            return carry

        lax.fori_loop(0, n_win, window, 0)

    @pl.when(step >= ns + n_esteps)
    def _combine():
        j = step - ns - n_esteps
        c = lax.dot_general(pg_ref[j], pltpu.bitcast(xy_ref[j], BF16), (((0,), (0,)), ((), ())),
                            preferred_element_type=F32)
        o_ref[...] = _rms_norm(x1_ref[...] + c, fw_ref[...])


def _moe(tok, logt, x1, w, ts, ns):
    n = tok.shape[0]
    n_slots = _moe_slots(ts)
    n_esteps = N_EXPERTS // MOE_EXPERTS_PER_STEP
    ne = MOE_EXPERTS_PER_STEP
    sub_in = lambda s, k: s * ns + jnp.clip(k, 0, ns - 1)
    sub_out = lambda s, k: s * ns + jnp.clip(k - ns - n_esteps, 0, ns - 1)
    expert = lambda k: jnp.clip(k - ns, 0, n_esteps - 1)
    utri = (jnp.arange(ts)[:, None] < jnp.arange(ts)[None, :]).astype(BF16)
    return pl.pallas_call(
        functools.partial(_moe_kernel, ts, ns),
        grid=(n // (ts * ns), 2 * ns + n_esteps),
        in_specs=[pl.BlockSpec((ts, D_MODEL), lambda s, k: (sub_in(s, k), 0)),
                  pl.BlockSpec((ROUTER_ROWS, ts), lambda s, k: (0, sub_in(s, k))),
                  pl.BlockSpec((ts, D_MODEL), lambda s, k: (sub_out(s, k), 0)),
                  pl.BlockSpec((ts, ts), lambda s, k: (0, 0)),
                  pl.BlockSpec((ne, D_MODEL, D_EXPERT), lambda s, k: (expert(k), 0, 0)),
                  pl.BlockSpec((ne, D_MODEL, D_EXPERT), lambda s, k: (expert(k), 0, 0)),
                  pl.BlockSpec((ne, D_EXPERT, D_MODEL), lambda s, k: (expert(k), 0, 0)),
                  pl.BlockSpec((1, D_MODEL), lambda s, k: (0, 0))],
        out_specs=pl.BlockSpec((ts, D_MODEL), lambda s, k: (sub_out(s, k), 0)),
        out_shape=jax.ShapeDtypeStruct((n, D_MODEL), F32),
        scratch_shapes=[pltpu.VMEM((ns, n_slots // 2, D_MODEL), jnp.int32), pltpu.VMEM((ns, n_slots, ts), BF16),
                        pltpu.VMEM((ns, n_slots, 128), F32), pltpu.VMEM((ns, 2 * N_EXPERTS, 128), jnp.int32)],
        compiler_params=pltpu.CompilerParams(dimension_semantics=("parallel", "arbitrary"),
                                             vmem_limit_bytes=VMEM_LIMIT),
        name="moe",
    )(tok, logt, x1, utri, w["wg"], w["wu"], w["wd"], w["fw"])


def _prep_weights(norm_mix_w, w_in, sgu_ln_w, sgu_ln_b, sgu_w_s, sgu_b, rwkv_mu, w_lora_up, w_bias,
                  a_lora_up, a_bias, g_lora_up, k_k, k_a, r_k, lnx_w, lnx_b, proj_a, proj_b, w_out,
                  norm_ffn_w, router_group, router_group_bias, router_expert, router_expert_bias,
                  moe_w_gate, moe_w_up, moe_w_down, norm_final_w):
    row = lambda z: z.reshape(1, -1).astype(F32)
    causal = jnp.tril(jnp.ones((CHUNK, CHUNK), dtype=bool))
    zeros_lora = jnp.zeros((64, D_MODEL), F32)
    head = jnp.arange(256) // B_HEAD_DIM
    ones_bd = (head[:, None] == head[None, :]).astype(BF16)
    pad = EXPERT_ROW0 - N_GROUPS
    rw = jnp.transpose(jnp.concatenate([router_group, jnp.zeros((D_MODEL, pad), F32), router_expert],
                                       axis=1))
    rw_hi = rw.astype(BF16)
    return {
        "nw": row(norm_mix_w), "win": w_in.astype(BF16), "lnw": row(sgu_ln_w), "lnb": row(sgu_ln_b),
        "wsc": jnp.where(causal[None], sgu_w_s, 0.0).astype(BF16),
        "bsb": jnp.repeat(jnp.transpose(sgu_b), A_HEAD_DIM, axis=1),
        "ws0": jnp.repeat(sgu_w_s[:, 0, 0], A_HEAD_DIM).reshape(1, -1),
        "bs0": jnp.repeat(sgu_b[:, 0], A_HEAD_DIM).reshape(1, -1),
        "mu": row(rwkv_mu),
        "wl": jnp.concatenate([w_lora_up, zeros_lora], axis=0).astype(BF16), "wb": row(w_bias),
        "al": jnp.concatenate([zeros_lora, a_lora_up], axis=0).astype(BF16), "ab": row(a_bias),
        "gl": g_lora_up.astype(BF16), "kk": row(k_k), "ka": row(k_a), "rk": row(r_k),
        "pa": proj_a.astype(BF16), "ones2": ones_bd,
        "lxw": row(lnx_w), "lxb": row(lnx_b), "pb": proj_b.astype(BF16), "wo": w_out.astype(BF16),
        "nfw": row(norm_ffn_w),
        "rw3": jnp.concatenate([rw_hi, (rw - rw_hi.astype(F32)).astype(BF16), rw_hi], axis=1),
        "rbt": jnp.concatenate([router_group_bias, jnp.zeros((pad,), F32), router_expert_bias]).reshape(-1, 1),
        "wg": moe_w_gate.astype(BF16), "wu": moe_w_up.astype(BF16), "wd": moe_w_down.astype(BF16),
        "fw": row(norm_final_w),
    }


def kernel(x_prompt, x_sample, state_wkv, state_shift, norm_mix_w, w_in, sgu_ln_w, sgu_ln_b, sgu_w_s, sgu_b, rwkv_mu, w_lora_up, w_bias, a_lora_up, a_bias, g_lora_up, k_k, k_a, r_k, lnx_w, lnx_b, proj_a, proj_b, w_out, norm_ffn_w, router_group, router_group_bias, router_expert, router_expert_bias, moe_w_gate, moe_w_up, moe_w_down, norm_final_w):
    layer = [z[0] for z in (norm_mix_w, w_in, sgu_ln_w, sgu_ln_b, sgu_w_s, sgu_b, rwkv_mu, w_lora_up, w_bias,
                            a_lora_up, a_bias, g_lora_up, k_k, k_a, r_k, lnx_w, lnx_b, proj_a, proj_b, w_out,
                            norm_ffn_w, router_group, router_group_bias, router_expert, router_expert_bias,
                            moe_w_gate, moe_w_up, moe_w_down)]
    w = _prep_weights(*layer, norm_final_w)
    bsz, t_len, _ = x_prompt.shape
    n_s = x_sample.shape[0]

    (outa, gateb, g, bonus, r, lw, k, v, kn, bs, last) = _front_prompt(x_prompt, w, 256)
    y, wkv_p = _scan_prompt(r, lw, k, v, kn, bs, 128, N_PAIRS)
    flat = lambda z: z.reshape(bsz * t_len, D_MODEL)
    x1, tok, logt = _back(True, y, flat(bonus), flat(g), flat(gateb), flat(outa), flat(x_prompt), w, 512)
    y_prompt = _moe(tok, logt, x1, w, 512, 4).reshape(bsz, t_len, D_MODEL)

    xs2 = x_sample.reshape(n_s, D_MODEL)
    (outa, gateb, g, bonus, r, lw, k, v, kn, bs, cols_s, vn_s) = _front_sample(xs2, state_shift[0], w)
    y, wkv_s = _scan_sample(r, lw, k, v, kn, bs, state_wkv)
    x1, tok, logt = _back(False, y, bonus, g, gateb, outa, xs2, w, n_s)
    y_sample = _moe(tok, logt, x1, w, n_s, 1).reshape(n_s, 1, D_MODEL)

    return (y_prompt, y_sample, wkv_p[None], last.reshape(1, bsz, SHIFT_WIDTH), wkv_s, cols_s[None],
            vn_s.reshape(1, n_s, 1, A_HEADS, A_HEAD_DIM))
```

```python
import functools
import math

import jax
import jax.numpy as jnp
from jax import lax
from jax.experimental import pallas as pl
from jax.experimental.pallas import tpu as pltpu

F32 = jnp.float32
BF16 = jnp.bfloat16
HIGHEST = lax.Precision.HIGHEST

D_MODEL = 1024
CHUNK = 128
A_HEADS = 8
A_HEAD_DIM = 128
B_HEADS = 16
B_HEAD_DIM = 64
N_PAIRS = B_HEADS // 2
SHIFT_WIDTH = 3328
OFF_V = 1024
OFF_SHIFT = 2048
OFF_GATE_A = OFF_SHIFT + SHIFT_WIDTH
OFF_GATE_B = OFF_GATE_A + D_MODEL
IN_COLS = OFF_GATE_B + D_MODEL
N_GROUPS = 4
EXPERTS_PER_GROUP = 8
N_EXPERTS = 32
D_EXPERT = 256
ROUTER_ROWS = 40
EXPERT_ROW0 = 8
SEG_ALIGN = 16
BACK_ROWS = 256
MOE_WINDOW = 48
MOE_EXPERTS_PER_STEP = 4
RMS_EPS = 1e-6
LN_EPS = 1e-5
GN_EPS = 64e-5

SCAN_CHUNK = 64
FRONT_BLOCK = 256
VMEM_LIMIT = 56 * 1024 * 1024


def _gelu(x):
    return x * (0.5 * (1.0 + jnp.tanh(math.sqrt(2.0 / math.pi) * (x + 0.044715 * (x * x * x)))))


def _sigmoid(x):
    return 1.0 / (1.0 + jnp.exp(-x))


def _softplus(z):
    return jnp.maximum(z, 0.0) + jnp.log(1.0 + jnp.exp(-jnp.abs(z)))


def _rms_norm(x, g):
    return x * lax.rsqrt(jnp.mean(x * x, axis=-1, keepdims=True) + RMS_EPS) * g


def _dot(a, b):
    return jnp.dot(a, b, preferred_element_type=F32)


def _dot_hi(a, b):
    return jnp.dot(a, b, preferred_element_type=F32, precision=HIGHEST)


def _head_sums(z, ones2):
    outs = [_dot(z[:, p * 256:(p + 1) * 256].astype(BF16), ones2) for p in range(z.shape[1] // 256)]
    return outs[0] if len(outs) == 1 else jnp.concatenate(outs, axis=1)


def _front_kernel(is_sample, tm, *refs):
    if is_sample:
        (x_ref, prev_ref, nw_ref, win_ref, lnw_ref, lnb_ref, ws0_ref, bs0_ref, mu_ref, wl_ref, wb_ref,
         al_ref, ab_ref, gl_ref, kk_ref, ka_ref, rk_ref, pa_ref, ones2_ref,
         outa_ref, gateb_ref, g_ref, bonus_ref, r_ref, lw_ref, k_ref, v_ref, kn_ref, bs_ref,
         cols_ref, vn_ref, ya_ref) = refs
    else:
        (x_ref, nw_ref, win_ref, lnw_ref, lnb_ref, wsc_ref, bsb_ref, mu_ref, wl_ref, wb_ref,
         al_ref, ab_ref, gl_ref, kk_ref, ka_ref, rk_ref, pa_ref, ones2_ref,
         outa_ref, gateb_ref, g_ref, bonus_ref, r_ref, lw_ref, k_ref, v_ref, kn_ref, bs_ref,
         last_ref, ya_ref, carry_ref) = refs

        @pl.when(pl.program_id(1) == 0)
        def _():
            carry_ref[...] = jnp.zeros_like(carry_ref)

    xb = _rms_norm(x_ref[...], nw_ref[...]).astype(BF16)
    ones2 = ones2_ref[...]

    def proj(lo, width):
        return _dot(xb, win_ref[:, lo:lo + width])

    def shifted(lo, width, cols=None):
        cs = slice(lo, lo + width)
        if cols is None:
            cols = proj(OFF_SHIFT + lo, width)
        if is_sample:
            prev = prev_ref[:, cs]
            cols_ref[:, cs] = cols
        else:
            row = lax.broadcasted_iota(jnp.int32, cols.shape, 0)
            prev = jnp.where(row == 0, carry_ref[0:1, cs], pltpu.roll(cols, 1, 0))
            carry_ref[0:1, cs] = cols[tm - 1:tm, :]
            last_ref[:, cs] = cols[tm - 1:tm, :]
        return cols + (prev - cols) * mu_ref[:, cs]

    tail = shifted(3 * D_MODEL, 256)
    wa = tail[:, 0:128]
    twa = jnp.tanh(wa).astype(BF16)
    wab = wa.astype(BF16)
    sgd = _sigmoid(tail[:, 128:256]).astype(BF16)

    def block_matmuls(j):
        lo = j * FRONT_BLOCK
        cs = slice(lo, lo + FRONT_BLOCK)
        return (proj(OFF_SHIFT + lo, FRONT_BLOCK), proj(OFF_SHIFT + D_MODEL + lo, FRONT_BLOCK),
                proj(OFF_SHIFT + 2 * D_MODEL + lo, FRONT_BLOCK), _dot(twa, wl_ref[:, cs]), _dot(wab, al_ref[:, cs]),
                _dot(sgd, gl_ref[:, cs]), proj(lo, FRONT_BLOCK), proj(OFF_V + lo, FRONT_BLOCK),
                proj(OFF_GATE_B + lo, FRONT_BLOCK))

    n_blocks = D_MODEL // FRONT_BLOCK
    raw = block_matmuls(0)
    gate_a_raw = None
    for j in range(n_blocks):
        lo = j * FRONT_BLOCK
        cs = slice(lo, lo + FRONT_BLOCK)
        r_raw, k_raw, v_raw, lw_dot, a_dot, g_dot, u_raw, va_raw, gb_raw = raw
        if j + 1 < n_blocks:
            raw = block_matmuls(j + 1)
        else:
            gate_a_raw = proj(OFF_GATE_A, D_MODEL)

        r = shifted(lo, FRONT_BLOCK, r_raw)
        k = shifted(D_MODEL + lo, FRONT_BLOCK, k_raw)
        vb = shifted(2 * D_MODEL + lo, FRONT_BLOCK, v_raw)
        w_log = -_softplus(-(wb_ref[:, cs] + lw_dot)) - 0.5
        logw = -jnp.exp(w_log)
        a = _sigmoid(ab_ref[:, cs] + a_dot)
        g_ref[:, cs] = g_dot
        kk = k * kk_ref[:, cs]
        kkn = kk / jnp.maximum(jnp.sqrt(_head_sums(kk * kk, ones2)), 1e-12)
        k2 = k * (1.0 + (a - 1.0) * ka_ref[:, cs])
        bonus_ref[:, cs] = _head_sums(r * k2 * rk_ref[:, cs], ones2) * vb
        bsc = kkn * a
        outs = ((r_ref, r), (lw_ref, logw), (k_ref, k2), (v_ref, vb), (kn_ref, kkn), (bs_ref, bsc))
        for o_ref, val in outs:
            if is_sample:
                o_ref[:, cs] = val
            else:
                for q in range(FRONT_BLOCK // 128):
                    o_ref[lo // 128 + q] = val[:, q * 128:(q + 1) * 128]

        u2 = _gelu(u_raw)
        v2 = _gelu(va_raw)
        vnb = []
        for q in range(FRONT_BLOCK // A_HEAD_DIM):
            h = lo // A_HEAD_DIM + q
            hs = slice(h * A_HEAD_DIM, (h + 1) * A_HEAD_DIM)
            u = u2[:, q * A_HEAD_DIM:(q + 1) * A_HEAD_DIM]
            v = v2[:, q * A_HEAD_DIM:(q + 1) * A_HEAD_DIM]
            mean = jnp.mean(v, axis=-1, keepdims=True)
            vc = v - mean
            var = jnp.mean(vc * vc, axis=-1, keepdims=True)
            vn = vc * lax.rsqrt(var + LN_EPS) * lnw_ref[:, hs] + lnb_ref[:, hs]
            if is_sample:
                vn_ref[:, hs] = vn
                ya_ref[:, hs] = (u * (vn * ws0_ref[:, hs] + bs0_ref[:, hs])).astype(BF16)
            else:
                vnb.append(vn.astype(BF16))
        if not is_sample:
            h0 = lo // A_HEAD_DIM
            w_cat = jnp.concatenate([wsc_ref[h0], wsc_ref[h0 + 1]], axis=1)
            zero = jnp.zeros((CHUNK, A_HEAD_DIM), BF16)
            for c in range(tm // CHUNK):
                rs = slice(c * CHUNK, (c + 1) * CHUNK)
                v_bd = jnp.concatenate([jnp.concatenate([vnb[0][rs, :], zero], axis=1),
                                        jnp.concatenate([zero, vnb[1][rs, :]], axis=1)], axis=0)
                ya_ref[rs, cs] = (u2[rs, :] * (_dot(w_cat, v_bd) + bsb_ref[:, cs])).astype(BF16)

        gateb_ref[:, cs] = _sigmoid(gb_raw)

    outa_ref[...] = _sigmoid(gate_a_raw) * _dot(ya_ref[...], pa_ref[...])


def _const_spec(shape, grid_rank):
    zeros = (0,) * len(shape)
    if grid_rank == 1:
        return pl.BlockSpec(shape, lambda i: zeros, pipeline_mode=pl.Buffered(1))
    return pl.BlockSpec(shape, lambda b, i: zeros, pipeline_mode=pl.Buffered(1))


def _front_prompt(x, w, tm):
    bsz, t_len, _ = x.shape
    nt = t_len // tm
    consts = [w["nw"], w["win"], w["lnw"], w["lnb"], w["wsc"], w["bsb"], w["mu"], w["wl"], w["wb"],
              w["al"], w["ab"], w["gl"], w["kk"], w["ka"], w["rk"], w["pa"], w["ones2"]]
    tok = lambda: pl.BlockSpec((None, tm, D_MODEL), lambda b, i: (b, i, 0))
    pair = lambda: pl.BlockSpec((None, N_PAIRS, tm, 128), lambda b, i: (b, 0, i, 0))
    tok_shape = jax.ShapeDtypeStruct((bsz, t_len, D_MODEL), F32)
    pair_shape = jax.ShapeDtypeStruct((bsz, N_PAIRS, t_len, 128), F32)
    return pl.pallas_call(
        functools.partial(_front_kernel, False, tm),
        grid=(bsz, nt),
        in_specs=[tok()] + [_const_spec(c.shape, 2) for c in consts],
        out_specs=[tok(), tok(), tok(), tok()] + [pair() for _ in range(6)]
        + [pl.BlockSpec((None, 1, SHIFT_WIDTH), lambda b, i: (b, 0, 0))],
        out_shape=[tok_shape] * 4 + [pair_shape] * 6 + [jax.ShapeDtypeStruct((bsz, 1, SHIFT_WIDTH), F32)],
        scratch_shapes=[pltpu.VMEM((tm, D_MODEL), BF16), pltpu.VMEM((8, SHIFT_WIDTH), F32)],
        compiler_params=pltpu.CompilerParams(dimension_semantics=("parallel", "arbitrary"),
                                             vmem_limit_bytes=VMEM_LIMIT),
        name="front_prompt",
    )(x, *consts)


def _front_sample(x, prev, w):
    n = x.shape[0]
    consts = [w["nw"], w["win"], w["lnw"], w["lnb"], w["ws0"], w["bs0"], w["mu"], w["wl"], w["wb"],
              w["al"], w["ab"], w["gl"], w["kk"], w["ka"], w["rk"], w["pa"], w["ones2"]]
    tok = lambda: pl.BlockSpec((n, D_MODEL), lambda i: (0, 0))
    wide = lambda: pl.BlockSpec((n, SHIFT_WIDTH), lambda i: (0, 0))
    tok_shape = jax.ShapeDtypeStruct((n, D_MODEL), F32)
    return pl.pallas_call(
        functools.partial(_front_kernel, True, n),
        grid=(1,),
        in_specs=[tok(), wide()] + [_const_spec(c.shape, 1) for c in consts],
        out_specs=[tok() for _ in range(10)] + [wide(), tok()],
        out_shape=[tok_shape] * 10 + [jax.ShapeDtypeStruct((n, SHIFT_WIDTH), F32), tok_shape],
        scratch_shapes=[pltpu.VMEM((n, D_MODEL), BF16)],
        compiler_params=pltpu.CompilerParams(dimension_semantics=("arbitrary",),
                                             vmem_limit_bytes=VMEM_LIMIT),
        name="front_sample",
    )(x, prev, *consts)


def _split3(x):
    hi = x.astype(BF16)
    r1 = x - hi.astype(F32)
    mid = r1.astype(BF16)
    lo = (r1 - mid.astype(F32)).astype(BF16)
    return hi, mid, lo


def _scan_kernel(tt, npp, r_ref, lw_ref, k_ref, v_ref, kn_ref, bs_ref, y_ref, sout_ref, st_ref):
    c_len = SCAN_CHUNK
    i = pl.program_id(2)

    @pl.when(i == 0)
    def _():
        st_ref[...] = jnp.zeros_like(st_ref)

    lane = lax.broadcasted_iota(jnp.int32, (c_len, 128), 1)
    head0 = lane < B_HEAD_DIM
    tpos = lax.broadcasted_iota(jnp.int32, (c_len, 128), 0)
    spos = jnp.bitwise_and(lane, c_len - 1)
    strict = tpos > spos
    incl = tpos >= spos
    row2 = lax.broadcasted_iota(jnp.int32, (128, 128), 0)
    col2 = lax.broadcasted_iota(jnp.int32, (128, 128), 1)
    eye = row2 == col2
    same_head = (row2 < B_HEAD_DIM) == (col2 < B_HEAD_DIM)
    rc = lax.broadcasted_iota(jnp.int32, (c_len, 3 * c_len), 0)
    cc = jnp.bitwise_and(lax.broadcasted_iota(jnp.int32, (c_len, 3 * c_len), 1), c_len - 1)
    tri3 = jnp.where(rc >= cc, 1.0, 0.0).astype(BF16)

    def pair_diag(z):
        return jnp.concatenate([jnp.where(head0, z, 0.0), jnp.where(head0, 0.0, z)], axis=0).astype(BF16)

    n_chunks = tt // c_len
    insts = [(c, q) for c in range(n_chunks) for q in range(npp)]
    rows = lambda c: slice(c * c_len, (c + 1) * c_len)

    log_p = {}
    for c, q in insts:
        log_p[c, q] = _dot(tri3, jnp.concatenate(_split3(lw_ref[q, rows(c), :]), axis=0))

    ops, gram = {}, {}
    for c, q in insts:
        lp = log_p[c, q]
        kn = kn_ref[q, rows(c), :]
        bs = bs_ref[q, rows(c), :]
        kx = k_ref[q, rows(c), :]
        log_pc = lp[c_len - 1:c_len, :]
        e_inv = jnp.exp(-lp)
        e_dec = jnp.exp(log_pc - lp)
        a_s = -kn * jnp.exp(lp - lw_ref[q, rows(c), :])
        r_s = (r_ref[q, rows(c), :] * jnp.exp(lp)).astype(BF16)
        v_s = v_ref[q, rows(c), :]
        btkt = jnp.concatenate([bs * e_dec, kx * e_dec], axis=0).astype(BF16)
        pc_col = jnp.sum(jnp.where(eye, jnp.exp(log_pc), 0.0), axis=1, keepdims=True)
        ops[c, q] = (a_s, r_s, v_s.astype(BF16), pair_diag(v_s), btkt, pc_col)
        gram[c, q] = lax.dot_general(jnp.concatenate([a_s.astype(BF16), r_s], axis=0),
                                     jnp.concatenate([pair_diag(bs * e_inv), pair_diag(kx * e_inv)], axis=0),
                                     (((1,), (1,)), ((), ())), preferred_element_type=F32)

    n_sum, pw, pw_f, l_r, lakv = {}, {}, {}, {}, {}
    for c, q in insts:
        g = gram[c, q]
        l_ab = jnp.where(strict, g[0:c_len, 0:128], 0.0)
        n_sum[c, q] = l_ab
        pw_f[c, q] = l_ab
        pw[c, q] = l_ab.astype(BF16)
        l_r[c, q] = jnp.concatenate([jnp.where(incl, g[c_len:2 * c_len, 0:128], 0.0),
                                     jnp.where(incl, g[c_len:2 * c_len, 128:256], 0.0)], axis=1).astype(BF16)
        lakv[c, q] = _dot(jnp.where(strict, g[0:c_len, 128:256], 0.0).astype(BF16), ops[c, q][3])

    for c, q in insts:
        pw_f[c, q] = _dot(pw[c, q], pair_diag(pw_f[c, q]))
        pw[c, q] = pw_f[c, q].astype(BF16)
    for _ in range(int(math.log2(c_len)) - 2):
        prod = {}
        for c, q in insts:
            prod[c, q] = _dot(pw[c, q], jnp.concatenate([pair_diag(pw_f[c, q]), pair_diag(n_sum[c, q])], axis=1))
        for c, q in insts:
            n_sum[c, q] = n_sum[c, q] + pw_f[c, q] + prod[c, q][:, 128:256]
            pw_f[c, q] = prod[c, q][:, 0:128]
            pw[c, q] = pw_f[c, q].astype(BF16)
    for c, q in insts:
        n_sum[c, q] = n_sum[c, q] + pw_f[c, q] + _dot(pw[c, q], pair_diag(n_sum[c, q]))

    wu = {}
    for c, q in insts:
        a_s = ops[c, q][0]
        rhs = jnp.concatenate([pair_diag(a_s), pair_diag(lakv[c, q])], axis=1)
        wu[c, q] = jnp.concatenate([a_s, lakv[c, q]], axis=1) + _dot(n_sum[c, q].astype(BF16), rhs)

    for c in range(n_chunks):
        xs, st = {}, {}
        for q in range(npp):
            st[q] = st_ref[q]
            xs[q] = _dot(jnp.concatenate([wu[c, q][:, 0:128].astype(BF16), ops[c, q][1]], axis=0),
                         st[q].astype(BF16))
        for q in range(npp):
            _, _, v_b, v_m, btkt, pc_col = ops[c, q]
            u_s = xs[q][0:c_len, :] + wu[c, q][:, 128:256]
            y_ref[q, rows(c), :] = xs[q][c_len:2 * c_len, :] + _dot(
                l_r[c, q], jnp.concatenate([pair_diag(u_s), v_m], axis=0))
            upd = lax.dot_general(btkt, jnp.concatenate([u_s.astype(BF16), v_b], axis=0),
                                  (((0,), (0,)), ((), ())), preferred_element_type=F32)
            st_ref[q] = pc_col * st[q] + jnp.where(same_head, upd, 0.0)

    @pl.when(i == pl.num_programs(2) - 1)
    def _():
        n = B_HEAD_DIM
        for q in range(npp):
            full_t = st_ref[q].T
            sout_ref[2 * q] = full_t[0:n, 0:n]
            sout_ref[2 * q + 1] = full_t[n:128, n:128]


def _scan_prompt(r, lw, k, v, kn, bs, tt, npp):
    assert SCAN_CHUNK == B_HEAD_DIM, "the [token, (head, token')] lane packing needs chunk == head dim"
    bsz, n_pairs, t_len, _ = r.shape
    blk = lambda: pl.BlockSpec((None, npp, tt, 128), lambda b, p, i: (b, p, i, 0))
    return pl.pallas_call(
        functools.partial(_scan_kernel, tt, npp),
        grid=(bsz, n_pairs // npp, t_len // tt),
        in_specs=[blk() for _ in range(6)],
        out_specs=[blk(), pl.BlockSpec((None, 2 * npp, B_HEAD_DIM, B_HEAD_DIM), lambda b, p, i: (b, p, 0, 0))],
        out_shape=[jax.ShapeDtypeStruct((bsz, n_pairs, t_len, 128), F32),
                   jax.ShapeDtypeStruct((bsz, 2 * n_pairs, B_HEAD_DIM, B_HEAD_DIM), F32)],
        scratch_shapes=[pltpu.VMEM((npp, 128, 128), F32)],
        compiler_params=pltpu.CompilerParams(dimension_semantics=("parallel", "parallel", "arbitrary"),
                                             vmem_limit_bytes=VMEM_LIMIT),
        name="scan_prompt",
    )(r, lw, k, v, kn, bs)


def _step_kernel(r_ref, lw_ref, k_ref, v_ref, kn_ref, bs_ref, s_ref, y_ref, sout_ref):
    a = -kn_ref[...]
    w = jnp.exp(lw_ref[...])
    kx = k_ref[...]
    r = r_ref[...]
    b = bs_ref[...]
    ys = []
    for i in range(B_HEAD_DIM):
        s0 = s_ref[i]
        sa = jnp.sum(s0 * a, axis=0, keepdims=True)
        s1 = s0 * w + sa * b + v_ref[i:i + 1, :] * kx
        sout_ref[i] = s1
        ys.append(jnp.sum(s1 * r, axis=0, keepdims=True))
    y_ref[...] = jnp.concatenate(ys, axis=0)


def _scan_sample(r, lw, k, v, kn, bs, s0):
    n = r.shape[0]
    vecs = [jnp.transpose(z) for z in (r, lw, k, v, kn, bs)]
    s0_t = jnp.transpose(s0[0], (1, 2, 3, 0))
    vec = lambda: pl.BlockSpec((B_HEAD_DIM, n), lambda h: (h, 0))
    st = lambda: pl.BlockSpec((None, B_HEAD_DIM, B_HEAD_DIM, n), lambda h: (h, 0, 0, 0))
    y_t, s1_t = pl.pallas_call(
        _step_kernel,
        grid=(B_HEADS,),
        in_specs=[vec() for _ in range(6)] + [st()],
        out_specs=[vec(), st()],
        out_shape=[jax.ShapeDtypeStruct((D_MODEL, n), F32), jax.ShapeDtypeStruct(s0_t.shape, F32)],
        compiler_params=pltpu.CompilerParams(dimension_semantics=("parallel",)),
        name="scan_sample",
    )(*vecs, s0_t)
    return jnp.transpose(y_t), jnp.transpose(s1_t, (3, 0, 1, 2))[None]


def _back_kernel(paired, y_ref, bonus_ref, g_ref, gateb_ref, outa_ref, x_ref, lxw_ref, lxb_ref, pb_ref,
                 wo_ref, nfw_ref, rw_ref, rb_ref, ones2_ref, x1_ref, tok_ref, logt_ref):
    tm = x_ref.shape[0]
    n_parts = max(1, tm // BACK_ROWS)
    ones2 = ones2_ref[...]
    inv_n = 1.0 / B_HEAD_DIM
    parts = range(n_parts)
    rows = lambda i: slice(i * (tm // n_parts), (i + 1) * (tm // n_parts))

    if paired:
        y = [jnp.concatenate([y_ref[p, rows(i), :] for p in range(N_PAIRS)], axis=1) for i in parts]
    else:
        y = [y_ref[rows(i), :] for i in parts]
    mean = [_head_sums(y[i], ones2) * inv_n for i in parts]
    yc = [y[i] - mean[i] for i in parts]
    var = [_head_sums(yc[i] * yc[i], ones2) * inv_n for i in parts]
    yb = [((yc[i] * lax.rsqrt(var[i] + GN_EPS) * lxw_ref[...] + lxb_ref[...] + bonus_ref[rows(i), :])
           * g_ref[rows(i), :]).astype(BF16) for i in parts]
    mb = [_dot(yb[i], pb_ref[...]) for i in parts]
    merged = [(outa_ref[rows(i), :] + gateb_ref[rows(i), :] * mb[i]).astype(BF16) for i in parts]
    out = [_dot(merged[i], wo_ref[...]) for i in parts]
    tok = []
    for i in parts:
        x1 = x_ref[rows(i), :] + out[i]
        x1_ref[rows(i), :] = x1
        tok.append(_rms_norm(x1, nfw_ref[...]))
        tok_ref[rows(i), :] = tok[i].astype(BF16)
    for i in parts:
        t_hi = tok[i].astype(BF16)
        t_lo = (tok[i] - t_hi.astype(F32)).astype(BF16)
        logt_ref[:, rows(i)] = lax.dot_general(
            rw_ref[...], jnp.concatenate([t_hi, t_hi, t_lo], axis=1), (((1,), (1,)), ((), ())),
            preferred_element_type=F32) + rb_ref[...]


def _back(paired, y, bonus, g, gateb, outa, x, w, tm):
    n = x.shape[0]
    consts = [w["lxw"], w["lxb"], w["pb"], w["wo"], w["nfw"], w["rw3"], w["rbt"], w["ones2"]]
    tok = lambda: pl.BlockSpec((tm, D_MODEL), lambda i: (i, 0))
    if paired:
        nt = y.shape[2] // tm
        y_spec = pl.BlockSpec((None, N_PAIRS, tm, 128), lambda i: (i // nt, 0, i % nt, 0))
    else:
        y_spec = tok()
    return pl.pallas_call(
        functools.partial(_back_kernel, paired),
        grid=(n // tm,),
        in_specs=[y_spec] + [tok() for _ in range(5)] + [_const_spec(c.shape, 1) for c in consts],
        out_specs=[tok(), tok(), pl.BlockSpec((ROUTER_ROWS, tm), lambda i: (0, i))],
        out_shape=[jax.ShapeDtypeStruct((n, D_MODEL), F32), jax.ShapeDtypeStruct((n, D_MODEL), BF16),
                   jax.ShapeDtypeStruct((ROUTER_ROWS, n), F32)],
        compiler_params=pltpu.CompilerParams(dimension_semantics=("parallel",), vmem_limit_bytes=VMEM_LIMIT),
        name="back_prompt" if paired else "back_sample",
    )(y, bonus, g, gateb, outa, x, *consts)


def _moe_slots(ts):
    n = 2 * ts + N_EXPERTS * (SEG_ALIGN - 1)
    return -(-n // MOE_WINDOW) * MOE_WINDOW


def _split3_f32(x):
    hi = x.astype(BF16).astype(F32)
    mid = (x - hi).astype(BF16).astype(F32)
    lo = (x - hi - mid).astype(BF16).astype(F32)
    return hi, mid, lo


def _moe_kernel(ts, ns, tok_ref, logt_ref, x1_ref, utri_ref, wg_ref, wu_ref, wd_ref, fw_ref, o_ref,
                xy_ref, pg_ref, ws_ref, meta_ref):
    n_slots = 2 * xy_ref.shape[1]
    half_w = MOE_WINDOW // 2
    n_esteps = N_EXPERTS // MOE_EXPERTS_PER_STEP
    step = pl.program_id(1)
    tn = (((1,), (1,)), ((), ()))

    @pl.when(step < ns)
    def _route_and_sort():
        lt = logt_ref[...]
        neg = jnp.float32(-jnp.inf)
        big = jnp.float32(99.0)
        row8 = lax.broadcasted_iota(jnp.int32, (8, ts), 0).astype(F32)
        is_grp = row8 < N_GROUPS
        lg = jnp.where(is_grp, lt[0:8, :], neg)
        gmax = jnp.max(lg, axis=0, keepdims=True)
        grp = jnp.min(jnp.where(lg == gmax, row8, big), axis=0, keepdims=True)
        p_grp = 1.0 / jnp.sum(jnp.where(is_grp, jnp.exp(lg - gmax), 0.0), axis=0, keepdims=True)
        le = lt[EXPERT_ROW0:EXPERT_ROW0 + EXPERTS_PER_GROUP, :]
        for g in range(1, N_GROUPS):
            lo = EXPERT_ROW0 + g * EXPERTS_PER_GROUP
            le = jnp.where(grp == g, lt[lo:lo + EXPERTS_PER_GROUP, :], le)
        top1 = jnp.max(le, axis=0, keepdims=True)
        i1 = jnp.min(jnp.where(le == top1, row8, big), axis=0, keepdims=True)
        le2 = jnp.where(row8 == i1, neg, le)
        top2 = jnp.max(le2, axis=0, keepdims=True)
        i2 = jnp.min(jnp.where(le2 == top2, row8, big), axis=0, keepdims=True)
        e2 = jnp.exp(top2 - top1)
        w1 = p_grp / (1.0 + e2)
        w2 = p_grp * e2 / (1.0 + e2)

        row_e = lax.broadcasted_iota(jnp.int32, (N_EXPERTS, ts), 0).astype(F32)
        a1 = row_e == grp * EXPERTS_PER_GROUP + i1
        a2 = row_e == grp * EXPERTS_PER_GROUP + i2
        at = jnp.where(a1, 1.0, 0.0) + jnp.where(a2, 1.0, 0.0)
        rank = _dot(at.astype(BF16), utri_ref[...])
        cnt = rank[:, ts - 1:ts] + at[:, ts - 1:ts]
        cntp = jnp.floor((cnt + (SEG_ALIGN - 1)) * (1.0 / SEG_ALIGN)) * SEG_ALIGN
        cntp_b = jnp.broadcast_to(cntp, (N_EXPERTS, 128))
        r_e = lax.broadcasted_iota(jnp.int32, (N_EXPERTS, N_EXPERTS), 0)
        c_e = lax.broadcasted_iota(jnp.int32, (N_EXPERTS, N_EXPERTS), 1)
        off = _dot(jnp.where(r_e > c_e, 1.0, 0.0).astype(BF16), cntp_b.astype(BF16))
        slot = off[:, 0:1] + rank
        s1 = jnp.sum(jnp.where(a1, slot, 0.0), axis=0, keepdims=True)
        s2 = jnp.sum(jnp.where(a2, slot, 0.0), axis=0, keepdims=True)
        srow = lax.broadcasted_iota(jnp.int32, (n_slots, ts), 0).astype(F32)
        pg1 = jnp.where(srow == s1, 1.0, 0.0)
        pg2 = jnp.where(srow == s2, 1.0, 0.0)
        pg = (pg1 + pg2).astype(BF16)
        pg_ref[step] = pg
        xy_ref[step] = pltpu.bitcast(_dot(pg, tok_ref[...]).astype(BF16), jnp.int32)

        wrows = jnp.concatenate(
            [jnp.concatenate([p1, p2], axis=1) for p1, p2 in zip(_split3_f32(w1), _split3_f32(w2))]
            + [jnp.zeros((5, 2 * ts), F32)], axis=0).astype(BF16)
        wsl = lax.dot_general(jnp.concatenate([pg1.astype(BF16), pg2.astype(BF16)], axis=1), wrows, tn,
                              preferred_element_type=F32)
        ws_ref[step] = jnp.broadcast_to(wsl[:, 0:1] + wsl[:, 1:2] + wsl[:, 2:3], (n_slots, 128))
        meta_ref[step, 0:N_EXPERTS, :] = off.astype(jnp.int32)
        meta_ref[step, N_EXPERTS:2 * N_EXPERTS, :] = cntp_b.astype(jnp.int32)

    @pl.when((step >= ns) & (step < ns + n_esteps))
    def _experts():
        tile_row = (lax.broadcasted_iota(jnp.int32, (half_w, D_MODEL), 0) // 8) * SEG_ALIGN
        experts = range(MOE_EXPERTS_PER_STEP)
        offs, cnts = {}, {}
        for el in experts:
            e = (step - ns) * MOE_EXPERTS_PER_STEP + el
            offs[el] = [meta_ref[j, pl.ds(e, 1), :][0, 0] for j in range(ns)]
            cnts[el] = [meta_ref[j, pl.ds(N_EXPERTS + e, 1), :][0, 0] for j in range(ns)]
        n_win = functools.reduce(jnp.maximum, [(c + MOE_WINDOW - 1) // MOE_WINDOW
                                               for el in experts for c in cnts[el]])

        def window(w, carry):
            starts, lhs = {}, {}
            for el in experts:
                for j in range(ns):
                    st = jnp.minimum(offs[el][j] + w * MOE_WINDOW, n_slots - MOE_WINDOW)
                    starts[el, j] = pl.multiple_of(st, SEG_ALIGN)
                lhs[el] = jnp.concatenate(
                    [pltpu.bitcast(xy_ref[j, pl.ds(pl.multiple_of(starts[el, j] // 2, 8), half_w), :], BF16)
                     for j in range(ns)], axis=0)
            gate = {el: _dot(lhs[el], wg_ref[el]) for el in experts}
            up = {el: _dot(lhs[el], wu_ref[el]) for el in experts}
            y = {el: _dot((gate[el] * _sigmoid(gate[el]) * up[el]).astype(BF16), wd_ref[el]) for el in experts}
            for el in experts:
                for j in range(ns):
                    rows = starts[el, j] + tile_row
                    own = (rows >= offs[el][j] + w * MOE_WINDOW) & (rows < offs[el][j] + cnts[el][j])
                    yw = (y[el][j * MOE_WINDOW:(j + 1) * MOE_WINDOW, :]
                          * ws_ref[j, pl.ds(starts[el, j], MOE_WINDOW), 0:1])
                    pltpu.store(xy_ref.at[j, pl.ds(pl.multiple_of(starts[el, j] // 2, 8), half_w), :],
                                pltpu.bitcast(yw.astype(BF16), jnp.int32), mask=own)
            return carry

        lax.fori_loop(0, n_win, window, 0)

    @pl.when(step >= ns + n_esteps)
    def _combine():
        j = step - ns - n_esteps
        c = lax.dot_general(pg_ref[j], pltpu.bitcast(xy_ref[j], BF16), (((0,), (0,)), ((), ())),
                            preferred_element_type=F32)
        o_ref[...] = _rms_norm(x1_ref[...] + c, fw_ref[...])


def _moe(tok, logt, x1, w, ts, ns):
    n = tok.shape[0]
    n_slots = _moe_slots(ts)
    n_esteps = N_EXPERTS // MOE_EXPERTS_PER_STEP
    ne = MOE_EXPERTS_PER_STEP
    sub_in = lambda s, k: s * ns + jnp.clip(k, 0, ns - 1)
    sub_out = lambda s, k: s * ns + jnp.clip(k - ns - n_esteps, 0, ns - 1)
    expert = lambda k: jnp.clip(k - ns, 0, n_esteps - 1)
    utri = (jnp.arange(ts)[:, None] < jnp.arange(ts)[None, :]).astype(BF16)
    return pl.pallas_call(
        functools.partial(_moe_kernel, ts, ns),
        grid=(n // (ts * ns), 2 * ns + n_esteps),
        in_specs=[pl.BlockSpec((ts, D_MODEL), lambda s, k: (sub_in(s, k), 0)),
                  pl.BlockSpec((ROUTER_ROWS, ts), lambda s, k: (0, sub_in(s, k))),
                  pl.BlockSpec((ts, D_MODEL), lambda s, k: (sub_out(s, k), 0)),
                  pl.BlockSpec((ts, ts), lambda s, k: (0, 0)),
                  pl.BlockSpec((ne, D_MODEL, D_EXPERT), lambda s, k: (expert(k), 0, 0)),
                  pl.BlockSpec((ne, D_MODEL, D_EXPERT), lambda s, k: (expert(k), 0, 0)),
                  pl.BlockSpec((ne, D_EXPERT, D_MODEL), lambda s, k: (expert(k), 0, 0)),
                  pl.BlockSpec((1, D_MODEL), lambda s, k: (0, 0))],
        out_specs=pl.BlockSpec((ts, D_MODEL), lambda s, k: (sub_out(s, k), 0)),
        out_shape=jax.ShapeDtypeStruct((n, D_MODEL), F32),
        scratch_shapes=[pltpu.VMEM((ns, n_slots // 2, D_MODEL), jnp.int32), pltpu.VMEM((ns, n_slots, ts), BF16),
                        pltpu.VMEM((ns, n_slots, 128), F32), pltpu.VMEM((ns, 2 * N_EXPERTS, 128), jnp.int32)],
        compiler_params=pltpu.CompilerParams(dimension_semantics=("parallel", "arbitrary"),
                                             vmem_limit_bytes=VMEM_LIMIT),
        name="moe",
    )(tok, logt, x1, utri, w["wg"], w["wu"], w["wd"], w["fw"])


def _prep_weights(norm_mix_w, w_in, sgu_ln_w, sgu_ln_b, sgu_w_s, sgu_b, rwkv_mu, w_lora_up, w_bias,
                  a_lora_up, a_bias, g_lora_up, k_k, k_a, r_k, lnx_w, lnx_b, proj_a, proj_b, w_out,
                  norm_ffn_w, router_group, router_group_bias, router_expert, router_expert_bias,
                  moe_w_gate, moe_w_up, moe_w_down, norm_final_w):
    row = lambda z: z.reshape(1, -1).astype(F32)
    causal = jnp.tril(jnp.ones((CHUNK, CHUNK), dtype=bool))
    zeros_lora = jnp.zeros((64, D_MODEL), F32)
    head = jnp.arange(256) // B_HEAD_DIM
    ones_bd = (head[:, None] == head[None, :]).astype(BF16)
    pad = EXPERT_ROW0 - N_GROUPS
    rw = jnp.transpose(jnp.concatenate([router_group, jnp.zeros((D_MODEL, pad), F32), router_expert],
                                       axis=1))
    rw_hi = rw.astype(BF16)
    return {
        "nw": row(norm_mix_w), "win": w_in.astype(BF16), "lnw": row(sgu_ln_w), "lnb": row(sgu_ln_b),
        "wsc": jnp.where(causal[None], sgu_w_s, 0.0).astype(BF16),
        "bsb": jnp.repeat(jnp.transpose(sgu_b), A_HEAD_DIM, axis=1),
        "ws0": jnp.repeat(sgu_w_s[:, 0, 0], A_HEAD_DIM).reshape(1, -1),
        "bs0": jnp.repeat(sgu_b[:, 0], A_HEAD_DIM).reshape(1, -1),
        "mu": row(rwkv_mu),
        "wl": jnp.concatenate([w_lora_up, zeros_lora], axis=0).astype(BF16), "wb": row(w_bias),
        "al": jnp.concatenate([zeros_lora, a_lora_up], axis=0).astype(BF16), "ab": row(a_bias),
        "gl": g_lora_up.astype(BF16), "kk": row(k_k), "ka": row(k_a), "rk": row(r_k),
        "pa": proj_a.astype(BF16), "ones2": ones_bd,
        "lxw": row(lnx_w), "lxb": row(lnx_b), "pb": proj_b.astype(BF16), "wo": w_out.astype(BF16),
        "nfw": row(norm_ffn_w),
        "rw3": jnp.concatenate([rw_hi, (rw - rw_hi.astype(F32)).astype(BF16), rw_hi], axis=1),
        "rbt": jnp.concatenate([router_group_bias, jnp.zeros((pad,), F32), router_expert_bias]).reshape(-1, 1),
        "wg": moe_w_gate.astype(BF16), "wu": moe_w_up.astype(BF16), "wd": moe_w_down.astype(BF16),
        "fw": row(norm_final_w),
    }


def kernel(x_prompt, x_sample, state_wkv, state_shift, norm_mix_w, w_in, sgu_ln_w, sgu_ln_b, sgu_w_s, sgu_b, rwkv_mu, w_lora_up, w_bias, a_lora_up, a_bias, g_lora_up, k_k, k_a, r_k, lnx_w, lnx_b, proj_a, proj_b, w_out, norm_ffn_w, router_group, router_group_bias, router_expert, router_expert_bias, moe_w_gate, moe_w_up, moe_w_down, norm_final_w):
    layer = [z[0] for z in (norm_mix_w, w_in, sgu_ln_w, sgu_ln_b, sgu_w_s, sgu_b, rwkv_mu, w_lora_up, w_bias,
                            a_lora_up, a_bias, g_lora_up, k_k, k_a, r_k, lnx_w, lnx_b, proj_a, proj_b, w_out,
                            norm_ffn_w, router_group, router_group_bias, router_expert, router_expert_bias,
                            moe_w_gate, moe_w_up, moe_w_down)]
    w = _prep_weights(*layer, norm_final_w)
    bsz, t_len, _ = x_prompt.shape
    n_s = x_sample.shape[0]

    (outa, gateb, g, bonus, r, lw, k, v, kn, bs, last) = _front_prompt(x_prompt, w, 256)
    y, wkv_p = _scan_prompt(r, lw, k, v, kn, bs, 128, N_PAIRS)
    flat = lambda z: z.reshape(bsz * t_len, D_MODEL)
    x1, tok, logt = _back(True, y, flat(bonus), flat(g), flat(gateb), flat(outa), flat(x_prompt), w, 512)
    y_prompt = _moe(tok, logt, x1, w, 512, 4).reshape(bsz, t_len, D_MODEL)

    xs2 = x_sample.reshape(n_s, D_MODEL)
    (outa, gateb, g, bonus, r, lw, k, v, kn, bs, cols_s, vn_s) = _front_sample(xs2, state_shift[0], w)
    y, wkv_s = _scan_sample(r, lw, k, v, kn, bs, state_wkv)
    x1, tok, logt = _back(False, y, bonus, g, gateb, outa, xs2, w, n_s)
    y_sample = _moe(tok, logt, x1, w, n_s, 1).reshape(n_s, 1, D_MODEL)

    return (y_prompt, y_sample, wkv_p[None], last.reshape(1, bsz, SHIFT_WIDTH), wkv_s, cols_s[None],
            vn_s.reshape(1, n_s, 1, A_HEADS, A_HEAD_DIM))
```

```python
import functools
import math

import jax
import jax.numpy as jnp
from jax import lax
from jax.experimental import pallas as pl
from jax.experimental.pallas import tpu as pltpu

F32 = jnp.float32
BF16 = jnp.bfloat16
HIGHEST = lax.Precision.HIGHEST

D_MODEL = 1024
CHUNK = 128
A_HEADS = 8
A_HEAD_DIM = 128
B_HEADS = 16
B_HEAD_DIM = 64
N_PAIRS = B_HEADS // 2
SHIFT_WIDTH = 3328
OFF_V = 1024
OFF_SHIFT = 2048
OFF_GATE_A = OFF_SHIFT + SHIFT_WIDTH
OFF_GATE_B = OFF_GATE_A + D_MODEL
IN_COLS = OFF_GATE_B + D_MODEL
N_GROUPS = 4
EXPERTS_PER_GROUP = 8
N_EXPERTS = 32
D_EXPERT = 256
ROUTER_ROWS = 40
EXPERT_ROW0 = 8
SEG_ALIGN = 16
BACK_ROWS = 256
MOE_WINDOW = 48
MOE_EXPERTS_PER_STEP = 4
RMS_EPS = 1e-6
LN_EPS = 1e-5
GN_EPS = 64e-5

SCAN_CHUNK = 64
FRONT_BLOCK = 256
VMEM_LIMIT = 56 * 1024 * 1024


def _gelu(x):
    return x * (0.5 * (1.0 + jnp.tanh(math.sqrt(2.0 / math.pi) * (x + 0.044715 * (x * x * x)))))


def _sigmoid(x):
    return 1.0 / (1.0 + jnp.exp(-x))


def _softplus(z):
    return jnp.maximum(z, 0.0) + jnp.log(1.0 + jnp.exp(-jnp.abs(z)))


def _rms_norm(x, g):
    return x * lax.rsqrt(jnp.mean(x * x, axis=-1, keepdims=True) + RMS_EPS) * g


def _dot(a, b):
    return jnp.dot(a, b, preferred_element_type=F32)


def _dot_hi(a, b):
    return jnp.dot(a, b, preferred_element_type=F32, precision=HIGHEST)


def _head_sums(z, ones2):
    outs = [_dot(z[:, p * 256:(p + 1) * 256].astype(BF16), ones2) for p in range(z.shape[1] // 256)]
    return outs[0] if len(outs) == 1 else jnp.concatenate(outs, axis=1)


def _front_kernel(is_sample, tm, *refs):
    if is_sample:
        (x_ref, prev_ref, nw_ref, win_ref, lnw_ref, lnb_ref, ws0_ref, bs0_ref, mu_ref, wl_ref, wb_ref,
         al_ref, ab_ref, gl_ref, kk_ref, ka_ref, rk_ref, pa_ref, ones2_ref,
         outa_ref, gateb_ref, g_ref, bonus_ref, r_ref, lw_ref, k_ref, v_ref, kn_ref, bs_ref,
         cols_ref, vn_ref, ya_ref) = refs
    else:
        (x_ref, nw_ref, win_ref, lnw_ref, lnb_ref, wsc_ref, bsb_ref, mu_ref, wl_ref, wb_ref,
         al_ref, ab_ref, gl_ref, kk_ref, ka_ref, rk_ref, pa_ref, ones2_ref,
         outa_ref, gateb_ref, g_ref, bonus_ref, r_ref, lw_ref, k_ref, v_ref, kn_ref, bs_ref,
         last_ref, ya_ref, carry_ref) = refs

        @pl.when(pl.program_id(1) == 0)
        def _():
            carry_ref[...] = jnp.zeros_like(carry_ref)

    xb = _rms_norm(x_ref[...], nw_ref[...]).astype(BF16)
    ones2 = ones2_ref[...]

    def proj(lo, width):
        return _dot(xb, win_ref[:, lo:lo + width])

    def shifted(lo, width, cols=None):
        cs = slice(lo, lo + width)
        if cols is None:
            cols = proj(OFF_SHIFT + lo, width)
        if is_sample:
            prev = prev_ref[:, cs]
            cols_ref[:, cs] = cols
        else:
            row = lax.broadcasted_iota(jnp.int32, cols.shape, 0)
            prev = jnp.where(row == 0, carry_ref[0:1, cs], pltpu.roll(cols, 1, 0))
            carry_ref[0:1, cs] = cols[tm - 1:tm, :]
            last_ref[:, cs] = cols[tm - 1:tm, :]
        return cols + (prev - cols) * mu_ref[:, cs]

    tail = shifted(3 * D_MODEL, 256)
    wa = tail[:, 0:128]
    twa = jnp.tanh(wa).astype(BF16)
    wab = wa.astype(BF16)
    sgd = _sigmoid(tail[:, 128:256]).astype(BF16)

    def block_matmuls(j):
        lo = j * FRONT_BLOCK
        cs = slice(lo, lo + FRONT_BLOCK)
        return (proj(OFF_SHIFT + lo, FRONT_BLOCK), proj(OFF_SHIFT + D_MODEL + lo, FRONT_BLOCK),
                proj(OFF_SHIFT + 2 * D_MODEL + lo, FRONT_BLOCK), _dot(twa, wl_ref[:, cs]), _dot(wab, al_ref[:, cs]),
                _dot(sgd, gl_ref[:, cs]), proj(lo, FRONT_BLOCK), proj(OFF_V + lo, FRONT_BLOCK),
                proj(OFF_GATE_B + lo, FRONT_BLOCK))

    n_blocks = D_MODEL // FRONT_BLOCK
    raw = block_matmuls(0)
    gate_a_raw = None
    for j in range(n_blocks):
        lo = j * FRONT_BLOCK
        cs = slice(lo, lo + FRONT_BLOCK)
        r_raw, k_raw, v_raw, lw_dot, a_dot, g_dot, u_raw, va_raw, gb_raw = raw
        if j + 1 < n_blocks:
            raw = block_matmuls(j + 1)
        else:
            gate_a_raw = proj(OFF_GATE_A, D_MODEL)

        r = shifted(lo, FRONT_BLOCK, r_raw)
        k = shifted(D_MODEL + lo, FRONT_BLOCK, k_raw)
        vb = shifted(2 * D_MODEL + lo, FRONT_BLOCK, v_raw)
        w_log = -_softplus(-(wb_ref[:, cs] + lw_dot)) - 0.5
        logw = -jnp.exp(w_log)
        a = _sigmoid(ab_ref[:, cs] + a_dot)
        g_ref[:, cs] = g_dot
        kk = k * kk_ref[:, cs]
        kkn = kk / jnp.maximum(jnp.sqrt(_head_sums(kk * kk, ones2)), 1e-12)
        k2 = k * (1.0 + (a - 1.0) * ka_ref[:, cs])
        bonus_ref[:, cs] = _head_sums(r * k2 * rk_ref[:, cs], ones2) * vb
        bsc = kkn * a
        outs = ((r_ref, r), (lw_ref, logw), (k_ref, k2), (v_ref, vb), (kn_ref, kkn), (bs_ref, bsc))
        for o_ref, val in outs:
            if is_sample:
                o_ref[:, cs] = val
            else:
                for q in range(FRONT_BLOCK // 128):
                    o_ref[lo // 128 + q] = val[:, q * 128:(q + 1) * 128]

        u2 = _gelu(u_raw)
        v2 = _gelu(va_raw)
        vnb = []
        for q in range(FRONT_BLOCK // A_HEAD_DIM):
            h = lo // A_HEAD_DIM + q
            hs = slice(h * A_HEAD_DIM, (h + 1) * A_HEAD_DIM)
            u = u2[:, q * A_HEAD_DIM:(q + 1) * A_HEAD_DIM]
            v = v2[:, q * A_HEAD_DIM:(q + 1) * A_HEAD_DIM]
            mean = jnp.mean(v, axis=-1, keepdims=True)
            vc = v - mean
            var = jnp.mean(vc * vc, axis=-1, keepdims=True)
            vn = vc * lax.rsqrt(var + LN_EPS) * lnw_ref[:, hs] + lnb_ref[:, hs]
            if is_sample:
                vn_ref[:, hs] = vn
                ya_ref[:, hs] = (u * (vn * ws0_ref[:, hs] + bs0_ref[:, hs])).astype(BF16)
            else:
                vnb.append(vn.astype(BF16))
        if not is_sample:
            h0 = lo // A_HEAD_DIM
            w_cat = jnp.concatenate([wsc_ref[h0], wsc_ref[h0 + 1]], axis=1)
            zero = jnp.zeros((CHUNK, A_HEAD_DIM), BF16)
            for c in range(tm // CHUNK):
                rs = slice(c * CHUNK, (c + 1) * CHUNK)
                v_bd = jnp.concatenate([jnp.concatenate([vnb[0][rs, :], zero], axis=1),
                                        jnp.concatenate([zero, vnb[1][rs, :]], axis=1)], axis=0)
                ya_ref[rs, cs] = (u2[rs, :] * (_dot(w_cat, v_bd) + bsb_ref[:, cs])).astype(BF16)

        gateb_ref[:, cs] = _sigmoid(gb_raw)

    outa_ref[...] = _sigmoid(gate_a_raw) * _dot(ya_ref[...], pa_ref[...])


def _const_spec(shape, grid_rank):
    zeros = (0,) * len(shape)
    if grid_rank == 1:
        return pl.BlockSpec(shape, lambda i: zeros, pipeline_mode=pl.Buffered(1))
    return pl.BlockSpec(shape, lambda b, i: zeros, pipeline_mode=pl.Buffered(1))


def _front_prompt(x, w, tm):
    bsz, t_len, _ = x.shape
    nt = t_len // tm
    consts = [w["nw"], w["win"], w["lnw"], w["lnb"], w["wsc"], w["bsb"], w["mu"], w["wl"], w["wb"],
              w["al"], w["ab"], w["gl"], w["kk"], w["ka"], w["rk"], w["pa"], w["ones2"]]
    tok = lambda: pl.BlockSpec((None, tm, D_MODEL), lambda b, i: (b, i, 0))
    pair = lambda: pl.BlockSpec((None, N_PAIRS, tm, 128), lambda b, i: (b, 0, i, 0))
    tok_shape = jax.ShapeDtypeStruct((bsz, t_len, D_MODEL), F32)
    pair_shape = jax.ShapeDtypeStruct((bsz, N_PAIRS, t_len, 128), F32)
    return pl.pallas_call(
        functools.partial(_front_kernel, False, tm),
        grid=(bsz, nt),
        in_specs=[tok()] + [_const_spec(c.shape, 2) for c in consts],
        out_specs=[tok(), tok(), tok(), tok()] + [pair() for _ in range(6)]
        + [pl.BlockSpec((None, 1, SHIFT_WIDTH), lambda b, i: (b, 0, 0))],
        out_shape=[tok_shape] * 4 + [pair_shape] * 6 + [jax.ShapeDtypeStruct((bsz, 1, SHIFT_WIDTH), F32)],
        scratch_shapes=[pltpu.VMEM((tm, D_MODEL), BF16), pltpu.VMEM((8, SHIFT_WIDTH), F32)],
        compiler_params=pltpu.CompilerParams(dimension_semantics=("parallel", "arbitrary"),
                                             vmem_limit_bytes=VMEM_LIMIT),
        name="front_prompt",
    )(x, *consts)


def _front_sample(x, prev, w):
    n = x.shape[0]
    consts = [w["nw"], w["win"], w["lnw"], w["lnb"], w["ws0"], w["bs0"], w["mu"], w["wl"], w["wb"],
              w["al"], w["ab"], w["gl"], w["kk"], w["ka"], w["rk"], w["pa"], w["ones2"]]
    tok = lambda: pl.BlockSpec((n, D_MODEL), lambda i: (0, 0))
    wide = lambda: pl.BlockSpec((n, SHIFT_WIDTH), lambda i: (0, 0))
    tok_shape = jax.ShapeDtypeStruct((n, D_MODEL), F32)
    return pl.pallas_call(
        functools.partial(_front_kernel, True, n),
        grid=(1,),
        in_specs=[tok(), wide()] + [_const_spec(c.shape, 1) for c in consts],
        out_specs=[tok() for _ in range(10)] + [wide(), tok()],
        out_shape=[tok_shape] * 10 + [jax.ShapeDtypeStruct((n, SHIFT_WIDTH), F32), tok_shape],
        scratch_shapes=[pltpu.VMEM((n, D_MODEL), BF16)],
        compiler_params=pltpu.CompilerParams(dimension_semantics=("arbitrary",),
                                             vmem_limit_bytes=VMEM_LIMIT),
        name="front_sample",
    )(x, prev, *consts)


def _split3(x):
    hi = x.astype(BF16)
    r1 = x - hi.astype(F32)
    mid = r1.astype(BF16)
    lo = (r1 - mid.astype(F32)).astype(BF16)
    return hi, mid, lo


def _scan_kernel(tt, npp, r_ref, lw_ref, k_ref, v_ref, kn_ref, bs_ref, y_ref, sout_ref, st_ref):
    c_len = SCAN_CHUNK
    i = pl.program_id(2)

    @pl.when(i == 0)
    def _():
        st_ref[...] = jnp.zeros_like(st_ref)

    lane = lax.broadcasted_iota(jnp.int32, (c_len, 128), 1)
    head0 = lane < B_HEAD_DIM
    tpos = lax.broadcasted_iota(jnp.int32, (c_len, 128), 0)
    spos = jnp.bitwise_and(lane, c_len - 1)
    strict = tpos > spos
    incl = tpos >= spos
    row2 = lax.broadcasted_iota(jnp.int32, (128, 128), 0)
    col2 = lax.broadcasted_iota(jnp.int32, (128, 128), 1)
    eye = row2 == col2
    same_head = (row2 < B_HEAD_DIM) == (col2 < B_HEAD_DIM)
    rc = lax.broadcasted_iota(jnp.int32, (c_len, 3 * c_len), 0)
    cc = jnp.bitwise_and(lax.broadcasted_iota(jnp.int32, (c_len, 3 * c_len), 1), c_len - 1)
    tri3 = jnp.where(rc >= cc, 1.0, 0.0).astype(BF16)

    def pair_diag(z):
        return jnp.concatenate([jnp.where(head0, z, 0.0), jnp.where(head0, 0.0, z)], axis=0).astype(BF16)

    n_chunks = tt // c_len
    insts = [(c, q) for c in range(n_chunks) for q in range(npp)]
    rows = lambda c: slice(c * c_len, (c + 1) * c_len)

    log_p, ops, gram, n_sum, pw, pw_f, l_r, lakv, wu = ({} for _ in range(9))

    def stage_decay(group):
        for c, q in group:
            log_p[c, q] = _dot(tri3, jnp.concatenate(_split3(lw_ref[q, rows(c), :]), axis=0))

    def stage_gram(group):
        for c, q in group:
            lp = log_p[c, q]
            kn = kn_ref[q, rows(c), :]
            bs = bs_ref[q, rows(c), :]
            kx = k_ref[q, rows(c), :]
            log_pc = lp[c_len - 1:c_len, :]
            e_inv = jnp.exp(-lp)
            e_dec = jnp.exp(log_pc - lp)
            a_s = -kn * jnp.exp(lp - lw_ref[q, rows(c), :])
            r_s = (r_ref[q, rows(c), :] * jnp.exp(lp)).astype(BF16)
            v_s = v_ref[q, rows(c), :]
            btkt = jnp.concatenate([bs * e_dec, kx * e_dec], axis=0).astype(BF16)
            pc_col = jnp.sum(jnp.where(eye, jnp.exp(log_pc), 0.0), axis=1, keepdims=True)
            ops[c, q] = (a_s, r_s, v_s.astype(BF16), pair_diag(v_s), btkt, pc_col)
            gram[c, q] = lax.dot_general(
                jnp.concatenate([a_s.astype(BF16), r_s], axis=0),
                jnp.concatenate([pair_diag(bs * e_inv), pair_diag(kx * e_inv)], axis=0),
                (((1,), (1,)), ((), ())), preferred_element_type=F32)

    def stage_masks(group):
        for c, q in group:
            g = gram[c, q]
            l_ab = jnp.where(strict, g[0:c_len, 0:128], 0.0)
            n_sum[c, q] = l_ab
            pw_f[c, q] = l_ab
            pw[c, q] = l_ab.astype(BF16)
            l_r[c, q] = jnp.concatenate([jnp.where(incl, g[c_len:2 * c_len, 0:128], 0.0),
                                         jnp.where(incl, g[c_len:2 * c_len, 128:256], 0.0)], axis=1).astype(BF16)
            lakv[c, q] = _dot(jnp.where(strict, g[0:c_len, 128:256], 0.0).astype(BF16), ops[c, q][3])

    def stage_first_round(group):
        for c, q in group:
            pw_f[c, q] = _dot(pw[c, q], pair_diag(pw_f[c, q]))
            pw[c, q] = pw_f[c, q].astype(BF16)

    def stage_round(group):
        prod = {}
        for c, q in group:
            prod[c, q] = _dot(pw[c, q], jnp.concatenate([pair_diag(pw_f[c, q]), pair_diag(n_sum[c, q])], axis=1))
        for c, q in group:
            n_sum[c, q] = n_sum[c, q] + pw_f[c, q] + prod[c, q][:, 128:256]
            pw_f[c, q] = prod[c, q][:, 0:128]
            pw[c, q] = pw_f[c, q].astype(BF16)

    def stage_solve(group):
        for c, q in group:
            n_sum[c, q] = n_sum[c, q] + pw_f[c, q] + _dot(pw[c, q], pair_diag(n_sum[c, q]))
        for c, q in group:
            a_s = ops[c, q][0]
            rhs = jnp.concatenate([pair_diag(a_s), pair_diag(lakv[c, q])], axis=1)
            wu[c, q] = jnp.concatenate([a_s, lakv[c, q]], axis=1) + _dot(n_sum[c, q].astype(BF16), rhs)

    def stage_state(c):
        xs, st = {}, {}
        for q in range(npp):
            st[q] = st_ref[q]
            xs[q] = _dot(jnp.concatenate([wu[c, q][:, 0:128].astype(BF16), ops[c, q][1]], axis=0),
                         st[q].astype(BF16))
        for q in range(npp):
            _, _, v_b, v_m, btkt, pc_col = ops[c, q]
            u_s = xs[q][0:c_len, :] + wu[c, q][:, 128:256]
            y_ref[q, rows(c), :] = xs[q][c_len:2 * c_len, :] + _dot(
                l_r[c, q], jnp.concatenate([pair_diag(u_s), v_m], axis=0))
            upd = lax.dot_general(btkt, jnp.concatenate([u_s.astype(BF16), v_b], axis=0),
                                  (((0,), (0,)), ((), ())), preferred_element_type=F32)
            st_ref[q] = pc_col * st[q] + jnp.where(same_head, upd, 0.0)

    stage_decay(insts)
    stage_gram(insts)
    stage_masks(insts)
    stage_first_round(insts)
    for _ in range(int(math.log2(c_len)) - 2):
        stage_round(insts)
    stage_solve(insts)
    for c in range(n_chunks):
        stage_state(c)

    @pl.when(i == pl.num_programs(2) - 1)
    def _():
        n = B_HEAD_DIM
        for q in range(npp):
            full_t = st_ref[q].T
            sout_ref[2 * q] = full_t[0:n, 0:n]
            sout_ref[2 * q + 1] = full_t[n:128, n:128]


def _scan_prompt(r, lw, k, v, kn, bs, tt, npp):
    assert SCAN_CHUNK == B_HEAD_DIM, "the [token, (head, token')] lane packing needs chunk == head dim"
    bsz, n_pairs, t_len, _ = r.shape
    blk = lambda: pl.BlockSpec((None, npp, tt, 128), lambda b, p, i: (b, p, i, 0))
    return pl.pallas_call(
        functools.partial(_scan_kernel, tt, npp),
        grid=(bsz, n_pairs // npp, t_len // tt),
        in_specs=[blk() for _ in range(6)],
        out_specs=[blk(), pl.BlockSpec((None, 2 * npp, B_HEAD_DIM, B_HEAD_DIM), lambda b, p, i: (b, p, 0, 0))],
        out_shape=[jax.ShapeDtypeStruct((bsz, n_pairs, t_len, 128), F32),
                   jax.ShapeDtypeStruct((bsz, 2 * n_pairs, B_HEAD_DIM, B_HEAD_DIM), F32)],
        scratch_shapes=[pltpu.VMEM((npp, 128, 128), F32)],
        compiler_params=pltpu.CompilerParams(dimension_semantics=("parallel", "parallel", "arbitrary"),
                                             vmem_limit_bytes=VMEM_LIMIT),
        name="scan_prompt",
    )(r, lw, k, v, kn, bs)


def _step_kernel(r_ref, lw_ref, k_ref, v_ref, kn_ref, bs_ref, s_ref, y_ref, sout_ref):
    a = -kn_ref[...]
    w = jnp.exp(lw_ref[...])
    kx = k_ref[...]
    r = r_ref[...]
    b = bs_ref[...]
    ys = []
    for i in range(B_HEAD_DIM):
        s0 = s_ref[i]
        sa = jnp.sum(s0 * a, axis=0, keepdims=True)
        s1 = s0 * w + sa * b + v_ref[i:i + 1, :] * kx
        sout_ref[i] = s1
        ys.append(jnp.sum(s1 * r, axis=0, keepdims=True))
    y_ref[...] = jnp.concatenate(ys, axis=0)


def _scan_sample(r, lw, k, v, kn, bs, s0):
    n = r.shape[0]
    vecs = [jnp.transpose(z) for z in (r, lw, k, v, kn, bs)]
    s0_t = jnp.transpose(s0[0], (1, 2, 3, 0))
    vec = lambda: pl.BlockSpec((B_HEAD_DIM, n), lambda h: (h, 0))
    st = lambda: pl.BlockSpec((None, B_HEAD_DIM, B_HEAD_DIM, n), lambda h: (h, 0, 0, 0))
    y_t, s1_t = pl.pallas_call(
        _step_kernel,
        grid=(B_HEADS,),
        in_specs=[vec() for _ in range(6)] + [st()],
        out_specs=[vec(), st()],
        out_shape=[jax.ShapeDtypeStruct((D_MODEL, n), F32), jax.ShapeDtypeStruct(s0_t.shape, F32)],
        compiler_params=pltpu.CompilerParams(dimension_semantics=("parallel",)),
        name="scan_sample",
    )(*vecs, s0_t)
    return jnp.transpose(y_t), jnp.transpose(s1_t, (3, 0, 1, 2))[None]


def _back_kernel(paired, y_ref, bonus_ref, g_ref, gateb_ref, outa_ref, x_ref, lxw_ref, lxb_ref, pb_ref,
                 wo_ref, nfw_ref, rw_ref, rb_ref, ones2_ref, x1_ref, tok_ref, logt_ref):
    tm = x_ref.shape[0]
    n_parts = max(1, tm // BACK_ROWS)
    ones2 = ones2_ref[...]
    inv_n = 1.0 / B_HEAD_DIM
    parts = range(n_parts)
    rows = lambda i: slice(i * (tm // n_parts), (i + 1) * (tm // n_parts))

    if paired:
        y = [jnp.concatenate([y_ref[p, rows(i), :] for p in range(N_PAIRS)], axis=1) for i in parts]
    else:
        y = [y_ref[rows(i), :] for i in parts]
    mean = [_head_sums(y[i], ones2) * inv_n for i in parts]
    yc = [y[i] - mean[i] for i in parts]
    var = [_head_sums(yc[i] * yc[i], ones2) * inv_n for i in parts]
    yb = [((yc[i] * lax.rsqrt(var[i] + GN_EPS) * lxw_ref[...] + lxb_ref[...] + bonus_ref[rows(i), :])
           * g_ref[rows(i), :]).astype(BF16) for i in parts]
    mb = [_dot(yb[i], pb_ref[...]) for i in parts]
    merged = [(outa_ref[rows(i), :] + gateb_ref[rows(i), :] * mb[i]).astype(BF16) for i in parts]
    out = [_dot(merged[i], wo_ref[...]) for i in parts]
    tok = []
    for i in parts:
        x1 = x_ref[rows(i), :] + out[i]
        x1_ref[rows(i), :] = x1
        tok.append(_rms_norm(x1, nfw_ref[...]))
        tok_ref[rows(i), :] = tok[i].astype(BF16)
    for i in parts:
        t_hi = tok[i].astype(BF16)
        t_lo = (tok[i] - t_hi.astype(F32)).astype(BF16)
        logt_ref[:, rows(i)] = lax.dot_general(
            rw_ref[...], jnp.concatenate([t_hi, t_hi, t_lo], axis=1), (((1,), (1,)), ((), ())),
            preferred_element_type=F32) + rb_ref[...]


def _back(paired, y, bonus, g, gateb, outa, x, w, tm):
    n = x.shape[0]
    consts = [w["lxw"], w["lxb"], w["pb"], w["wo"], w["nfw"], w["rw3"], w["rbt"], w["ones2"]]
    tok = lambda: pl.BlockSpec((tm, D_MODEL), lambda i: (i, 0))
    if paired:
        nt = y.shape[2] // tm
        y_spec = pl.BlockSpec((None, N_PAIRS, tm, 128), lambda i: (i // nt, 0, i % nt, 0))
    else:
        y_spec = tok()
    return pl.pallas_call(
        functools.partial(_back_kernel, paired),
        grid=(n // tm,),
        in_specs=[y_spec] + [tok() for _ in range(5)] + [_const_spec(c.shape, 1) for c in consts],
        out_specs=[tok(), tok(), pl.BlockSpec((ROUTER_ROWS, tm), lambda i: (0, i))],
        out_shape=[jax.ShapeDtypeStruct((n, D_MODEL), F32), jax.ShapeDtypeStruct((n, D_MODEL), BF16),
                   jax.ShapeDtypeStruct((ROUTER_ROWS, n), F32)],
        compiler_params=pltpu.CompilerParams(dimension_semantics=("parallel",), vmem_limit_bytes=VMEM_LIMIT),
        name="back_prompt" if paired else "back_sample",
    )(y, bonus, g, gateb, outa, x, *consts)


def _moe_slots(ts):
    n = 2 * ts + N_EXPERTS * (SEG_ALIGN - 1)
    return -(-n // MOE_WINDOW) * MOE_WINDOW


def _split3_f32(x):
    hi = x.astype(BF16).astype(F32)
    mid = (x - hi).astype(BF16).astype(F32)
    lo = (x - hi - mid).astype(BF16).astype(F32)
    return hi, mid, lo


def _moe_kernel(ts, ns, tok_ref, logt_ref, x1_ref, utri_ref, wg_ref, wu_ref, wd_ref, fw_ref, o_ref,
                xy_ref, pg_ref, ws_ref, meta_ref):
    n_slots = 2 * xy_ref.shape[1]
    half_w = MOE_WINDOW // 2
    n_esteps = N_EXPERTS // MOE_EXPERTS_PER_STEP
    step = pl.program_id(1)
    tn = (((1,), (1,)), ((), ()))

    @pl.when(step < ns)
    def _route_and_sort():
        lt = logt_ref[...]
        neg = jnp.float32(-jnp.inf)
        big = jnp.float32(99.0)
        row8 = lax.broadcasted_iota(jnp.int32, (8, ts), 0).astype(F32)
        is_grp = row8 < N_GROUPS
        lg = jnp.where(is_grp, lt[0:8, :], neg)
        gmax = jnp.max(lg, axis=0, keepdims=True)
        grp = jnp.min(jnp.where(lg == gmax, row8, big), axis=0, keepdims=True)
        p_grp = 1.0 / jnp.sum(jnp.where(is_grp, jnp.exp(lg - gmax), 0.0), axis=0, keepdims=True)
        le = lt[EXPERT_ROW0:EXPERT_ROW0 + EXPERTS_PER_GROUP, :]
        for g in range(1, N_GROUPS):
            lo = EXPERT_ROW0 + g * EXPERTS_PER_GROUP
            le = jnp.where(grp == g, lt[lo:lo + EXPERTS_PER_GROUP, :], le)
        top1 = jnp.max(le, axis=0, keepdims=True)
        i1 = jnp.min(jnp.where(le == top1, row8, big), axis=0, keepdims=True)
        le2 = jnp.where(row8 == i1, neg, le)
        top2 = jnp.max(le2, axis=0, keepdims=True)
        i2 = jnp.min(jnp.where(le2 == top2, row8, big), axis=0, keepdims=True)
        e2 = jnp.exp(top2 - top1)
        w1 = p_grp / (1.0 + e2)
        w2 = p_grp * e2 / (1.0 + e2)

        row_e = lax.broadcasted_iota(jnp.int32, (N_EXPERTS, ts), 0).astype(F32)
        a1 = row_e == grp * EXPERTS_PER_GROUP + i1
        a2 = row_e == grp * EXPERTS_PER_GROUP + i2
        at = jnp.where(a1, 1.0, 0.0) + jnp.where(a2, 1.0, 0.0)
        rank = _dot(at.astype(BF16), utri_ref[...])
        cnt = rank[:, ts - 1:ts] + at[:, ts - 1:ts]
        cntp = jnp.floor((cnt + (SEG_ALIGN - 1)) * (1.0 / SEG_ALIGN)) * SEG_ALIGN
        cntp_b = jnp.broadcast_to(cntp, (N_EXPERTS, 128))
        r_e = lax.broadcasted_iota(jnp.int32, (N_EXPERTS, N_EXPERTS), 0)
        c_e = lax.broadcasted_iota(jnp.int32, (N_EXPERTS, N_EXPERTS), 1)
        off = _dot(jnp.where(r_e > c_e, 1.0, 0.0).astype(BF16), cntp_b.astype(BF16))
        slot = off[:, 0:1] + rank
        s1 = jnp.sum(jnp.where(a1, slot, 0.0), axis=0, keepdims=True)
        s2 = jnp.sum(jnp.where(a2, slot, 0.0), axis=0, keepdims=True)
        srow = lax.broadcasted_iota(jnp.int32, (n_slots, ts), 0).astype(F32)
        pg1 = jnp.where(srow == s1, 1.0, 0.0)
        pg2 = jnp.where(srow == s2, 1.0, 0.0)
        pg = (pg1 + pg2).astype(BF16)
        pg_ref[step] = pg
        xy_ref[step] = pltpu.bitcast(_dot(pg, tok_ref[...]).astype(BF16), jnp.int32)

        wrows = jnp.concatenate(
            [jnp.concatenate([p1, p2], axis=1) for p1, p2 in zip(_split3_f32(w1), _split3_f32(w2))]
            + [jnp.zeros((5, 2 * ts), F32)], axis=0).astype(BF16)
        wsl = lax.dot_general(jnp.concatenate([pg1.astype(BF16), pg2.astype(BF16)], axis=1), wrows, tn,
                              preferred_element_type=F32)
        ws_ref[step] = jnp.broadcast_to(wsl[:, 0:1] + wsl[:, 1:2] + wsl[:, 2:3], (n_slots, 128))
        meta_ref[step, 0:N_EXPERTS, :] = off.astype(jnp.int32)
        meta_ref[step, N_EXPERTS:2 * N_EXPERTS, :] = cntp_b.astype(jnp.int32)

    @pl.when((step >= ns) & (step < ns + n_esteps))
    def _experts():
        tile_row = (lax.broadcasted_iota(jnp.int32, (half_w, D_MODEL), 0) // 8) * SEG_ALIGN
        experts = range(MOE_EXPERTS_PER_STEP)
        offs, cnts = {}, {}
        for el in experts:
            e = (step - ns) * MOE_EXPERTS_PER_STEP + el
            offs[el] = [meta_ref[j, pl.ds(e, 1), :][0, 0] for j in range(ns)]
            cnts[el] = [meta_ref[j, pl.ds(N_EXPERTS + e, 1), :][0, 0] for j in range(ns)]
        n_win = functools.reduce(jnp.maximum, [(c + MOE_WINDOW - 1) // MOE_WINDOW
                                               for el in experts for c in cnts[el]])

        def window(w, carry):
            starts, lhs = {}, {}
            for el in experts:
                for j in range(ns):
                    st = jnp.minimum(offs[el][j] + w * MOE_WINDOW, n_slots - MOE_WINDOW)
                    starts[el, j] = pl.multiple_of(st, SEG_ALIGN)
                lhs[el] = jnp.concatenate(
                    [pltpu.bitcast(xy_ref[j, pl.ds(pl.multiple_of(starts[el, j] // 2, 8), half_w), :], BF16)
                     for j in range(ns)], axis=0)
            gate = {el: _dot(lhs[el], wg_ref[el]) for el in experts}
            up = {el: _dot(lhs[el], wu_ref[el]) for el in experts}
            y = {el: _dot((gate[el] * _sigmoid(gate[el]) * up[el]).astype(BF16), wd_ref[el]) for el in experts}
            for el in experts:
                for j in range(ns):
                    rows = starts[el, j] + tile_row
                    own = (rows >= offs[el][j] + w * MOE_WINDOW) & (rows < offs[el][j] + cnts[el][j])
                    yw = (y[el][j * MOE_WINDOW:(j + 1) * MOE_WINDOW, :]
                          * ws_ref[j, pl.ds(starts[el, j], MOE_WINDOW), 0:1])
                    pltpu.store(xy_ref.at[j, pl.ds(pl.multiple_of(starts[el, j] // 2, 8), half_w), :],
                                pltpu.bitcast(yw.astype(BF16), jnp.int32), mask=own)
            return carry

        lax.fori_loop(0, n_win, window, 0)

    @pl.when(step >= ns + n_esteps)
    def _combine():
        j = step - ns - n_esteps
        c = lax.dot_general(pg_ref[j], pltpu.bitcast(xy_ref[j], BF16), (((0,), (0,)), ((), ())),
                            preferred_element_type=F32)
        o_ref[...] = _rms_norm(x1_ref[...] + c, fw_ref[...])


def _moe(tok, logt, x1, w, ts, ns):
    n = tok.shape[0]
    n_slots = _moe_slots(ts)
    n_esteps = N_EXPERTS // MOE_EXPERTS_PER_STEP
    ne = MOE_EXPERTS_PER_STEP
    sub_in = lambda s, k: s * ns + jnp.clip(k, 0, ns - 1)
    sub_out = lambda s, k: s * ns + jnp.clip(k - ns - n_esteps, 0, ns - 1)
    expert = lambda k: jnp.clip(k - ns, 0, n_esteps - 1)
    utri = (jnp.arange(ts)[:, None] < jnp.arange(ts)[None, :]).astype(BF16)
    return pl.pallas_call(
        functools.partial(_moe_kernel, ts, ns),
        grid=(n // (ts * ns), 2 * ns + n_esteps),
        in_specs=[pl.BlockSpec((ts, D_MODEL), lambda s, k: (sub_in(s, k), 0)),
                  pl.BlockSpec((ROUTER_ROWS, ts), lambda s, k: (0, sub_in(s, k))),
                  pl.BlockSpec((ts, D_MODEL), lambda s, k: (sub_out(s, k), 0)),
                  pl.BlockSpec((ts, ts), lambda s, k: (0, 0)),
                  pl.BlockSpec((ne, D_MODEL, D_EXPERT), lambda s, k: (expert(k), 0, 0)),
                  pl.BlockSpec((ne, D_MODEL, D_EXPERT), lambda s, k: (expert(k), 0, 0)),
                  pl.BlockSpec((ne, D_EXPERT, D_MODEL), lambda s, k: (expert(k), 0, 0)),
                  pl.BlockSpec((1, D_MODEL), lambda s, k: (0, 0))],
        out_specs=pl.BlockSpec((ts, D_MODEL), lambda s, k: (sub_out(s, k), 0)),
        out_shape=jax.ShapeDtypeStruct((n, D_MODEL), F32),
        scratch_shapes=[pltpu.VMEM((ns, n_slots // 2, D_MODEL), jnp.int32), pltpu.VMEM((ns, n_slots, ts), BF16),
                        pltpu.VMEM((ns, n_slots, 128), F32), pltpu.VMEM((ns, 2 * N_EXPERTS, 128), jnp.int32)],
        compiler_params=pltpu.CompilerParams(dimension_semantics=("parallel", "arbitrary"),
                                             vmem_limit_bytes=VMEM_LIMIT),
        name="moe",
    )(tok, logt, x1, utri, w["wg"], w["wu"], w["wd"], w["fw"])


def _cast_kernel(x_ref, o_ref):
    o_ref[...] = x_ref[...].astype(o_ref.dtype)


def _to_bf16(x, rows):
    n, m = x.shape
    return pl.pallas_call(
        _cast_kernel,
        grid=(n // rows,),
        in_specs=[pl.BlockSpec((rows, m), lambda i: (i, 0))],
        out_specs=pl.BlockSpec((rows, m), lambda i: (i, 0)),
        out_shape=jax.ShapeDtypeStruct((n, m), BF16),
        compiler_params=pltpu.CompilerParams(dimension_semantics=("parallel",)),
        name="weight_cast",
    )(x)


def _prep_weights(norm_mix_w, w_in, sgu_ln_w, sgu_ln_b, sgu_w_s, sgu_b, rwkv_mu, w_lora_up, w_bias,
                  a_lora_up, a_bias, g_lora_up, k_k, k_a, r_k, lnx_w, lnx_b, proj_a, proj_b, w_out,
                  norm_ffn_w, router_group, router_group_bias, router_expert, router_expert_bias,
                  moe_w_gate, moe_w_up, moe_w_down, norm_final_w):
    row = lambda z: z.reshape(1, -1).astype(F32)
    causal = jnp.tril(jnp.ones((CHUNK, CHUNK), dtype=bool))
    zeros_lora = jnp.zeros((64, D_MODEL), F32)
    head = jnp.arange(256) // B_HEAD_DIM
    ones_bd = (head[:, None] == head[None, :]).astype(BF16)
    pad = EXPERT_ROW0 - N_GROUPS
    rw = jnp.transpose(jnp.concatenate([router_group, jnp.zeros((D_MODEL, pad), F32), router_expert],
                                       axis=1))
    rw_hi = rw.astype(BF16)
    return {
        "nw": row(norm_mix_w), "win": _to_bf16(w_in, 128), "lnw": row(sgu_ln_w), "lnb": row(sgu_ln_b),
        "wsc": jnp.where(causal[None], sgu_w_s, 0.0).astype(BF16),
        "bsb": jnp.repeat(jnp.transpose(sgu_b), A_HEAD_DIM, axis=1),
        "ws0": jnp.repeat(sgu_w_s[:, 0, 0], A_HEAD_DIM).reshape(1, -1),
        "bs0": jnp.repeat(sgu_b[:, 0], A_HEAD_DIM).reshape(1, -1),
        "mu": row(rwkv_mu),
        "wl": jnp.concatenate([w_lora_up, zeros_lora], axis=0).astype(BF16), "wb": row(w_bias),
        "al": jnp.concatenate([zeros_lora, a_lora_up], axis=0).astype(BF16), "ab": row(a_bias),
        "gl": g_lora_up.astype(BF16), "kk": row(k_k), "ka": row(k_a), "rk": row(r_k),
        "pa": proj_a.astype(BF16), "ones2": ones_bd,
        "lxw": row(lnx_w), "lxb": row(lnx_b), "pb": proj_b.astype(BF16), "wo": w_out.astype(BF16),
        "nfw": row(norm_ffn_w),
        "rw3": jnp.concatenate([rw_hi, (rw - rw_hi.astype(F32)).astype(BF16), rw_hi], axis=1),
        "rbt": jnp.concatenate([router_group_bias, jnp.zeros((pad,), F32), router_expert_bias]).reshape(-1, 1),
        "wg": moe_w_gate.astype(BF16), "wu": moe_w_up.astype(BF16), "wd": moe_w_down.astype(BF16),
        "fw": row(norm_final_w),
    }


def kernel(x_prompt, x_sample, state_wkv, state_shift, norm_mix_w, w_in, sgu_ln_w, sgu_ln_b, sgu_w_s, sgu_b, rwkv_mu, w_lora_up, w_bias, a_lora_up, a_bias, g_lora_up, k_k, k_a, r_k, lnx_w, lnx_b, proj_a, proj_b, w_out, norm_ffn_w, router_group, router_group_bias, router_expert, router_expert_bias, moe_w_gate, moe_w_up, moe_w_down, norm_final_w):
    layer = [z[0] for z in (norm_mix_w, w_in, sgu_ln_w, sgu_ln_b, sgu_w_s, sgu_b, rwkv_mu, w_lora_up, w_bias,
                            a_lora_up, a_bias, g_lora_up, k_k, k_a, r_k, lnx_w, lnx_b, proj_a, proj_b, w_out,
                            norm_ffn_w, router_group, router_group_bias, router_expert, router_expert_bias,
                            moe_w_gate, moe_w_up, moe_w_down)]
    w = _prep_weights(*layer, norm_final_w)
    bsz, t_len, _ = x_prompt.shape
    n_s = x_sample.shape[0]

    (outa, gateb, g, bonus, r, lw, k, v, kn, bs, last) = _front_prompt(x_prompt, w, 256)
    y, wkv_p = _scan_prompt(r, lw, k, v, kn, bs, 256, N_PAIRS)
    flat = lambda z: z.reshape(bsz * t_len, D_MODEL)
    x1, tok, logt = _back(True, y, flat(bonus), flat(g), flat(gateb), flat(outa), flat(x_prompt), w, 512)
    y_prompt = _moe(tok, logt, x1, w, 512, 4).reshape(bsz, t_len, D_MODEL)

    xs2 = x_sample.reshape(n_s, D_MODEL)
    (outa, gateb, g, bonus, r, lw, k, v, kn, bs, cols_s, vn_s) = _front_sample(xs2, state_shift[0], w)
    y, wkv_s = _scan_sample(r, lw, k, v, kn, bs, state_wkv)
    x1, tok, logt = _back(False, y, bonus, g, gateb, outa, xs2, w, n_s)
    y_sample = _moe(tok, logt, x1, w, n_s, 1).reshape(n_s, 1, D_MODEL)

    return (y_prompt, y_sample, wkv_p[None], last.reshape(1, bsz, SHIFT_WIDTH), wkv_s, cols_s[None],
            vn_s.reshape(1, n_s, 1, A_HEADS, A_HEAD_DIM))
```

```python
import functools
import math

import jax
import jax.numpy as jnp
from jax import lax
from jax.experimental import pallas as pl
from jax.experimental.pallas import tpu as pltpu

F32 = jnp.float32
BF16 = jnp.bfloat16

D_MODEL = 1024
CHUNK = 128
A_HEADS = 8
A_HEAD_DIM = 128
B_HEADS = 16
B_HEAD_DIM = 64
N_PAIRS = B_HEADS // 2
SHIFT_WIDTH = 3328
OFF_V = 1024
OFF_SHIFT = 2048
OFF_GATE_A = OFF_SHIFT + SHIFT_WIDTH
OFF_GATE_B = OFF_GATE_A + D_MODEL
N_GROUPS = 4
EXPERTS_PER_GROUP = 8
N_EXPERTS = 32
D_EXPERT = 256
ROUTER_ROWS = 40
EXPERT_ROW0 = 8
SEG_ALIGN = 16
BACK_ROWS = 256
MOE_WINDOW = 48
MOE_EXPERTS_PER_STEP = 4
RMS_EPS = 1e-6
LN_EPS = 1e-5
GN_EPS = 64e-5

SCAN_CHUNK = 64
FRONT_BLOCK = 256
VMEM_LIMIT = 56 * 1024 * 1024

FRONT_ROWS = 256
SCAN_ROWS = 512
BACK_TILE = 512
MOE_SUBTILE = 512
MOE_SUBTILES = 4


def _gelu(x):
    return x * (0.5 * (1.0 + jnp.tanh(math.sqrt(2.0 / math.pi) * (x + 0.044715 * (x * x * x)))))


def _sigmoid(x):
    return 1.0 / (1.0 + jnp.exp(-x))


def _softplus(z):
    return jnp.maximum(z, 0.0) + jnp.log(1.0 + jnp.exp(-jnp.abs(z)))


def _rms_norm(x, g):
    return x * lax.rsqrt(jnp.mean(x * x, axis=-1, keepdims=True) + RMS_EPS) * g


def _dot(a, b):
    return jnp.dot(a, b, preferred_element_type=F32)


def _head_sums(z, ones2):
    outs = [_dot(z[:, p * 256:(p + 1) * 256].astype(BF16), ones2) for p in range(z.shape[1] // 256)]
    return outs[0] if len(outs) == 1 else jnp.concatenate(outs, axis=1)


def _front_kernel(is_sample, tm, *refs):
    if is_sample:
        (x_ref, prev_ref, nw_ref, win_ref, lnw_ref, lnb_ref, ws0_ref, bs0_ref, mu_ref, wl_ref, wb_ref,
         al_ref, ab_ref, gl_ref, kk_ref, ka_ref, rk_ref, pa_ref, ones2_ref,
         outa_ref, gateb_ref, g_ref, bonus_ref, r_ref, lw_ref, k_ref, v_ref, kn_ref, bs_ref,
         cols_ref, vn_ref, ya_ref) = refs
    else:
        (x_ref, nw_ref, win_ref, lnw_ref, lnb_ref, wsc_ref, bsb_ref, mu_ref, wl_ref, wb_ref,
         al_ref, ab_ref, gl_ref, kk_ref, ka_ref, rk_ref, pa_ref, ones2_ref,
         outa_ref, gateb_ref, g_ref, bonus_ref, r_ref, lw_ref, k_ref, v_ref, kn_ref, bs_ref,
         last_ref, ya_ref, carry_ref) = refs

        @pl.when(pl.program_id(1) == 0)
        def _():
            carry_ref[...] = jnp.zeros_like(carry_ref)

    xb = _rms_norm(x_ref[...], nw_ref[...]).astype(BF16)
    ones2 = ones2_ref[...]

    def proj(lo, width):
        return _dot(xb, win_ref[:, lo:lo + width])

    def shifted(lo, width, cols=None):
        cs = slice(lo, lo + width)
        if cols is None:
            cols = proj(OFF_SHIFT + lo, width)
        if is_sample:
            prev = prev_ref[:, cs]
            cols_ref[:, cs] = cols
        else:
            row = lax.broadcasted_iota(jnp.int32, cols.shape, 0)
            prev = jnp.where(row == 0, carry_ref[0:1, cs], pltpu.roll(cols, 1, 0))
            carry_ref[0:1, cs] = cols[tm - 1:tm, :]
            last_ref[:, cs] = cols[tm - 1:tm, :]
        return cols + (prev - cols) * mu_ref[:, cs]

    tail = shifted(3 * D_MODEL, 256)
    wa = tail[:, 0:128]
    twa = jnp.tanh(wa).astype(BF16)
    wab = wa.astype(BF16)
    sgd = _sigmoid(tail[:, 128:256]).astype(BF16)

    def block_matmuls(j):
        lo = j * FRONT_BLOCK
        cs = slice(lo, lo + FRONT_BLOCK)
        return (proj(OFF_SHIFT + lo, FRONT_BLOCK), proj(OFF_SHIFT + D_MODEL + lo, FRONT_BLOCK),
                proj(OFF_SHIFT + 2 * D_MODEL + lo, FRONT_BLOCK), _dot(twa, wl_ref[:, cs]), _dot(wab, al_ref[:, cs]),
                _dot(sgd, gl_ref[:, cs]), proj(lo, FRONT_BLOCK), proj(OFF_V + lo, FRONT_BLOCK),
                proj(OFF_GATE_B + lo, FRONT_BLOCK))

    n_blocks = D_MODEL // FRONT_BLOCK
    raw = block_matmuls(0)
    gate_a_raw = None
    for j in range(n_blocks):
        lo = j * FRONT_BLOCK
        cs = slice(lo, lo + FRONT_BLOCK)
        r_raw, k_raw, v_raw, lw_dot, a_dot, g_dot, u_raw, va_raw, gb_raw = raw
        if j + 1 < n_blocks:
            raw = block_matmuls(j + 1)
        else:
            gate_a_raw = proj(OFF_GATE_A, D_MODEL)

        r = shifted(lo, FRONT_BLOCK, r_raw)
        k = shifted(D_MODEL + lo, FRONT_BLOCK, k_raw)
        vb = shifted(2 * D_MODEL + lo, FRONT_BLOCK, v_raw)
        w_log = -_softplus(-(wb_ref[:, cs] + lw_dot)) - 0.5
        logw = -jnp.exp(w_log)
        a = _sigmoid(ab_ref[:, cs] + a_dot)
        g_ref[:, cs] = g_dot
        kk = k * kk_ref[:, cs]
        kkn = kk / jnp.maximum(jnp.sqrt(_head_sums(kk * kk, ones2)), 1e-12)
        k2 = k * (1.0 + (a - 1.0) * ka_ref[:, cs])
        bonus_ref[:, cs] = _head_sums(r * k2 * rk_ref[:, cs], ones2) * vb
        bsc = kkn * a
        outs = ((r_ref, r), (lw_ref, logw), (k_ref, k2), (v_ref, vb), (kn_ref, kkn), (bs_ref, bsc))
        for o_ref, val in outs:
            if is_sample:
                o_ref[:, cs] = val
            else:
                for q in range(FRONT_BLOCK // 128):
                    o_ref[lo // 128 + q] = val[:, q * 128:(q + 1) * 128]

        u2 = _gelu(u_raw)
        v2 = _gelu(va_raw)
        vnb = []
        for q in range(FRONT_BLOCK // A_HEAD_DIM):
            h = lo // A_HEAD_DIM + q
            hs = slice(h * A_HEAD_DIM, (h + 1) * A_HEAD_DIM)
            u = u2[:, q * A_HEAD_DIM:(q + 1) * A_HEAD_DIM]
            v = v2[:, q * A_HEAD_DIM:(q + 1) * A_HEAD_DIM]
            mean = jnp.mean(v, axis=-1, keepdims=True)
            vc = v - mean
            var = jnp.mean(vc * vc, axis=-1, keepdims=True)
            vn = vc * lax.rsqrt(var + LN_EPS) * lnw_ref[:, hs] + lnb_ref[:, hs]
            if is_sample:
                vn_ref[:, hs] = vn
                ya_ref[:, hs] = (u * (vn * ws0_ref[:, hs] + bs0_ref[:, hs])).astype(BF16)
            else:
                vnb.append(vn.astype(BF16))
        if not is_sample:
            h0 = lo // A_HEAD_DIM
            w_cat = jnp.concatenate([wsc_ref[h0], wsc_ref[h0 + 1]], axis=1)
            zero = jnp.zeros((CHUNK, A_HEAD_DIM), BF16)
            for c in range(tm // CHUNK):
                rs = slice(c * CHUNK, (c + 1) * CHUNK)
                v_bd = jnp.concatenate([jnp.concatenate([vnb[0][rs, :], zero], axis=1),
                                        jnp.concatenate([zero, vnb[1][rs, :]], axis=1)], axis=0)
                ya_ref[rs, cs] = (u2[rs, :] * (_dot(w_cat, v_bd) + bsb_ref[:, cs])).astype(BF16)

        gateb_ref[:, cs] = _sigmoid(gb_raw)

    outa_ref[...] = _sigmoid(gate_a_raw) * _dot(ya_ref[...], pa_ref[...])


def _const_spec(shape, grid_rank):
    zeros = (0,) * len(shape)
    if grid_rank == 1:
        return pl.BlockSpec(shape, lambda i: zeros, pipeline_mode=pl.Buffered(1))
    return pl.BlockSpec(shape, lambda b, i: zeros, pipeline_mode=pl.Buffered(1))


def _front_prompt(x, w, tm):
    bsz, t_len, _ = x.shape
    nt = t_len // tm
    consts = [w["nw"], w["win"], w["lnw"], w["lnb"], w["wsc"], w["bsb"], w["mu"], w["wl"], w["wb"],
              w["al"], w["ab"], w["gl"], w["kk"], w["ka"], w["rk"], w["pa"], w["ones2"]]
    tok = lambda: pl.BlockSpec((None, tm, D_MODEL), lambda b, i: (b, i, 0))
    pair = lambda: pl.BlockSpec((None, N_PAIRS, tm, 128), lambda b, i: (b, 0, i, 0))
    tok_shape = jax.ShapeDtypeStruct((bsz, t_len, D_MODEL), F32)
    pair_shape = jax.ShapeDtypeStruct((bsz, N_PAIRS, t_len, 128), F32)
    return pl.pallas_call(
        functools.partial(_front_kernel, False, tm),
        grid=(bsz, nt),
        in_specs=[tok()] + [_const_spec(c.shape, 2) for c in consts],
        out_specs=[tok(), tok(), tok(), tok()] + [pair() for _ in range(6)]
        + [pl.BlockSpec((None, 1, SHIFT_WIDTH), lambda b, i: (b, 0, 0))],
        out_shape=[tok_shape] * 4 + [pair_shape] * 6 + [jax.ShapeDtypeStruct((bsz, 1, SHIFT_WIDTH), F32)],
        scratch_shapes=[pltpu.VMEM((tm, D_MODEL), BF16), pltpu.VMEM((8, SHIFT_WIDTH), F32)],
        compiler_params=pltpu.CompilerParams(dimension_semantics=("parallel", "arbitrary"),
                                             vmem_limit_bytes=VMEM_LIMIT),
        name="front_prompt",
    )(x, *consts)


def _front_sample(x, prev, w):
    n = x.shape[0]
    consts = [w["nw"], w["win"], w["lnw"], w["lnb"], w["ws0"], w["bs0"], w["mu"], w["wl"], w["wb"],
              w["al"], w["ab"], w["gl"], w["kk"], w["ka"], w["rk"], w["pa"], w["ones2"]]
    tok = lambda: pl.BlockSpec((n, D_MODEL), lambda i: (0, 0))
    wide = lambda: pl.BlockSpec((n, SHIFT_WIDTH), lambda i: (0, 0))
    tok_shape = jax.ShapeDtypeStruct((n, D_MODEL), F32)
    return pl.pallas_call(
        functools.partial(_front_kernel, True, n),
        grid=(1,),
        in_specs=[tok(), wide()] + [_const_spec(c.shape, 1) for c in consts],
        out_specs=[tok() for _ in range(10)] + [wide(), tok()],
        out_shape=[tok_shape] * 10 + [jax.ShapeDtypeStruct((n, SHIFT_WIDTH), F32), tok_shape],
        scratch_shapes=[pltpu.VMEM((n, D_MODEL), BF16)],
        compiler_params=pltpu.CompilerParams(dimension_semantics=("arbitrary",),
                                             vmem_limit_bytes=VMEM_LIMIT),
        name="front_sample",
    )(x, prev, *consts)


def _split3(x):
    hi = x.astype(BF16)
    r1 = x - hi.astype(F32)
    mid = r1.astype(BF16)
    lo = (r1 - mid.astype(F32)).astype(BF16)
    return hi, mid, lo


def _scan_kernel(tt, npp, r_ref, lw_ref, k_ref, v_ref, kn_ref, bs_ref, y_ref, sout_ref, st_ref):
    c_len = SCAN_CHUNK
    i = pl.program_id(2)

    @pl.when(i == 0)
    def _():
        st_ref[...] = jnp.zeros_like(st_ref)

    lane = lax.broadcasted_iota(jnp.int32, (c_len, 128), 1)
    head0 = lane < B_HEAD_DIM
    tpos = lax.broadcasted_iota(jnp.int32, (c_len, 128), 0)
    spos = jnp.bitwise_and(lane, c_len - 1)
    strict = tpos > spos
    incl = tpos >= spos
    row2 = lax.broadcasted_iota(jnp.int32, (128, 128), 0)
    col2 = lax.broadcasted_iota(jnp.int32, (128, 128), 1)
    eye = row2 == col2
    same_head = (row2 < B_HEAD_DIM) == (col2 < B_HEAD_DIM)
    rc = lax.broadcasted_iota(jnp.int32, (c_len, 3 * c_len), 0)
    cc = jnp.bitwise_and(lax.broadcasted_iota(jnp.int32, (c_len, 3 * c_len), 1), c_len - 1)
    tri3 = jnp.where(rc >= cc, 1.0, 0.0).astype(BF16)

    def pair_diag(z):
        return jnp.concatenate([jnp.where(head0, z, 0.0), jnp.where(head0, 0.0, z)], axis=0).astype(BF16)

    n_chunks = tt // c_len
    insts = [(c, q) for c in range(n_chunks) for q in range(npp)]
    rows = lambda c: slice(c * c_len, (c + 1) * c_len)

    log_p, ops, gram, n_sum, pw, pw_f, l_r, lakv, wu = ({} for _ in range(9))

    def stage_decay(group):
        for c, q in group:
            log_p[c, q] = _dot(tri3, jnp.concatenate(_split3(lw_ref[q, rows(c), :]), axis=0))

    def stage_gram(group):
        for c, q in group:
            lp = log_p[c, q]
            kn = kn_ref[q, rows(c), :]
            bs = bs_ref[q, rows(c), :]
            kx = k_ref[q, rows(c), :]
            log_pc = lp[c_len - 1:c_len, :]
            e_inv = jnp.exp(-lp)
            e_dec = jnp.exp(log_pc - lp)
            a_s = -kn * jnp.exp(lp - lw_ref[q, rows(c), :])
            r_s = (r_ref[q, rows(c), :] * jnp.exp(lp)).astype(BF16)
            v_s = v_ref[q, rows(c), :]
            btkt = jnp.concatenate([bs * e_dec, kx * e_dec], axis=0).astype(BF16)
            pc_col = jnp.sum(jnp.where(eye, jnp.exp(log_pc), 0.0), axis=1, keepdims=True)
            ops[c, q] = (a_s, r_s, v_s.astype(BF16), pair_diag(v_s), btkt, pc_col)
            gram[c, q] = lax.dot_general(
                jnp.concatenate([a_s.astype(BF16), r_s], axis=0),
                jnp.concatenate([pair_diag(bs * e_inv), pair_diag(kx * e_inv)], axis=0),
                (((1,), (1,)), ((), ())), preferred_element_type=F32)

    def stage_masks(group):
        for c, q in group:
            g = gram[c, q]
            l_ab = jnp.where(strict, g[0:c_len, 0:128], 0.0)
            n_sum[c, q] = l_ab
            pw_f[c, q] = l_ab
            pw[c, q] = l_ab.astype(BF16)
            l_r[c, q] = jnp.concatenate([jnp.where(incl, g[c_len:2 * c_len, 0:128], 0.0),
                                         jnp.where(incl, g[c_len:2 * c_len, 128:256], 0.0)], axis=1).astype(BF16)
            lakv[c, q] = _dot(jnp.where(strict, g[0:c_len, 128:256], 0.0).astype(BF16), ops[c, q][3])

    def stage_first_round(group):
        for c, q in group:
            pw_f[c, q] = _dot(pw[c, q], pair_diag(pw_f[c, q]))
            pw[c, q] = pw_f[c, q].astype(BF16)

    def stage_round(group):
        prod = {}
        for c, q in group:
            prod[c, q] = _dot(pw[c, q], jnp.concatenate([pair_diag(pw_f[c, q]), pair_diag(n_sum[c, q])], axis=1))
        for c, q in group:
            n_sum[c, q] = n_sum[c, q] + pw_f[c, q] + prod[c, q][:, 128:256]
            pw_f[c, q] = prod[c, q][:, 0:128]
            pw[c, q] = pw_f[c, q].astype(BF16)

    def stage_solve(group):
        for c, q in group:
            n_sum[c, q] = n_sum[c, q] + pw_f[c, q] + _dot(pw[c, q], pair_diag(n_sum[c, q]))
        for c, q in group:
            a_s = ops[c, q][0]
            rhs = jnp.concatenate([pair_diag(a_s), pair_diag(lakv[c, q])], axis=1)
            wu[c, q] = jnp.concatenate([a_s, lakv[c, q]], axis=1) + _dot(n_sum[c, q].astype(BF16), rhs)

    def stage_state(c):
        xs, st = {}, {}
        for q in range(npp):
            st[q] = st_ref[q]
            xs[q] = _dot(jnp.concatenate([wu[c, q][:, 0:128].astype(BF16), ops[c, q][1]], axis=0),
                         st[q].astype(BF16))
        for q in range(npp):
            _, _, v_b, v_m, btkt, pc_col = ops[c, q]
            u_s = xs[q][0:c_len, :] + wu[c, q][:, 128:256]
            y_ref[q, rows(c), :] = xs[q][c_len:2 * c_len, :] + _dot(
                l_r[c, q], jnp.concatenate([pair_diag(u_s), v_m], axis=0))
            upd = lax.dot_general(btkt, jnp.concatenate([u_s.astype(BF16), v_b], axis=0),
                                  (((0,), (0,)), ((), ())), preferred_element_type=F32)
            st_ref[q] = pc_col * st[q] + jnp.where(same_head, upd, 0.0)

    stage_decay(insts)
    stage_gram(insts)
    stage_masks(insts)
    stage_first_round(insts)
    for _ in range(int(math.log2(c_len)) - 2):
        stage_round(insts)
    stage_solve(insts)
    for c in range(n_chunks):
        stage_state(c)

    @pl.when(i == pl.num_programs(2) - 1)
    def _():
        n = B_HEAD_DIM
        for q in range(npp):
            full_t = st_ref[q].T
            sout_ref[2 * q] = full_t[0:n, 0:n]
            sout_ref[2 * q + 1] = full_t[n:128, n:128]


def _scan_prompt(r, lw, k, v, kn, bs, tt, npp):
    assert SCAN_CHUNK == B_HEAD_DIM, "the [token, (head, token')] lane packing needs chunk == head dim"
    bsz, n_pairs, t_len, _ = r.shape
    blk = lambda: pl.BlockSpec((None, npp, tt, 128), lambda b, p, i: (b, p, i, 0))
    return pl.pallas_call(
        functools.partial(_scan_kernel, tt, npp),
        grid=(bsz, n_pairs // npp, t_len // tt),
        in_specs=[blk() for _ in range(6)],
        out_specs=[blk(), pl.BlockSpec((None, 2 * npp, B_HEAD_DIM, B_HEAD_DIM), lambda b, p, i: (b, p, 0, 0))],
        out_shape=[jax.ShapeDtypeStruct((bsz, n_pairs, t_len, 128), F32),
                   jax.ShapeDtypeStruct((bsz, 2 * n_pairs, B_HEAD_DIM, B_HEAD_DIM), F32)],
        scratch_shapes=[pltpu.VMEM((npp, 128, 128), F32)],
        compiler_params=pltpu.CompilerParams(dimension_semantics=("parallel", "parallel", "arbitrary"),
                                             vmem_limit_bytes=VMEM_LIMIT),
        name="scan_prompt",
    )(r, lw, k, v, kn, bs)


def _step_kernel(r_ref, lw_ref, k_ref, v_ref, kn_ref, bs_ref, s_ref, y_ref, sout_ref):
    a = -kn_ref[...]
    w = jnp.exp(lw_ref[...])
    kx = k_ref[...]
    r = r_ref[...]
    b = bs_ref[...]
    ys = []
    for i in range(B_HEAD_DIM):
        s0 = s_ref[i]
        sa = jnp.sum(s0 * a, axis=0, keepdims=True)
        s1 = s0 * w + sa * b + v_ref[i:i + 1, :] * kx
        sout_ref[i] = s1
        ys.append(jnp.sum(s1 * r, axis=0, keepdims=True))
    y_ref[...] = jnp.concatenate(ys, axis=0)


def _scan_sample(r, lw, k, v, kn, bs, s0):
    n = r.shape[0]
    vecs = [jnp.transpose(z) for z in (r, lw, k, v, kn, bs)]
    s0_t = jnp.transpose(s0[0], (1, 2, 3, 0))
    vec = lambda: pl.BlockSpec((B_HEAD_DIM, n), lambda h: (h, 0))
    st = lambda: pl.BlockSpec((None, B_HEAD_DIM, B_HEAD_DIM, n), lambda h: (h, 0, 0, 0))
    y_t, s1_t = pl.pallas_call(
        _step_kernel,
        grid=(B_HEADS,),
        in_specs=[vec() for _ in range(6)] + [st()],
        out_specs=[vec(), st()],
        out_shape=[jax.ShapeDtypeStruct((D_MODEL, n), F32), jax.ShapeDtypeStruct(s0_t.shape, F32)],
        compiler_params=pltpu.CompilerParams(dimension_semantics=("parallel",)),
        name="scan_sample",
    )(*vecs, s0_t)
    return jnp.transpose(y_t), jnp.transpose(s1_t, (3, 0, 1, 2))[None]


def _back_kernel(paired, y_ref, bonus_ref, g_ref, gateb_ref, outa_ref, x_ref, lxw_ref, lxb_ref, pb_ref,
                 wo_ref, nfw_ref, rw_ref, rb_ref, ones2_ref, x1_ref, tok_ref, logt_ref):
    tm = x_ref.shape[0]
    n_parts = max(1, tm // BACK_ROWS)
    ones2 = ones2_ref[...]
    inv_n = 1.0 / B_HEAD_DIM
    parts = range(n_parts)
    rows = lambda i: slice(i * (tm // n_parts), (i + 1) * (tm // n_parts))

    if paired:
        y = [jnp.concatenate([y_ref[p, rows(i), :] for p in range(N_PAIRS)], axis=1) for i in parts]
    else:
        y = [y_ref[rows(i), :] for i in parts]
    mean = [_head_sums(y[i], ones2) * inv_n for i in parts]
    yc = [y[i] - mean[i] for i in parts]
    var = [_head_sums(yc[i] * yc[i], ones2) * inv_n for i in parts]
    yb = [((yc[i] * lax.rsqrt(var[i] + GN_EPS) * lxw_ref[...] + lxb_ref[...] + bonus_ref[rows(i), :])
           * g_ref[rows(i), :]).astype(BF16) for i in parts]
    mb = [_dot(yb[i], pb_ref[...]) for i in parts]
    merged = [(outa_ref[rows(i), :] + gateb_ref[rows(i), :] * mb[i]).astype(BF16) for i in parts]
    out = [_dot(merged[i], wo_ref[...]) for i in parts]
    tok = []
    for i in parts:
        x1 = x_ref[rows(i), :] + out[i]
        x1_ref[rows(i), :] = x1
        tok.append(_rms_norm(x1, nfw_ref[...]))
        tok_ref[rows(i), :] = tok[i].astype(BF16)
    for i in parts:
        t_hi = tok[i].astype(BF16)
        t_lo = (tok[i] - t_hi.astype(F32)).astype(BF16)
        logt_ref[:, rows(i)] = lax.dot_general(
            rw_ref[...], jnp.concatenate([t_hi, t_hi, t_lo], axis=1), (((1,), (1,)), ((), ())),
            preferred_element_type=F32) + rb_ref[...]


def _back(paired, y, bonus, g, gateb, outa, x, w, tm):
    n = x.shape[0]
    consts = [w["lxw"], w["lxb"], w["pb"], w["wo"], w["nfw"], w["rw3"], w["rbt"], w["ones2"]]
    tok = lambda: pl.BlockSpec((tm, D_MODEL), lambda i: (i, 0))
    if paired:
        nt = y.shape[2] // tm
        y_spec = pl.BlockSpec((None, N_PAIRS, tm, 128), lambda i: (i // nt, 0, i % nt, 0))
    else:
        y_spec = tok()
    return pl.pallas_call(
        functools.partial(_back_kernel, paired),
        grid=(n // tm,),
        in_specs=[y_spec] + [tok() for _ in range(5)] + [_const_spec(c.shape, 1) for c in consts],
        out_specs=[tok(), tok(), pl.BlockSpec((ROUTER_ROWS, tm), lambda i: (0, i))],
        out_shape=[jax.ShapeDtypeStruct((n, D_MODEL), F32), jax.ShapeDtypeStruct((n, D_MODEL), BF16),
                   jax.ShapeDtypeStruct((ROUTER_ROWS, n), F32)],
        compiler_params=pltpu.CompilerParams(dimension_semantics=("parallel",), vmem_limit_bytes=VMEM_LIMIT),
        name="back_prompt" if paired else "back_sample",
    )(y, bonus, g, gateb, outa, x, *consts)


def _moe_slots(ts):
    n = 2 * ts + N_EXPERTS * (SEG_ALIGN - 1)
    return -(-n // MOE_WINDOW) * MOE_WINDOW


def _split3_f32(x):
    hi = x.astype(BF16).astype(F32)
    mid = (x - hi).astype(BF16).astype(F32)
    lo = (x - hi - mid).astype(BF16).astype(F32)
    return hi, mid, lo


def _moe_kernel(ts, ns, tok_ref, logt_ref, x1_ref, utri_ref, wg_ref, wu_ref, wd_ref, fw_ref, o_ref,
                xy_ref, pg_ref, ws_ref, meta_ref):
    n_slots = 2 * xy_ref.shape[1]
    half_w = MOE_WINDOW // 2
    n_esteps = N_EXPERTS // MOE_EXPERTS_PER_STEP
    step = pl.program_id(1)
    tn = (((1,), (1,)), ((), ()))

    @pl.when(step < ns)
    def _route_and_sort():
        lt = logt_ref[...]
        neg = jnp.float32(-jnp.inf)
        big = jnp.float32(99.0)
        row8 = lax.broadcasted_iota(jnp.int32, (8, ts), 0).astype(F32)
        is_grp = row8 < N_GROUPS
        lg = jnp.where(is_grp, lt[0:8, :], neg)
        gmax = jnp.max(lg, axis=0, keepdims=True)
        grp = jnp.min(jnp.where(lg == gmax, row8, big), axis=0, keepdims=True)
        p_grp = 1.0 / jnp.sum(jnp.where(is_grp, jnp.exp(lg - gmax), 0.0), axis=0, keepdims=True)
        le = lt[EXPERT_ROW0:EXPERT_ROW0 + EXPERTS_PER_GROUP, :]
        for g in range(1, N_GROUPS):
            lo = EXPERT_ROW0 + g * EXPERTS_PER_GROUP
            le = jnp.where(grp == g, lt[lo:lo + EXPERTS_PER_GROUP, :], le)
        top1 = jnp.max(le, axis=0, keepdims=True)
        i1 = jnp.min(jnp.where(le == top1, row8, big), axis=0, keepdims=True)
        le2 = jnp.where(row8 == i1, neg, le)
        top2 = jnp.max(le2, axis=0, keepdims=True)
        i2 = jnp.min(jnp.where(le2 == top2, row8, big), axis=0, keepdims=True)
        e2 = jnp.exp(top2 - top1)
        w1 = p_grp / (1.0 + e2)
        w2 = p_grp * e2 / (1.0 + e2)

        row_e = lax.broadcasted_iota(jnp.int32, (N_EXPERTS, ts), 0).astype(F32)
        a1 = row_e == grp * EXPERTS_PER_GROUP + i1
        a2 = row_e == grp * EXPERTS_PER_GROUP + i2
        at = jnp.where(a1, 1.0, 0.0) + jnp.where(a2, 1.0, 0.0)
        rank = _dot(at.astype(BF16), utri_ref[...])
        cnt = rank[:, ts - 1:ts] + at[:, ts - 1:ts]
        cntp = jnp.floor((cnt + (SEG_ALIGN - 1)) * (1.0 / SEG_ALIGN)) * SEG_ALIGN
        cntp_b = jnp.broadcast_to(cntp, (N_EXPERTS, 128))
        r_e = lax.broadcasted_iota(jnp.int32, (N_EXPERTS, N_EXPERTS), 0)
        c_e = lax.broadcasted_iota(jnp.int32, (N_EXPERTS, N_EXPERTS), 1)
        off = _dot(jnp.where(r_e > c_e, 1.0, 0.0).astype(BF16), cntp_b.astype(BF16))
        slot = off[:, 0:1] + rank
        s1 = jnp.sum(jnp.where(a1, slot, 0.0), axis=0, keepdims=True)
        s2 = jnp.sum(jnp.where(a2, slot, 0.0), axis=0, keepdims=True)
        srow = lax.broadcasted_iota(jnp.int32, (n_slots, ts), 0).astype(F32)
        pg1 = jnp.where(srow == s1, 1.0, 0.0)
        pg2 = jnp.where(srow == s2, 1.0, 0.0)
        pg = (pg1 + pg2).astype(BF16)
        pg_ref[step] = pg
        xy_ref[step] = pltpu.bitcast(_dot(pg, tok_ref[...]).astype(BF16), jnp.int32)

        wrows = jnp.concatenate(
            [jnp.concatenate([p1, p2], axis=1) for p1, p2 in zip(_split3_f32(w1), _split3_f32(w2))]
            + [jnp.zeros((5, 2 * ts), F32)], axis=0).astype(BF16)
        wsl = lax.dot_general(jnp.concatenate([pg1.astype(BF16), pg2.astype(BF16)], axis=1), wrows, tn,
                              preferred_element_type=F32)
        ws_ref[step] = jnp.broadcast_to(wsl[:, 0:1] + wsl[:, 1:2] + wsl[:, 2:3], (n_slots, 128))
        meta_ref[step, 0:N_EXPERTS, :] = off.astype(jnp.int32)
        meta_ref[step, N_EXPERTS:2 * N_EXPERTS, :] = cntp_b.astype(jnp.int32)

    @pl.when((step >= ns) & (step < ns + n_esteps))
    def _experts():
        tile_row = (lax.broadcasted_iota(jnp.int32, (half_w, D_MODEL), 0) // 8) * SEG_ALIGN
        experts = range(MOE_EXPERTS_PER_STEP)
        offs, cnts = {}, {}
        for el in experts:
            e = (step - ns) * MOE_EXPERTS_PER_STEP + el
            offs[el] = [meta_ref[j, pl.ds(e, 1), :][0, 0] for j in range(ns)]
            cnts[el] = [meta_ref[j, pl.ds(N_EXPERTS + e, 1), :][0, 0] for j in range(ns)]
        n_win = functools.reduce(jnp.maximum, [(c + MOE_WINDOW - 1) // MOE_WINDOW
                                               for el in experts for c in cnts[el]])

        def window(w, carry):
            starts, lhs = {}, {}
            for el in experts:
                for j in range(ns):
                    st = jnp.minimum(offs[el][j] + w * MOE_WINDOW, n_slots - MOE_WINDOW)
                    starts[el, j] = pl.multiple_of(st, SEG_ALIGN)
                lhs[el] = jnp.concatenate(
                    [pltpu.bitcast(xy_ref[j, pl.ds(pl.multiple_of(starts[el, j] // 2, 8), half_w), :], BF16)
                     for j in range(ns)], axis=0)
            gate = {el: _dot(lhs[el], wg_ref[el]) for el in experts}
            up = {el: _dot(lhs[el], wu_ref[el]) for el in experts}
            y = {el: _dot((gate[el] * _sigmoid(gate[el]) * up[el]).astype(BF16), wd_ref[el]) for el in experts}
            for el in experts:
                for j in range(ns):
                    rows = starts[el, j] + tile_row
                    own = (rows >= offs[el][j] + w * MOE_WINDOW) & (rows < offs[el][j] + cnts[el][j])
                    yw = (y[el][j * MOE_WINDOW:(j + 1) * MOE_WINDOW, :]
                          * ws_ref[j, pl.ds(starts[el, j], MOE_WINDOW), 0:1])
                    pltpu.store(xy_ref.at[j, pl.ds(pl.multiple_of(starts[el, j] // 2, 8), half_w), :],
                                pltpu.bitcast(yw.astype(BF16), jnp.int32), mask=own)
            return carry

        lax.fori_loop(0, n_win, window, 0)

    @pl.when(step >= ns + n_esteps)
    def _combine():
        j = step - ns - n_esteps
        c = lax.dot_general(pg_ref[j], pltpu.bitcast(xy_ref[j], BF16), (((0,), (0,)), ((), ())),
                            preferred_element_type=F32)
        o_ref[...] = _rms_norm(x1_ref[...] + c, fw_ref[...])


def _moe(tok, logt, x1, w, ts, ns):
    n = tok.shape[0]
    n_slots = _moe_slots(ts)
    n_esteps = N_EXPERTS // MOE_EXPERTS_PER_STEP
    ne = MOE_EXPERTS_PER_STEP
    sub_in = lambda s, k: s * ns + jnp.clip(k, 0, ns - 1)
    sub_out = lambda s, k: s * ns + jnp.clip(k - ns - n_esteps, 0, ns - 1)
    expert = lambda k: jnp.clip(k - ns, 0, n_esteps - 1)
    utri = (jnp.arange(ts)[:, None] < jnp.arange(ts)[None, :]).astype(BF16)
    return pl.pallas_call(
        functools.partial(_moe_kernel, ts, ns),
        grid=(n // (ts * ns), 2 * ns + n_esteps),
        in_specs=[pl.BlockSpec((ts, D_MODEL), lambda s, k: (sub_in(s, k), 0)),
                  pl.BlockSpec((ROUTER_ROWS, ts), lambda s, k: (0, sub_in(s, k))),
                  pl.BlockSpec((ts, D_MODEL), lambda s, k: (sub_out(s, k), 0)),
                  pl.BlockSpec((ts, ts), lambda s, k: (0, 0)),
                  pl.BlockSpec((ne, D_MODEL, D_EXPERT), lambda s, k: (expert(k), 0, 0)),
                  pl.BlockSpec((ne, D_MODEL, D_EXPERT), lambda s, k: (expert(k), 0, 0)),
                  pl.BlockSpec((ne, D_EXPERT, D_MODEL), lambda s, k: (expert(k), 0, 0)),
                  pl.BlockSpec((1, D_MODEL), lambda s, k: (0, 0))],
        out_specs=pl.BlockSpec((ts, D_MODEL), lambda s, k: (sub_out(s, k), 0)),
        out_shape=jax.ShapeDtypeStruct((n, D_MODEL), F32),
        scratch_shapes=[pltpu.VMEM((ns, n_slots // 2, D_MODEL), jnp.int32), pltpu.VMEM((ns, n_slots, ts), BF16),
                        pltpu.VMEM((ns, n_slots, 128), F32), pltpu.VMEM((ns, 2 * N_EXPERTS, 128), jnp.int32)],
        compiler_params=pltpu.CompilerParams(dimension_semantics=("parallel", "arbitrary"),
                                             vmem_limit_bytes=VMEM_LIMIT),
        name="moe",
    )(tok, logt, x1, utri, w["wg"], w["wu"], w["wd"], w["fw"])


def _prep_weights(norm_mix_w, w_in, sgu_ln_w, sgu_ln_b, sgu_w_s, sgu_b, rwkv_mu, w_lora_up, w_bias,
                  a_lora_up, a_bias, g_lora_up, k_k, k_a, r_k, lnx_w, lnx_b, proj_a, proj_b, w_out,
                  norm_ffn_w, router_group, router_group_bias, router_expert, router_expert_bias,
                  moe_w_gate, moe_w_up, moe_w_down, norm_final_w):
    row = lambda z: z.reshape(1, -1).astype(F32)
    causal = jnp.tril(jnp.ones((CHUNK, CHUNK), dtype=bool))
    zeros_lora = jnp.zeros((64, D_MODEL), F32)
    head = jnp.arange(256) // B_HEAD_DIM
    ones_bd = (head[:, None] == head[None, :]).astype(BF16)
    pad = EXPERT_ROW0 - N_GROUPS
    rw = jnp.transpose(jnp.concatenate([router_group, jnp.zeros((D_MODEL, pad), F32), router_expert],
                                       axis=1))
    rw_hi = rw.astype(BF16)
    return {
        "nw": row(norm_mix_w), "win": w_in.astype(BF16), "lnw": row(sgu_ln_w), "lnb": row(sgu_ln_b),
        "wsc": jnp.where(causal[None], sgu_w_s, 0.0).astype(BF16),
        "bsb": jnp.repeat(jnp.transpose(sgu_b), A_HEAD_DIM, axis=1),
        "ws0": jnp.repeat(sgu_w_s[:, 0, 0], A_HEAD_DIM).reshape(1, -1),
        "bs0": jnp.repeat(sgu_b[:, 0], A_HEAD_DIM).reshape(1, -1),
        "mu": row(rwkv_mu),
        "wl": jnp.concatenate([w_lora_up, zeros_lora], axis=0).astype(BF16), "wb": row(w_bias),
        "al": jnp.concatenate([zeros_lora, a_lora_up], axis=0).astype(BF16), "ab": row(a_bias),
        "gl": g_lora_up.astype(BF16), "kk": row(k_k), "ka": row(k_a), "rk": row(r_k),
        "pa": proj_a.astype(BF16), "ones2": ones_bd,
        "lxw": row(lnx_w), "lxb": row(lnx_b), "pb": proj_b.astype(BF16), "wo": w_out.astype(BF16),
        "nfw": row(norm_ffn_w),
        "rw3": jnp.concatenate([rw_hi, (rw - rw_hi.astype(F32)).astype(BF16), rw_hi], axis=1),
        "rbt": jnp.concatenate([router_group_bias, jnp.zeros((pad,), F32), router_expert_bias]).reshape(-1, 1),
        "wg": moe_w_gate.astype(BF16), "wu": moe_w_up.astype(BF16), "wd": moe_w_down.astype(BF16),
        "fw": row(norm_final_w),
    }


def kernel(x_prompt, x_sample, state_wkv, state_shift, norm_mix_w, w_in, sgu_ln_w, sgu_ln_b, sgu_w_s, sgu_b, rwkv_mu, w_lora_up, w_bias, a_lora_up, a_bias, g_lora_up, k_k, k_a, r_k, lnx_w, lnx_b, proj_a, proj_b, w_out, norm_ffn_w, router_group, router_group_bias, router_expert, router_expert_bias, moe_w_gate, moe_w_up, moe_w_down, norm_final_w):
    layer = [z[0] for z in (norm_mix_w, w_in, sgu_ln_w, sgu_ln_b, sgu_w_s, sgu_b, rwkv_mu, w_lora_up, w_bias,
                            a_lora_up, a_bias, g_lora_up, k_k, k_a, r_k, lnx_w, lnx_b, proj_a, proj_b, w_out,
                            norm_ffn_w, router_group, router_group_bias, router_expert, router_expert_bias,
                            moe_w_gate, moe_w_up, moe_w_down)]
    w = _prep_weights(*layer, norm_final_w)
    bsz, t_len, _ = x_prompt.shape
    n_s = x_sample.shape[0]

    (outa, gateb, g, bonus, r, lw, k, v, kn, bs, last) = _front_prompt(x_prompt, w, FRONT_ROWS)
    y, wkv_p = _scan_prompt(r, lw, k, v, kn, bs, SCAN_ROWS, N_PAIRS)
    flat = lambda z: z.reshape(bsz * t_len, D_MODEL)
    x1, tok, logt = _back(True, y, flat(bonus), flat(g), flat(gateb), flat(outa), flat(x_prompt), w, BACK_TILE)
    y_prompt = _moe(tok, logt, x1, w, MOE_SUBTILE, MOE_SUBTILES).reshape(bsz, t_len, D_MODEL)

    xs2 = x_sample.reshape(n_s, D_MODEL)
    (outa, gateb, g, bonus, r, lw, k, v, kn, bs, cols_s, vn_s) = _front_sample(xs2, state_shift[0], w)
    y, wkv_s = _scan_sample(r, lw, k, v, kn, bs, state_wkv)
    x1, tok, logt = _back(False, y, bonus, g, gateb, outa, xs2, w, n_s)
    y_sample = _moe(tok, logt, x1, w, n_s, 1).reshape(n_s, 1, D_MODEL)

    return (y_prompt, y_sample, wkv_p[None], last.reshape(1, bsz, SHIFT_WIDTH), wkv_s, cols_s[None],
            vn_s.reshape(1, n_s, 1, A_HEADS, A_HEAD_DIM))
```

```python
import functools
import math

import jax
import jax.numpy as jnp
from jax import lax
from jax.experimental import pallas as pl
from jax.experimental.pallas import tpu as pltpu

F32 = jnp.float32
BF16 = jnp.bfloat16

D_MODEL = 1024
CHUNK = 128
A_HEADS = 8
A_HEAD_DIM = 128
B_HEADS = 16
B_HEAD_DIM = 64
N_PAIRS = B_HEADS // 2
SHIFT_WIDTH = 3328
OFF_V = 1024
OFF_SHIFT = 2048
OFF_GATE_A = OFF_SHIFT + SHIFT_WIDTH
OFF_GATE_B = OFF_GATE_A + D_MODEL
N_GROUPS = 4
EXPERTS_PER_GROUP = 8
N_EXPERTS = 32
D_EXPERT = 256
ROUTER_ROWS = 40
EXPERT_ROW0 = 8
SEG_ALIGN = 16
BACK_ROWS = 256
MOE_WINDOW = 48
MOE_EXPERTS_PER_STEP = 4
MOE_WEIGHT_SLOTS = 3
RMS_EPS = 1e-6
LN_EPS = 1e-5
GN_EPS = 64e-5

SCAN_CHUNK = 64
FRONT_BLOCK = 256
VMEM_LIMIT = 56 * 1024 * 1024

FRONT_ROWS = 256
SCAN_ROWS = 512
BACK_TILE = 512
MOE_SUBTILE = 512
MOE_SUBTILES = 4


def _gelu(x):
    return x * (0.5 * (1.0 + jnp.tanh(math.sqrt(2.0 / math.pi) * (x + 0.044715 * (x * x * x)))))


def _sigmoid(x):
    return 1.0 / (1.0 + jnp.exp(-x))


def _softplus(z):
    return jnp.maximum(z, 0.0) + jnp.log(1.0 + jnp.exp(-jnp.abs(z)))


def _rms_norm(x, g):
    return x * lax.rsqrt(jnp.mean(x * x, axis=-1, keepdims=True) + RMS_EPS) * g


def _dot(a, b):
    return jnp.dot(a, b, preferred_element_type=F32)


def _head_sums(z, ones2):
    outs = [_dot(z[:, p * 256:(p + 1) * 256].astype(BF16), ones2) for p in range(z.shape[1] // 256)]
    return outs[0] if len(outs) == 1 else jnp.concatenate(outs, axis=1)


def _front_kernel(is_sample, tm, *refs):
    if is_sample:
        (x_ref, prev_ref, nw_ref, win_ref, lnw_ref, lnb_ref, ws0_ref, bs0_ref, mu_ref, wl_ref, wb_ref,
         al_ref, ab_ref, gl_ref, kk_ref, ka_ref, rk_ref, pa_ref, ones2_ref,
         outa_ref, gateb_ref, g_ref, bonus_ref, r_ref, lw_ref, k_ref, v_ref, kn_ref, bs_ref,
         cols_ref, vn_ref, ya_ref) = refs
    else:
        (x_ref, nw_ref, win_ref, lnw_ref, lnb_ref, wsc_ref, bsb_ref, mu_ref, wl_ref, wb_ref,
         al_ref, ab_ref, gl_ref, kk_ref, ka_ref, rk_ref, pa_ref, ones2_ref,
         outa_ref, gateb_ref, g_ref, bonus_ref, r_ref, lw_ref, k_ref, v_ref, kn_ref, bs_ref,
         last_ref, ya_ref, carry_ref) = refs

        @pl.when(pl.program_id(1) == 0)
        def _():
            carry_ref[...] = jnp.zeros_like(carry_ref)

    xb = _rms_norm(x_ref[...], nw_ref[...]).astype(BF16)
    ones2 = ones2_ref[...]

    def proj(lo, width):
        return _dot(xb, win_ref[:, lo:lo + width])

    def shifted(lo, width, cols=None):
        cs = slice(lo, lo + width)
        if cols is None:
            cols = proj(OFF_SHIFT + lo, width)
        if is_sample:
            prev = prev_ref[:, cs]
            cols_ref[:, cs] = cols
        else:
            row = lax.broadcasted_iota(jnp.int32, cols.shape, 0)
            prev = jnp.where(row == 0, carry_ref[0:1, cs], pltpu.roll(cols, 1, 0))
            carry_ref[0:1, cs] = cols[tm - 1:tm, :]
            last_ref[:, cs] = cols[tm - 1:tm, :]
        return cols + (prev - cols) * mu_ref[:, cs]

    tail = shifted(3 * D_MODEL, 256)
    wa = tail[:, 0:128]
    twa = jnp.tanh(wa).astype(BF16)
    wab = wa.astype(BF16)
    sgd = _sigmoid(tail[:, 128:256]).astype(BF16)

    def block_matmuls(j):
        lo = j * FRONT_BLOCK
        cs = slice(lo, lo + FRONT_BLOCK)
        return (proj(OFF_SHIFT + lo, FRONT_BLOCK), proj(OFF_SHIFT + D_MODEL + lo, FRONT_BLOCK),
                proj(OFF_SHIFT + 2 * D_MODEL + lo, FRONT_BLOCK), _dot(twa, wl_ref[:, cs]), _dot(wab, al_ref[:, cs]),
                _dot(sgd, gl_ref[:, cs]), proj(lo, FRONT_BLOCK), proj(OFF_V + lo, FRONT_BLOCK),
                proj(OFF_GATE_B + lo, FRONT_BLOCK))

    n_blocks = D_MODEL // FRONT_BLOCK
    raw = block_matmuls(0)
    gate_a_raw = None
    for j in range(n_blocks):
        lo = j * FRONT_BLOCK
        cs = slice(lo, lo + FRONT_BLOCK)
        r_raw, k_raw, v_raw, lw_dot, a_dot, g_dot, u_raw, va_raw, gb_raw = raw
        if j + 1 < n_blocks:
            raw = block_matmuls(j + 1)
        else:
            gate_a_raw = proj(OFF_GATE_A, D_MODEL)

        r = shifted(lo, FRONT_BLOCK, r_raw)
        k = shifted(D_MODEL + lo, FRONT_BLOCK, k_raw)
        vb = shifted(2 * D_MODEL + lo, FRONT_BLOCK, v_raw)
        w_log = -_softplus(-(wb_ref[:, cs] + lw_dot)) - 0.5
        logw = -jnp.exp(w_log)
        a = _sigmoid(ab_ref[:, cs] + a_dot)
        g_ref[:, cs] = g_dot
        kk = k * kk_ref[:, cs]
        kkn = kk / jnp.maximum(jnp.sqrt(_head_sums(kk * kk, ones2)), 1e-12)
        k2 = k * (1.0 + (a - 1.0) * ka_ref[:, cs])
        bonus_ref[:, cs] = _head_sums(r * k2 * rk_ref[:, cs], ones2) * vb
        bsc = kkn * a
        outs = ((r_ref, r), (lw_ref, logw), (k_ref, k2), (v_ref, vb), (kn_ref, kkn), (bs_ref, bsc))
        for o_ref, val in outs:
            if is_sample:
                o_ref[:, cs] = val
            else:
                for q in range(FRONT_BLOCK // 128):
                    o_ref[lo // 128 + q] = val[:, q * 128:(q + 1) * 128]

        u2 = _gelu(u_raw)
        v2 = _gelu(va_raw)
        vnb = []
        for q in range(FRONT_BLOCK // A_HEAD_DIM):
            h = lo // A_HEAD_DIM + q
            hs = slice(h * A_HEAD_DIM, (h + 1) * A_HEAD_DIM)
            u = u2[:, q * A_HEAD_DIM:(q + 1) * A_HEAD_DIM]
            v = v2[:, q * A_HEAD_DIM:(q + 1) * A_HEAD_DIM]
            mean = jnp.mean(v, axis=-1, keepdims=True)
            vc = v - mean
            var = jnp.mean(vc * vc, axis=-1, keepdims=True)
            vn = vc * lax.rsqrt(var + LN_EPS) * lnw_ref[:, hs] + lnb_ref[:, hs]
            if is_sample:
                vn_ref[:, hs] = vn
                ya_ref[:, hs] = (u * (vn * ws0_ref[:, hs] + bs0_ref[:, hs])).astype(BF16)
            else:
                vnb.append(vn.astype(BF16))
        if not is_sample:
            h0 = lo // A_HEAD_DIM
            w_cat = jnp.concatenate([wsc_ref[h0], wsc_ref[h0 + 1]], axis=1)
            zero = jnp.zeros((CHUNK, A_HEAD_DIM), BF16)
            for c in range(tm // CHUNK):
                rs = slice(c * CHUNK, (c + 1) * CHUNK)
                v_bd = jnp.concatenate([jnp.concatenate([vnb[0][rs, :], zero], axis=1),
                                        jnp.concatenate([zero, vnb[1][rs, :]], axis=1)], axis=0)
                ya_ref[rs, cs] = (u2[rs, :] * (_dot(w_cat, v_bd) + bsb_ref[:, cs])).astype(BF16)

        gateb_ref[:, cs] = _sigmoid(gb_raw)

    outa_ref[...] = _sigmoid(gate_a_raw) * _dot(ya_ref[...], pa_ref[...])


def _const_spec(shape, grid_rank):
    zeros = (0,) * len(shape)
    if grid_rank == 1:
        return pl.BlockSpec(shape, lambda i: zeros, pipeline_mode=pl.Buffered(1))
    return pl.BlockSpec(shape, lambda b, i: zeros, pipeline_mode=pl.Buffered(1))


def _front_prompt(x, w, tm):
    bsz, t_len, _ = x.shape
    nt = t_len // tm
    consts = [w["nw"], w["win"], w["lnw"], w["lnb"], w["wsc"], w["bsb"], w["mu"], w["wl"], w["wb"],
              w["al"], w["ab"], w["gl"], w["kk"], w["ka"], w["rk"], w["pa"], w["ones2"]]
    tok = lambda: pl.BlockSpec((None, tm, D_MODEL), lambda b, i: (b, i, 0))
    pair = lambda: pl.BlockSpec((None, N_PAIRS, tm, 128), lambda b, i: (b, 0, i, 0))
    tok_shape = jax.ShapeDtypeStruct((bsz, t_len, D_MODEL), F32)
    pair_shape = jax.ShapeDtypeStruct((bsz, N_PAIRS, t_len, 128), F32)
    return pl.pallas_call(
        functools.partial(_front_kernel, False, tm),
        grid=(bsz, nt),
        in_specs=[tok()] + [_const_spec(c.shape, 2) for c in consts],
        out_specs=[tok(), tok(), tok(), tok()] + [pair() for _ in range(6)]
        + [pl.BlockSpec((None, 1, SHIFT_WIDTH), lambda b, i: (b, 0, 0))],
        out_shape=[tok_shape] * 4 + [pair_shape] * 6 + [jax.ShapeDtypeStruct((bsz, 1, SHIFT_WIDTH), F32)],
        scratch_shapes=[pltpu.VMEM((tm, D_MODEL), BF16), pltpu.VMEM((8, SHIFT_WIDTH), F32)],
        compiler_params=pltpu.CompilerParams(dimension_semantics=("parallel", "arbitrary"),
                                             vmem_limit_bytes=VMEM_LIMIT),
        name="front_prompt",
    )(x, *consts)


def _front_sample(x, prev, w):
    n = x.shape[0]
    consts = [w["nw"], w["win"], w["lnw"], w["lnb"], w["ws0"], w["bs0"], w["mu"], w["wl"], w["wb"],
              w["al"], w["ab"], w["gl"], w["kk"], w["ka"], w["rk"], w["pa"], w["ones2"]]
    tok = lambda: pl.BlockSpec((n, D_MODEL), lambda i: (0, 0))
    wide = lambda: pl.BlockSpec((n, SHIFT_WIDTH), lambda i: (0, 0))
    tok_shape = jax.ShapeDtypeStruct((n, D_MODEL), F32)
    return pl.pallas_call(
        functools.partial(_front_kernel, True, n),
        grid=(1,),
        in_specs=[tok(), wide()] + [_const_spec(c.shape, 1) for c in consts],
        out_specs=[tok() for _ in range(10)] + [wide(), tok()],
        out_shape=[tok_shape] * 10 + [jax.ShapeDtypeStruct((n, SHIFT_WIDTH), F32), tok_shape],
        scratch_shapes=[pltpu.VMEM((n, D_MODEL), BF16)],
        compiler_params=pltpu.CompilerParams(dimension_semantics=("arbitrary",),
                                             vmem_limit_bytes=VMEM_LIMIT),
        name="front_sample",
    )(x, prev, *consts)


def _split3(x):
    hi = x.astype(BF16)
    r1 = x - hi.astype(F32)
    mid = r1.astype(BF16)
    lo = (r1 - mid.astype(F32)).astype(BF16)
    return hi, mid, lo


def _scan_kernel(tt, npp, r_ref, lw_ref, k_ref, v_ref, kn_ref, bs_ref, y_ref, sout_ref, st_ref):
    c_len = SCAN_CHUNK
    i = pl.program_id(2)

    @pl.when(i == 0)
    def _():
        st_ref[...] = jnp.zeros_like(st_ref)

    lane = lax.broadcasted_iota(jnp.int32, (c_len, 128), 1)
    head0 = lane < B_HEAD_DIM
    tpos = lax.broadcasted_iota(jnp.int32, (c_len, 128), 0)
    spos = jnp.bitwise_and(lane, c_len - 1)
    strict = tpos > spos
    incl = tpos >= spos
    row2 = lax.broadcasted_iota(jnp.int32, (128, 128), 0)
    col2 = lax.broadcasted_iota(jnp.int32, (128, 128), 1)
    eye = row2 == col2
    same_head = (row2 < B_HEAD_DIM) == (col2 < B_HEAD_DIM)
    rc = lax.broadcasted_iota(jnp.int32, (c_len, 3 * c_len), 0)
    cc = jnp.bitwise_and(lax.broadcasted_iota(jnp.int32, (c_len, 3 * c_len), 1), c_len - 1)
    tri3 = jnp.where(rc >= cc, 1.0, 0.0).astype(BF16)

    def pair_diag(z):
        return jnp.concatenate([jnp.where(head0, z, 0.0), jnp.where(head0, 0.0, z)], axis=0).astype(BF16)

    n_chunks = tt // c_len
    insts = [(c, q) for c in range(n_chunks) for q in range(npp)]
    rows = lambda c: slice(c * c_len, (c + 1) * c_len)

    log_p, ops, gram, n_sum, pw, pw_f, l_r, lakv, wu = ({} for _ in range(9))

    def stage_decay(group):
        for c, q in group:
            log_p[c, q] = _dot(tri3, jnp.concatenate(_split3(lw_ref[q, rows(c), :]), axis=0))

    def stage_gram(group):
        for c, q in group:
            lp = log_p[c, q]
            kn = kn_ref[q, rows(c), :]
            bs = bs_ref[q, rows(c), :]
            kx = k_ref[q, rows(c), :]
            log_pc = lp[c_len - 1:c_len, :]
            e_inv = jnp.exp(-lp)
            e_dec = jnp.exp(log_pc - lp)
            a_s = -kn * jnp.exp(lp - lw_ref[q, rows(c), :])
            r_s = (r_ref[q, rows(c), :] * jnp.exp(lp)).astype(BF16)
            v_s = v_ref[q, rows(c), :]
            btkt = jnp.concatenate([bs * e_dec, kx * e_dec], axis=0).astype(BF16)
            pc_col = jnp.sum(jnp.where(eye, jnp.exp(log_pc), 0.0), axis=1, keepdims=True)
            ops[c, q] = (a_s, r_s, v_s.astype(BF16), pair_diag(v_s), btkt, pc_col)
            gram[c, q] = lax.dot_general(
                jnp.concatenate([a_s.astype(BF16), r_s], axis=0),
                jnp.concatenate([pair_diag(bs * e_inv), pair_diag(kx * e_inv)], axis=0),
                (((1,), (1,)), ((), ())), preferred_element_type=F32)

    def stage_masks(group):
        for c, q in group:
            g = gram[c, q]
            l_ab = jnp.where(strict, g[0:c_len, 0:128], 0.0)
            n_sum[c, q] = l_ab
            pw_f[c, q] = l_ab
            pw[c, q] = l_ab.astype(BF16)
            l_r[c, q] = jnp.concatenate([jnp.where(incl, g[c_len:2 * c_len, 0:128], 0.0),
                                         jnp.where(incl, g[c_len:2 * c_len, 128:256], 0.0)], axis=1).astype(BF16)
            lakv[c, q] = _dot(jnp.where(strict, g[0:c_len, 128:256], 0.0).astype(BF16), ops[c, q][3])

    def stage_first_round(group):
        for c, q in group:
            pw_f[c, q] = _dot(pw[c, q], pair_diag(pw_f[c, q]))
            pw[c, q] = pw_f[c, q].astype(BF16)

    def stage_round(group):
        prod = {}
        for c, q in group:
            prod[c, q] = _dot(pw[c, q], jnp.concatenate([pair_diag(pw_f[c, q]), pair_diag(n_sum[c, q])], axis=1))
        for c, q in group:
            n_sum[c, q] = n_sum[c, q] + pw_f[c, q] + prod[c, q][:, 128:256]
            pw_f[c, q] = prod[c, q][:, 0:128]
            pw[c, q] = pw_f[c, q].astype(BF16)

    def stage_solve(group):
        for c, q in group:
            n_sum[c, q] = n_sum[c, q] + pw_f[c, q] + _dot(pw[c, q], pair_diag(n_sum[c, q]))
        for c, q in group:
            a_s = ops[c, q][0]
            rhs = jnp.concatenate([pair_diag(a_s), pair_diag(lakv[c, q])], axis=1)
            wu[c, q] = jnp.concatenate([a_s, lakv[c, q]], axis=1) + _dot(n_sum[c, q].astype(BF16), rhs)

    def stage_state(c):
        xs, st = {}, {}
        for q in range(npp):
            st[q] = st_ref[q]
            xs[q] = _dot(jnp.concatenate([wu[c, q][:, 0:128].astype(BF16), ops[c, q][1]], axis=0),
                         st[q].astype(BF16))
        for q in range(npp):
            _, _, v_b, v_m, btkt, pc_col = ops[c, q]
            u_s = xs[q][0:c_len, :] + wu[c, q][:, 128:256]
            y_ref[q, rows(c), :] = xs[q][c_len:2 * c_len, :] + _dot(
                l_r[c, q], jnp.concatenate([pair_diag(u_s), v_m], axis=0))
            upd = lax.dot_general(btkt, jnp.concatenate([u_s.astype(BF16), v_b], axis=0),
                                  (((0,), (0,)), ((), ())), preferred_element_type=F32)
            st_ref[q] = pc_col * st[q] + jnp.where(same_head, upd, 0.0)

    stage_decay(insts)
    stage_gram(insts)
    stage_masks(insts)
    stage_first_round(insts)
    for _ in range(int(math.log2(c_len)) - 2):
        stage_round(insts)
    stage_solve(insts)
    for c in range(n_chunks):
        stage_state(c)

    @pl.when(i == pl.num_programs(2) - 1)
    def _():
        n = B_HEAD_DIM
        for q in range(npp):
            full_t = st_ref[q].T
            sout_ref[2 * q] = full_t[0:n, 0:n]
            sout_ref[2 * q + 1] = full_t[n:128, n:128]


def _scan_prompt(r, lw, k, v, kn, bs, tt, npp):
    assert SCAN_CHUNK == B_HEAD_DIM, "the [token, (head, token')] lane packing needs chunk == head dim"
    bsz, n_pairs, t_len, _ = r.shape
    blk = lambda: pl.BlockSpec((None, npp, tt, 128), lambda b, p, i: (b, p, i, 0))
    return pl.pallas_call(
        functools.partial(_scan_kernel, tt, npp),
        grid=(bsz, n_pairs // npp, t_len // tt),
        in_specs=[blk() for _ in range(6)],
        out_specs=[blk(), pl.BlockSpec((None, 2 * npp, B_HEAD_DIM, B_HEAD_DIM), lambda b, p, i: (b, p, 0, 0))],
        out_shape=[jax.ShapeDtypeStruct((bsz, n_pairs, t_len, 128), F32),
                   jax.ShapeDtypeStruct((bsz, 2 * n_pairs, B_HEAD_DIM, B_HEAD_DIM), F32)],
        scratch_shapes=[pltpu.VMEM((npp, 128, 128), F32)],
        compiler_params=pltpu.CompilerParams(dimension_semantics=("parallel", "parallel", "arbitrary"),
                                             vmem_limit_bytes=VMEM_LIMIT),
        name="scan_prompt",
    )(r, lw, k, v, kn, bs)


def _step_kernel(r_ref, lw_ref, k_ref, v_ref, kn_ref, bs_ref, s_ref, y_ref, sout_ref):
    a = -kn_ref[...]
    w = jnp.exp(lw_ref[...])
    kx = k_ref[...]
    r = r_ref[...]
    b = bs_ref[...]
    ys = []
    for i in range(B_HEAD_DIM):
        s0 = s_ref[i]
        sa = jnp.sum(s0 * a, axis=0, keepdims=True)
        s1 = s0 * w + sa * b + v_ref[i:i + 1, :] * kx
        sout_ref[i] = s1
        ys.append(jnp.sum(s1 * r, axis=0, keepdims=True))
    y_ref[...] = jnp.concatenate(ys, axis=0)


def _scan_sample(r, lw, k, v, kn, bs, s0):
    n = r.shape[0]
    vecs = [jnp.transpose(z) for z in (r, lw, k, v, kn, bs)]
    s0_t = jnp.transpose(s0[0], (1, 2, 3, 0))
    vec = lambda: pl.BlockSpec((B_HEAD_DIM, n), lambda h: (h, 0))
    st = lambda: pl.BlockSpec((None, B_HEAD_DIM, B_HEAD_DIM, n), lambda h: (h, 0, 0, 0))
    y_t, s1_t = pl.pallas_call(
        _step_kernel,
        grid=(B_HEADS,),
        in_specs=[vec() for _ in range(6)] + [st()],
        out_specs=[vec(), st()],
        out_shape=[jax.ShapeDtypeStruct((D_MODEL, n), F32), jax.ShapeDtypeStruct(s0_t.shape, F32)],
        compiler_params=pltpu.CompilerParams(dimension_semantics=("parallel",)),
        name="scan_sample",
    )(*vecs, s0_t)
    return jnp.transpose(y_t), jnp.transpose(s1_t, (3, 0, 1, 2))[None]


def _back_kernel(paired, y_ref, bonus_ref, g_ref, gateb_ref, outa_ref, x_ref, lxw_ref, lxb_ref, pb_ref,
                 wo_ref, nfw_ref, rw_ref, rb_ref, ones2_ref, x1_ref, tok_ref, logt_ref):
    tm = x_ref.shape[0]
    n_parts = max(1, tm // BACK_ROWS)
    ones2 = ones2_ref[...]
    inv_n = 1.0 / B_HEAD_DIM
    parts = range(n_parts)
    rows = lambda i: slice(i * (tm // n_parts), (i + 1) * (tm // n_parts))

    if paired:
        y = [jnp.concatenate([y_ref[p, rows(i), :] for p in range(N_PAIRS)], axis=1) for i in parts]
    else:
        y = [y_ref[rows(i), :] for i in parts]
    mean = [_head_sums(y[i], ones2) * inv_n for i in parts]
    yc = [y[i] - mean[i] for i in parts]
    var = [_head_sums(yc[i] * yc[i], ones2) * inv_n for i in parts]
    yb = [((yc[i] * lax.rsqrt(var[i] + GN_EPS) * lxw_ref[...] + lxb_ref[...] + bonus_ref[rows(i), :])
           * g_ref[rows(i), :]).astype(BF16) for i in parts]
    mb = [_dot(yb[i], pb_ref[...]) for i in parts]
    merged = [(outa_ref[rows(i), :] + gateb_ref[rows(i), :] * mb[i]).astype(BF16) for i in parts]
    out = [_dot(merged[i], wo_ref[...]) for i in parts]
    tok = []
    for i in parts:
        x1 = x_ref[rows(i), :] + out[i]
        x1_ref[rows(i), :] = x1
        tok.append(_rms_norm(x1, nfw_ref[...]))
        tok_ref[rows(i), :] = tok[i].astype(BF16)
    for i in parts:
        t_hi = tok[i].astype(BF16)
        t_lo = (tok[i] - t_hi.astype(F32)).astype(BF16)
        logt_ref[:, rows(i)] = lax.dot_general(
            rw_ref[...], jnp.concatenate([t_hi, t_hi, t_lo], axis=1), (((1,), (1,)), ((), ())),
            preferred_element_type=F32) + rb_ref[...]


def _back(paired, y, bonus, g, gateb, outa, x, w, tm):
    n = x.shape[0]
    consts = [w["lxw"], w["lxb"], w["pb"], w["wo"], w["nfw"], w["rw3"], w["rbt"], w["ones2"]]
    tok = lambda: pl.BlockSpec((tm, D_MODEL), lambda i: (i, 0))
    if paired:
        nt = y.shape[2] // tm
        y_spec = pl.BlockSpec((None, N_PAIRS, tm, 128), lambda i: (i // nt, 0, i % nt, 0))
    else:
        y_spec = tok()
    return pl.pallas_call(
        functools.partial(_back_kernel, paired),
        grid=(n // tm,),
        in_specs=[y_spec] + [tok() for _ in range(5)] + [_const_spec(c.shape, 1) for c in consts],
        out_specs=[tok(), tok(), pl.BlockSpec((ROUTER_ROWS, tm), lambda i: (0, i))],
        out_shape=[jax.ShapeDtypeStruct((n, D_MODEL), F32), jax.ShapeDtypeStruct((n, D_MODEL), BF16),
                   jax.ShapeDtypeStruct((ROUTER_ROWS, n), F32)],
        compiler_params=pltpu.CompilerParams(dimension_semantics=("parallel",), vmem_limit_bytes=VMEM_LIMIT),
        name="back_prompt" if paired else "back_sample",
    )(y, bonus, g, gateb, outa, x, *consts)


def _moe_slots(ts):
    n = 2 * ts + N_EXPERTS * (SEG_ALIGN - 1)
    return -(-n // MOE_WINDOW) * MOE_WINDOW


def _split3_f32(x):
    hi = x.astype(BF16).astype(F32)
    mid = (x - hi).astype(BF16).astype(F32)
    lo = (x - hi - mid).astype(BF16).astype(F32)
    return hi, mid, lo


def _moe_kernel(ts, ns, tok_ref, logt_ref, x1_ref, utri_ref, wg_hbm, wu_hbm, wd_hbm, fw_ref, o_ref,
                xy_ref, pg_ref, ws_ref, meta_ref, wg_ref, wu_ref, wd_ref, w_sem):
    def weight_copies(group, slot):
        lo = group * MOE_EXPERTS_PER_STEP
        return [pltpu.make_async_copy(src.at[pl.ds(lo, MOE_EXPERTS_PER_STEP)], dst.at[slot], w_sem.at[slot, i])
                for i, (src, dst) in enumerate(((wg_hbm, wg_ref), (wu_hbm, wu_ref), (wd_hbm, wd_ref)))]

    @pl.when(pl.program_id(1) == 0)
    def _start_weights():
        for g in range(MOE_WEIGHT_SLOTS):
            for cp in weight_copies(g, g):
                cp.start()

    n_slots = 2 * xy_ref.shape[1]
    half_w = MOE_WINDOW // 2
    n_esteps = N_EXPERTS // MOE_EXPERTS_PER_STEP
    step = pl.program_id(1)
    tn = (((1,), (1,)), ((), ()))

    @pl.when(step < ns)
    def _route_and_sort():
        lt = logt_ref[...]
        neg = jnp.float32(-jnp.inf)
        big = jnp.float32(99.0)
        row8 = lax.broadcasted_iota(jnp.int32, (8, ts), 0).astype(F32)
        is_grp = row8 < N_GROUPS
        lg = jnp.where(is_grp, lt[0:8, :], neg)
        gmax = jnp.max(lg, axis=0, keepdims=True)
        grp = jnp.min(jnp.where(lg == gmax, row8, big), axis=0, keepdims=True)
        p_grp = 1.0 / jnp.sum(jnp.where(is_grp, jnp.exp(lg - gmax), 0.0), axis=0, keepdims=True)
        le = lt[EXPERT_ROW0:EXPERT_ROW0 + EXPERTS_PER_GROUP, :]
        for g in range(1, N_GROUPS):
            lo = EXPERT_ROW0 + g * EXPERTS_PER_GROUP
            le = jnp.where(grp == g, lt[lo:lo + EXPERTS_PER_GROUP, :], le)
        top1 = jnp.max(le, axis=0, keepdims=True)
        i1 = jnp.min(jnp.where(le == top1, row8, big), axis=0, keepdims=True)
        le2 = jnp.where(row8 == i1, neg, le)
        top2 = jnp.max(le2, axis=0, keepdims=True)
        i2 = jnp.min(jnp.where(le2 == top2, row8, big), axis=0, keepdims=True)
        e2 = jnp.exp(top2 - top1)
        w1 = p_grp / (1.0 + e2)
        w2 = p_grp * e2 / (1.0 + e2)

        row_e = lax.broadcasted_iota(jnp.int32, (N_EXPERTS, ts), 0).astype(F32)
        a1 = row_e == grp * EXPERTS_PER_GROUP + i1
        a2 = row_e == grp * EXPERTS_PER_GROUP + i2
        at = jnp.where(a1, 1.0, 0.0) + jnp.where(a2, 1.0, 0.0)
        rank = _dot(at.astype(BF16), utri_ref[...])
        cnt = rank[:, ts - 1:ts] + at[:, ts - 1:ts]
        cntp = jnp.floor((cnt + (SEG_ALIGN - 1)) * (1.0 / SEG_ALIGN)) * SEG_ALIGN
        cntp_b = jnp.broadcast_to(cntp, (N_EXPERTS, 128))
        r_e = lax.broadcasted_iota(jnp.int32, (N_EXPERTS, N_EXPERTS), 0)
        c_e = lax.broadcasted_iota(jnp.int32, (N_EXPERTS, N_EXPERTS), 1)
        off = _dot(jnp.where(r_e > c_e, 1.0, 0.0).astype(BF16), cntp_b.astype(BF16))
        slot = off[:, 0:1] + rank
        s1 = jnp.sum(jnp.where(a1, slot, 0.0), axis=0, keepdims=True)
        s2 = jnp.sum(jnp.where(a2, slot, 0.0), axis=0, keepdims=True)
        srow = lax.broadcasted_iota(jnp.int32, (n_slots, ts), 0).astype(F32)
        pg1 = jnp.where(srow == s1, 1.0, 0.0)
        pg2 = jnp.where(srow == s2, 1.0, 0.0)
        pg = (pg1 + pg2).astype(BF16)
        pg_ref[step] = pg
        xy_ref[step] = pltpu.bitcast(_dot(pg, tok_ref[...]).astype(BF16), jnp.int32)

        wrows = jnp.concatenate(
            [jnp.concatenate([p1, p2], axis=1) for p1, p2 in zip(_split3_f32(w1), _split3_f32(w2))]
            + [jnp.zeros((5, 2 * ts), F32)], axis=0).astype(BF16)
        wsl = lax.dot_general(jnp.concatenate([pg1.astype(BF16), pg2.astype(BF16)], axis=1), wrows, tn,
                              preferred_element_type=F32)
        ws_ref[step] = jnp.broadcast_to(wsl[:, 0:1] + wsl[:, 1:2] + wsl[:, 2:3], (n_slots, 128))
        meta_ref[step, 0:N_EXPERTS, :] = off.astype(jnp.int32)
        meta_ref[step, N_EXPERTS:2 * N_EXPERTS, :] = cntp_b.astype(jnp.int32)

    @pl.when((step >= ns) & (step < ns + n_esteps))
    def _experts():
        tile_row = (lax.broadcasted_iota(jnp.int32, (half_w, D_MODEL), 0) // 8) * SEG_ALIGN
        experts = range(MOE_EXPERTS_PER_STEP)
        group = step - ns
        slot = lax.rem(group, MOE_WEIGHT_SLOTS)
        offs, cnts = {}, {}
        for el in experts:
            e = (step - ns) * MOE_EXPERTS_PER_STEP + el
            offs[el] = [meta_ref[j, pl.ds(e, 1), :][0, 0] for j in range(ns)]
            cnts[el] = [meta_ref[j, pl.ds(N_EXPERTS + e, 1), :][0, 0] for j in range(ns)]
        n_win = functools.reduce(jnp.maximum, [(c + MOE_WINDOW - 1) // MOE_WINDOW
                                               for el in experts for c in cnts[el]])

        def window(w, carry):
            starts, lhs = {}, {}
            for el in experts:
                for j in range(ns):
                    st = jnp.minimum(offs[el][j] + w * MOE_WINDOW, n_slots - MOE_WINDOW)
                    starts[el, j] = pl.multiple_of(st, SEG_ALIGN)
                lhs[el] = jnp.concatenate(
                    [pltpu.bitcast(xy_ref[j, pl.ds(pl.multiple_of(starts[el, j] // 2, 8), half_w), :], BF16)
                     for j in range(ns)], axis=0)
            gate = {el: _dot(lhs[el], wg_ref[slot, el]) for el in experts}
            up = {el: _dot(lhs[el], wu_ref[slot, el]) for el in experts}
            y = {el: _dot((gate[el] * _sigmoid(gate[el]) * up[el]).astype(BF16), wd_ref[slot, el])
                 for el in experts}
            for el in experts:
                for j in range(ns):
                    rows = starts[el, j] + tile_row
                    own = (rows >= offs[el][j] + w * MOE_WINDOW) & (rows < offs[el][j] + cnts[el][j])
                    yw = (y[el][j * MOE_WINDOW:(j + 1) * MOE_WINDOW, :]
                          * ws_ref[j, pl.ds(starts[el, j], MOE_WINDOW), 0:1])
                    pltpu.store(xy_ref.at[j, pl.ds(pl.multiple_of(starts[el, j] // 2, 8), half_w), :],
                                pltpu.bitcast(yw.astype(BF16), jnp.int32), mask=own)
            return carry

        for cp in weight_copies(group, slot):
            cp.wait()
        lax.fori_loop(0, n_win, window, 0)

        @pl.when(group + MOE_WEIGHT_SLOTS < n_esteps)
        def _refill():
            for cp in weight_copies(group + MOE_WEIGHT_SLOTS, slot):
                cp.start()

    @pl.when(step >= ns + n_esteps)
    def _combine():
        j = step - ns - n_esteps
        c = lax.dot_general(pg_ref[j], pltpu.bitcast(xy_ref[j], BF16), (((0,), (0,)), ((), ())),
                            preferred_element_type=F32)
        o_ref[...] = _rms_norm(x1_ref[...] + c, fw_ref[...])


def _moe(tok, logt, x1, w, ts, ns):
    n = tok.shape[0]
    n_slots = _moe_slots(ts)
    n_esteps = N_EXPERTS // MOE_EXPERTS_PER_STEP
    ne = MOE_EXPERTS_PER_STEP
    sub_in = lambda s, k: s * ns + jnp.clip(k, 0, ns - 1)
    sub_out = lambda s, k: s * ns + jnp.clip(k - ns - n_esteps, 0, ns - 1)
    assert n_esteps >= MOE_WEIGHT_SLOTS
    slots = MOE_WEIGHT_SLOTS
    utri = (jnp.arange(ts)[:, None] < jnp.arange(ts)[None, :]).astype(BF16)
    return pl.pallas_call(
        functools.partial(_moe_kernel, ts, ns),
        grid=(n // (ts * ns), 2 * ns + n_esteps),
        in_specs=[pl.BlockSpec((ts, D_MODEL), lambda s, k: (sub_in(s, k), 0)),
                  pl.BlockSpec((ROUTER_ROWS, ts), lambda s, k: (0, sub_in(s, k))),
                  pl.BlockSpec((ts, D_MODEL), lambda s, k: (sub_out(s, k), 0)),
                  pl.BlockSpec((ts, ts), lambda s, k: (0, 0)),
                  pl.BlockSpec(memory_space=pl.ANY),
                  pl.BlockSpec(memory_space=pl.ANY),
                  pl.BlockSpec(memory_space=pl.ANY),
                  pl.BlockSpec((1, D_MODEL), lambda s, k: (0, 0))],
        out_specs=pl.BlockSpec((ts, D_MODEL), lambda s, k: (sub_out(s, k), 0)),
        out_shape=jax.ShapeDtypeStruct((n, D_MODEL), F32),
        scratch_shapes=[pltpu.VMEM((ns, n_slots // 2, D_MODEL), jnp.int32), pltpu.VMEM((ns, n_slots, ts), BF16),
                        pltpu.VMEM((ns, n_slots, 128), F32), pltpu.VMEM((ns, 2 * N_EXPERTS, 128), jnp.int32),
                        pltpu.VMEM((slots, ne, D_MODEL, D_EXPERT), BF16), pltpu.VMEM((slots, ne, D_MODEL, D_EXPERT), BF16),
                        pltpu.VMEM((slots, ne, D_EXPERT, D_MODEL), BF16), pltpu.SemaphoreType.DMA((slots, 3))],
        compiler_params=pltpu.CompilerParams(dimension_semantics=("arbitrary", "arbitrary"),
                                             vmem_limit_bytes=VMEM_LIMIT),
        name="moe",
    )(tok, logt, x1, utri, w["wg"], w["wu"], w["wd"], w["fw"])


def _prep_weights(norm_mix_w, w_in, sgu_ln_w, sgu_ln_b, sgu_w_s, sgu_b, rwkv_mu, w_lora_up, w_bias,
                  a_lora_up, a_bias, g_lora_up, k_k, k_a, r_k, lnx_w, lnx_b, proj_a, proj_b, w_out,
                  norm_ffn_w, router_group, router_group_bias, router_expert, router_expert_bias,
                  moe_w_gate, moe_w_up, moe_w_down, norm_final_w):
    row = lambda z: z.reshape(1, -1).astype(F32)
    causal = jnp.tril(jnp.ones((CHUNK, CHUNK), dtype=bool))
    zeros_lora = jnp.zeros((64, D_MODEL), F32)
    head = jnp.arange(256) // B_HEAD_DIM
    ones_bd = (head[:, None] == head[None, :]).astype(BF16)
    pad = EXPERT_ROW0 - N_GROUPS
    rw = jnp.transpose(jnp.concatenate([router_group, jnp.zeros((D_MODEL, pad), F32), router_expert],
                                       axis=1))
    rw_hi = rw.astype(BF16)
    return {
        "nw": row(norm_mix_w), "win": w_in.astype(BF16), "lnw": row(sgu_ln_w), "lnb": row(sgu_ln_b),
        "wsc": jnp.where(causal[None], sgu_w_s, 0.0).astype(BF16),
        "bsb": jnp.repeat(jnp.transpose(sgu_b), A_HEAD_DIM, axis=1),
        "ws0": jnp.repeat(sgu_w_s[:, 0, 0], A_HEAD_DIM).reshape(1, -1),
        "bs0": jnp.repeat(sgu_b[:, 0], A_HEAD_DIM).reshape(1, -1),
        "mu": row(rwkv_mu),
        "wl": jnp.concatenate([w_lora_up, zeros_lora], axis=0).astype(BF16), "wb": row(w_bias),
        "al": jnp.concatenate([zeros_lora, a_lora_up], axis=0).astype(BF16), "ab": row(a_bias),
        "gl": g_lora_up.astype(BF16), "kk": row(k_k), "ka": row(k_a), "rk": row(r_k),
        "pa": proj_a.astype(BF16), "ones2": ones_bd,
        "lxw": row(lnx_w), "lxb": row(lnx_b), "pb": proj_b.astype(BF16), "wo": w_out.astype(BF16),
        "nfw": row(norm_ffn_w),
        "rw3": jnp.concatenate([rw_hi, (rw - rw_hi.astype(F32)).astype(BF16), rw_hi], axis=1),
        "rbt": jnp.concatenate([router_group_bias, jnp.zeros((pad,), F32), router_expert_bias]).reshape(-1, 1),
        "wg": moe_w_gate.astype(BF16), "wu": moe_w_up.astype(BF16), "wd": moe_w_down.astype(BF16),
        "fw": row(norm_final_w),
    }


def kernel(x_prompt, x_sample, state_wkv, state_shift, norm_mix_w, w_in, sgu_ln_w, sgu_ln_b, sgu_w_s, sgu_b, rwkv_mu, w_lora_up, w_bias, a_lora_up, a_bias, g_lora_up, k_k, k_a, r_k, lnx_w, lnx_b, proj_a, proj_b, w_out, norm_ffn_w, router_group, router_group_bias, router_expert, router_expert_bias, moe_w_gate, moe_w_up, moe_w_down, norm_final_w):
    layer = [z[0] for z in (norm_mix_w, w_in, sgu_ln_w, sgu_ln_b, sgu_w_s, sgu_b, rwkv_mu, w_lora_up, w_bias,
                            a_lora_up, a_bias, g_lora_up, k_k, k_a, r_k, lnx_w, lnx_b, proj_a, proj_b, w_out,
                            norm_ffn_w, router_group, router_group_bias, router_expert, router_expert_bias,
                            moe_w_gate, moe_w_up, moe_w_down)]
    w = _prep_weights(*layer, norm_final_w)
    bsz, t_len, _ = x_prompt.shape
    n_s = x_sample.shape[0]

    (outa, gateb, g, bonus, r, lw, k, v, kn, bs, last) = _front_prompt(x_prompt, w, FRONT_ROWS)
    y, wkv_p = _scan_prompt(r, lw, k, v, kn, bs, SCAN_ROWS, N_PAIRS)
    flat = lambda z: z.reshape(bsz * t_len, D_MODEL)
    x1, tok, logt = _back(True, y, flat(bonus), flat(g), flat(gateb), flat(outa), flat(x_prompt), w, BACK_TILE)
    y_prompt = _moe(tok, logt, x1, w, MOE_SUBTILE, MOE_SUBTILES).reshape(bsz, t_len, D_MODEL)

    xs2 = x_sample.reshape(n_s, D_MODEL)
    (outa, gateb, g, bonus, r, lw, k, v, kn, bs, cols_s, vn_s) = _front_sample(xs2, state_shift[0], w)
    y, wkv_s = _scan_sample(r, lw, k, v, kn, bs, state_wkv)
    x1, tok, logt = _back(False, y, bonus, g, gateb, outa, xs2, w, n_s)
    y_sample = _moe(tok, logt, x1, w, n_s, 1).reshape(n_s, 1, D_MODEL)

    return (y_prompt, y_sample, wkv_p[None], last.reshape(1, bsz, SHIFT_WIDTH), wkv_s, cols_s[None],
            vn_s.reshape(1, n_s, 1, A_HEADS, A_HEAD_DIM))
```
